```python
import jax
import jax.numpy as jnp
from jax import lax
import numpy as np

D_MODEL = 1024
BATCH = 16
SEQ = 256
DEPTH = 4
DEC_BATCH = 4
DEC_SEQ = 1024
PAST_LEN = 512

GRID_W = 64
N_EVEN = (DEPTH + 1) // 2
N_ODD = DEPTH // 2
ALPHA = (2.0 * DEPTH) ** 0.25
BETA = (8.0 * DEPTH) ** -0.25
LN_EPS = 1e-5
RMS_EPS = 1e-6
ROPE_BASE = 10000.0
Q_BLOCK = 128
NEG_INF = -1e30
MLA_HEADS = 8
MLA_NOPE_DIM = 64
MLA_ROPE_DIM = 32
MLA_V_DIM = 64
MLA_Q_RANK = 256
MLA_KV_RANK = 128
RET_HEADS = 8
RET_HEAD_DIM = 64
RET_CHUNK = 128
RET_WIDTH = RET_HEADS * RET_HEAD_DIM
MLA_IN_DIM = MLA_Q_RANK + MLA_KV_RANK + MLA_ROPE_DIM
EVEN_IN_DIM = MLA_IN_DIM + 4 * RET_WIDTH
EVEN_SPLITS = (MLA_Q_RANK, MLA_Q_RANK + MLA_KV_RANK, MLA_IN_DIM, MLA_IN_DIM + RET_WIDTH,
               MLA_IN_DIM + 2 * RET_WIDTH, MLA_IN_DIM + 3 * RET_WIDTH)
EVEN_MIX_DIM = MLA_HEADS * MLA_V_DIM + RET_WIDTH
SWA_HEADS = 16
SWA_KV_HEADS = 4
SWA_GROUP = SWA_HEADS // SWA_KV_HEADS
SWA_HEAD_DIM = 64
WINDOW = 128
SWA_Q_DIM = SWA_HEADS * SWA_HEAD_DIM
SWA_KV_DIM = SWA_KV_HEADS * SWA_HEAD_DIM
SWA_IN_DIM = SWA_Q_DIM + 2 * SWA_KV_DIM
N_EXPERTS = 64
TOP_K = 8
N_GROUPS = 8
TOPK_GROUPS = 4
EXPERT_DIM = 256
SHARED_DIM = 256
ROUTED_SCALE = 2.5
EXPERT_BLOCK = 128

kernel_name = 'hybrid_mla_retention_swa_moe_diffusion_step'


def _layernorm(x, g, b):
    xf = x.astype(jnp.float32)
    mu = xf.mean(-1, keepdims=True)
    var = jnp.square(xf - mu).mean(-1, keepdims=True)
    y = (xf - mu) * lax.rsqrt(var + LN_EPS) * g.astype(jnp.float32) + b.astype(jnp.float32)
    return y.astype(x.dtype)


def _rmsnorm(x, g):
    xf = x.astype(jnp.float32)
    y = xf * lax.rsqrt(jnp.mean(xf * xf, -1, keepdims=True) + RMS_EPS) * g.astype(jnp.float32)
    return y.astype(x.dtype)


def _head_norm(x):
    xf = x.astype(jnp.float32)
    mu = xf.mean(-1, keepdims=True)
    var = jnp.square(xf - mu).mean(-1, keepdims=True)
    return ((xf - mu) * lax.rsqrt(var + LN_EPS)).astype(x.dtype)


def _modulation(cond, w, b):
    m = jax.nn.silu(cond) @ w + b
    return jnp.split(m[:, None, :], 6, axis=-1)


def _modulate(x, shift, scale):
    return x * (1.0 + scale) + shift


def _axial_rope(rows, rot_dim):
    nf = rot_dim // 4
    t = jnp.arange(rows * GRID_W)
    r = (t // GRID_W).astype(jnp.float32)
    col = (t % GRID_W).astype(jnp.float32)
    freqs = ROPE_BASE ** (-jnp.arange(nf, dtype=jnp.float32) / nf)
    ang = jnp.concatenate([r[:, None] * freqs, col[:, None] * freqs], axis=-1)
    return jnp.cos(ang), jnp.sin(ang)


def _rope_2d(x, cos, sin):
    t, r = x.shape[1], x.shape[-1]
    nf = r // 4
    xs = x.reshape(x.shape[:-1] + (2, 2, nf))
    x1, x2 = xs[..., 0, :], xs[..., 1, :]
    bshape = (1, t) + (1,) * (x.ndim - 3) + (2, nf)
    c = cos.reshape(bshape).astype(x.dtype)
    s = sin.reshape(bshape).astype(x.dtype)
    return jnp.stack([x1 * c - x2 * s, x2 * c + x1 * s], axis=-2).reshape(x.shape)


def _softmax_with_sink(s, sink):
    if sink is None:
        return jax.nn.softmax(s, axis=-1)
    col = jnp.broadcast_to(sink.astype(jnp.float32).reshape(1, s.shape[1], s.shape[2], 1, 1), s.shape[:-1] + (1,))
    return jax.nn.softmax(jnp.concatenate([s, col], axis=-1), axis=-1)[..., :-1]


def _block_attention(q, k, v, sink):
    b, t, hk, g, dq = q.shape
    nb = t // Q_BLOCK
    scale = dq ** -0.5
    qb = jnp.moveaxis(q.reshape(b, nb, Q_BLOCK, hk, g, dq), 1, 0)

    def one(qi):
        s = jnp.einsum('bqhgd,bshd->bhgqs', qi, k).astype(jnp.float32) * scale
        p = _softmax_with_sink(s, sink).astype(v.dtype)
        return jnp.einsum('bhgqs,bshd->bqhgd', p, v)

    o = lax.map(one, qb)
    return jnp.moveaxis(o, 0, 1).reshape(b, t, hk, g, v.shape[-1])


def _window_attention(q, k, v, k_ctx, v_ctx, sink):
    b, t, hk, g, dh = q.shape
    nb = t // Q_BLOCK
    band = Q_BLOCK + 2 * WINDOW
    scale = dh ** -0.5
    pad = ((0, 0), (WINDOW, WINDOW), (0, 0), (0, 0))
    kp = jnp.pad(k, pad)
    vp = jnp.pad(v, pad)

    def one(i):
        start = i * Q_BLOCK
        qi = lax.dynamic_slice_in_dim(q, start, Q_BLOCK, axis=1)
        ki = lax.dynamic_slice_in_dim(kp, start, band, axis=1)
        vi = lax.dynamic_slice_in_dim(vp, start, band, axis=1)
        qpos = start + jnp.arange(Q_BLOCK)
        kpos = start - WINDOW + jnp.arange(band)
        valid = (jnp.abs(qpos[:, None] - kpos[None, :]) <= WINDOW) & (kpos >= 0)[None, :] & (kpos < t)[None, :]
        s_band = jnp.einsum('bqhgd,bshd->bhgqs', qi, ki).astype(jnp.float32) * scale
        s_band = jnp.where(valid, s_band, NEG_INF)
        s_ctx = jnp.einsum('bqhgd,bshd->bhgqs', qi, k_ctx).astype(jnp.float32) * scale
        p = _softmax_with_sink(jnp.concatenate([s_band, s_ctx], axis=-1), sink).astype(v.dtype)
        return (jnp.einsum('bhgqs,bshd->bqhgd', p[..., :band], vi)
                + jnp.einsum('bhgqs,bshd->bqhgd', p[..., band:], v_ctx))

    o = lax.map(one, jnp.arange(nb))
    return jnp.moveaxis(o, 0, 1).reshape(b, t, hk, g, dh)


def _retention_scan(q, k, v, log_gamma, s0):
    b, t, h, _ = q.shape
    dv = v.shape[-1]
    n = t // RET_CHUNK

    def to_chunks(a):
        return a.astype(jnp.float32).reshape(b, n, RET_CHUNK, h, a.shape[-1]).transpose(1, 0, 3, 2, 4)

    idx = jnp.arange(RET_CHUNK, dtype=jnp.float32)
    diff = idx[:, None] - idx[None, :]
    inner_decay = jnp.where(diff >= 0, jnp.exp(log_gamma[:, None, None] * jnp.maximum(diff, 0.0)), 0.0)
    q_decay = jnp.exp(log_gamma[:, None] * (idx + 1.0))
    k_decay = jnp.exp(log_gamma[:, None] * (RET_CHUNK - 1.0 - idx))
    chunk_decay = jnp.exp(log_gamma * RET_CHUNK)

    def step(state, xs_):
        qi, ki, vi = xs_
        attn = jnp.einsum('bhnd,bhmd->bhnm', qi, ki) * inner_decay
        o = (jnp.einsum('bhnm,bhme->bhne', attn, vi)
             + jnp.einsum('bhnd,bhde->bhne', qi, state) * q_decay[:, :, None])
        state = state * chunk_decay[:, None, None] + jnp.einsum('bhmd,bhme->bhde', ki * k_decay[:, :, None], vi)
        return state, o

    s_fin, o = lax.scan(step, s0.astype(jnp.float32), (to_chunks(q), to_chunks(k), to_chunks(v)))
    o = o.transpose(1, 0, 3, 2, 4).reshape(b, t, h, dv)
    return o.astype(q.dtype), s_fin.astype(q.dtype)


def _even_mixer(h, w_in, q_norm, w_uq, kv_norm, w_uk, w_uv, decay_f, decay_b, gn_g, w_out, rope, ctx):
    b, t, _ = h.shape
    q_lat, kv_lat, k_pe, r_q, r_k, r_v, r_g = jnp.split(h @ w_in, EVEN_SPLITS, axis=-1)
    q = (_rmsnorm(q_lat, q_norm) @ w_uq).reshape(b, t, MLA_HEADS, MLA_NOPE_DIM + MLA_ROPE_DIM)
    q_nope, q_pe = q[..., :MLA_NOPE_DIM], q[..., MLA_NOPE_DIM:]
    ckv = _rmsnorm(kv_lat, kv_norm)
    k_pe = k_pe[:, :, None, :]
    if ctx is None:
        ckv_all, kpe_all = ckv, k_pe
        s0_f = jnp.zeros((b, RET_HEADS, RET_HEAD_DIM, RET_HEAD_DIM), jnp.float32)
        s0_b = s0_f
    else:
        ctx_ckv, ctx_kpe, s0_f, s0_b = ctx
        q_pe = _rope_2d(q_pe, *rope)
        k_pe = _rope_2d(k_pe, *rope)
        ckv_all = jnp.concatenate([ckv, ctx_ckv], axis=1)
        kpe_all = jnp.concatenate([k_pe, ctx_kpe[:, :, None, :]], axis=1)
    s = ckv_all.shape[1]
    k_nope = (ckv_all @ w_uk).reshape(b, s, MLA_HEADS, MLA_NOPE_DIM)
    v = (ckv_all @ w_uv).reshape(b, s, MLA_HEADS, MLA_V_DIM)
    k = jnp.concatenate([k_nope, jnp.broadcast_to(kpe_all, (b, s, MLA_HEADS, MLA_ROPE_DIM))], axis=-1)
    qf = jnp.concatenate([q_nope, q_pe], axis=-1)[:, :, :, None, :]
    o_mla = _block_attention(qf, k, v, None).reshape(b, t, MLA_HEADS * MLA_V_DIM)
    rq = r_q.reshape(b, t, RET_HEADS, RET_HEAD_DIM)
    rk = r_k.reshape(b, t, RET_HEADS, RET_HEAD_DIM) * (RET_HEAD_DIM ** -0.5)
    rv = r_v.reshape(b, t, RET_HEADS, RET_HEAD_DIM)
    o_f, s_f = _retention_scan(rq, rk, rv, jax.nn.log_sigmoid(decay_f.astype(jnp.float32)), s0_f)
    o_b, s_b = _retention_scan(rq[:, ::-1], rk[:, ::-1], rv[:, ::-1],
                               jax.nn.log_sigmoid(decay_b.astype(jnp.float32)), s0_b)
    o_ret = (_head_norm(o_f) + _head_norm(o_b[:, ::-1])).reshape(b, t, RET_WIDTH) * gn_g * jax.nn.silu(r_g)
    out = jnp.concatenate([o_mla, o_ret], axis=-1) @ w_out
    return out, (ckv, k_pe[:, :, 0, :], s_f, s_b)


def _odd_mixer(h, w_in, sink, w_out, rope, ctx):
    b, t, _ = h.shape
    q, k, v = jnp.split(h @ w_in, (SWA_Q_DIM, SWA_Q_DIM + SWA_KV_DIM), axis=-1)
    q = q.reshape(b, t, SWA_KV_HEADS, SWA_GROUP, SWA_HEAD_DIM)
    k = k.reshape(b, t, SWA_KV_HEADS, SWA_HEAD_DIM)
    v = v.reshape(b, t, SWA_KV_HEADS, SWA_HEAD_DIM)
    if ctx is None:
        o = _block_attention(q, k, v, sink)
    else:
        ctx_k, ctx_v = ctx
        o = _window_attention(_rope_2d(q, *rope), _rope_2d(k, *rope), v, ctx_k, ctx_v, sink)
    return o.reshape(b, t, SWA_Q_DIM) @ w_out, (k, v)


def _swiglu(x, wg, wu, wd):
    return (jax.nn.silu(x @ wg) * (x @ wu)) @ wd


def _grouped_experts(xt, idx, w, wg, wu, wd):
    n, kk = idx.shape
    d = xt.shape[-1]
    n_assign = n * kk
    n_blocks = -(-n_assign // EXPERT_BLOCK) + N_EXPERTS
    flat_e = idx.reshape(-1)
    order = jnp.argsort(flat_e)
    e_sorted = flat_e[order]
    tok = order // kk
    counts = jnp.bincount(flat_e, length=N_EXPERTS)
    padded = (counts + EXPERT_BLOCK - 1) // EXPERT_BLOCK * EXPERT_BLOCK
    pad_end = jnp.cumsum(padded)
    start = jnp.cumsum(counts) - counts
    dest = (pad_end - padded)[e_sorted] + jnp.arange(n_assign) - start[e_sorted]
    buf = jnp.zeros((n_blocks * EXPERT_BLOCK, d), xt.dtype).at[dest].set(xt[tok])
    block_e = jnp.minimum(jnp.searchsorted(pad_end, jnp.arange(n_blocks) * EXPERT_BLOCK, side='right'),
                          N_EXPERTS - 1)

    def expert_block(args):
        xb, e = args
        return _swiglu(xb, wg[e], wu[e], wd[e])

    out = lax.map(expert_block, (buf.reshape(n_blocks, EXPERT_BLOCK, d), block_e)).reshape(-1, d)
    y = out[dest] * w.reshape(-1)[order][:, None].astype(xt.dtype)
    return jax.ops.segment_sum(y, tok, num_segments=n)


def _moe(h, router, router_bias, wg, wu, wd, sg, su, sd):
    b, t, d = h.shape
    xt = h.reshape(b * t, d)
    n = xt.shape[0]
    scores = jax.nn.sigmoid((xt @ router).astype(jnp.float32))
    sel = scores + router_bias.astype(jnp.float32)
    grp = lax.top_k(sel.reshape(n, N_GROUPS, N_EXPERTS // N_GROUPS), 2)[0].sum(-1)
    _, top_g = lax.top_k(grp, TOPK_GROUPS)
    gmask = jax.nn.one_hot(top_g, N_GROUPS, dtype=jnp.float32).sum(1) > 0
    emask = jnp.repeat(gmask, N_EXPERTS // N_GROUPS, axis=1)
    _, idx = lax.top_k(jnp.where(emask, sel, NEG_INF), TOP_K)
    w = jnp.take_along_axis(scores, idx, axis=1)
    w = w / w.sum(-1, keepdims=True) * ROUTED_SCALE
    routed = _grouped_experts(xt, idx, w, wg, wu, wd)
    return (routed + _swiglu(xt, sg, su, sd)).reshape(b, t, d)


def setup_inputs(seed: int = 0) -> dict:
    key = jax.random.key(seed)
    ks = iter(jax.random.split(key, 40))

    def nrm(shape, scale):
        return jax.random.normal(next(ks), shape, jnp.float32) * scale

    base = 1.0 - 2.0 ** (-5.0 - jnp.arange(RET_HEADS, dtype=jnp.float32))
    base_logit = jnp.log(base) - jnp.log1p(-base)
    d = D_MODEL
    return {
        'x_prompt': nrm((BATCH, SEQ, d), 1.0),
        'x_sample': nrm((DEC_BATCH, DEC_SEQ, d), 1.0),
        'cache_mla_ckv': nrm((DEC_BATCH, N_EVEN, PAST_LEN, MLA_KV_RANK), 1.0),
        'cache_mla_kpe': nrm((DEC_BATCH, N_EVEN, PAST_LEN, MLA_ROPE_DIM), 1.0),
        'state_ret_fwd': nrm((DEC_BATCH, N_EVEN, RET_HEADS, RET_HEAD_DIM, RET_HEAD_DIM), 1.0),
        'state_ret_bwd': nrm((DEC_BATCH, N_EVEN, RET_HEADS, RET_HEAD_DIM, RET_HEAD_DIM), 1.0),
        'cache_swa_k': nrm((DEC_BATCH, N_ODD, PAST_LEN, SWA_KV_HEADS, SWA_HEAD_DIM), 1.0),
        'cache_swa_v': nrm((DEC_BATCH, N_ODD, PAST_LEN, SWA_KV_HEADS, SWA_HEAD_DIM), 1.0),
        'c': nrm((DEC_BATCH, d), 1.0),
        'c_ctx': nrm((d,), 1.0),
        'w_mod': nrm((DEPTH, d, 6 * d), 0.5 * d ** -0.5),
        'b_mod': nrm((DEPTH, 6 * d), 0.02),
        'ln1_g': 1.0 + nrm((DEPTH, d), 0.02),
        'ln1_b': nrm((DEPTH, d), 0.02),
        'ln2_g': 1.0 + nrm((DEPTH, d), 0.02),
        'ln2_b': nrm((DEPTH, d), 0.02),
        'mla_ret_w_in': nrm((N_EVEN, d, EVEN_IN_DIM), d ** -0.5),
        'mla_q_norm': 1.0 + nrm((N_EVEN, MLA_Q_RANK), 0.02),
        'mla_w_uq': nrm((N_EVEN, MLA_Q_RANK, MLA_HEADS * (MLA_NOPE_DIM + MLA_ROPE_DIM)), MLA_Q_RANK ** -0.5),
        'mla_kv_norm': 1.0 + nrm((N_EVEN, MLA_KV_RANK), 0.02),
        'mla_w_uk': nrm((N_EVEN, MLA_KV_RANK, MLA_HEADS * MLA_NOPE_DIM), MLA_KV_RANK ** -0.5),
        'mla_w_uv': nrm((N_EVEN, MLA_KV_RANK, MLA_HEADS * MLA_V_DIM), MLA_KV_RANK ** -0.5),
        'ret_decay_fwd': base_logit + nrm((N_EVEN, RET_HEADS), 0.1),
        'ret_decay_bwd': base_logit + nrm((N_EVEN, RET_HEADS), 0.1),
        'ret_gn_g': 1.0 + nrm((N_EVEN, RET_WIDTH), 0.02),
        'even_w_out': nrm((N_EVEN, EVEN_MIX_DIM, d), EVEN_MIX_DIM ** -0.5 * BETA),
        'swa_w_in': nrm((N_ODD, d, SWA_IN_DIM), d ** -0.5),
        'swa_sink': nrm((N_ODD, SWA_HEADS), 0.5),
        'swa_w_out': nrm((N_ODD, SWA_Q_DIM, d), SWA_Q_DIM ** -0.5 * BETA),
        'moe_router': nrm((DEPTH, d, N_EXPERTS), d ** -0.5),
        'moe_router_bias': nrm((DEPTH, N_EXPERTS), 0.01),
        'moe_w_gate': nrm((DEPTH, N_EXPERTS, d, EXPERT_DIM), d ** -0.5),
        'moe_w_up': nrm((DEPTH, N_EXPERTS, d, EXPERT_DIM), d ** -0.5),
        'moe_w_down': nrm((DEPTH, N_EXPERTS, EXPERT_DIM, d), EXPERT_DIM ** -0.5 * BETA),
        'shared_w_gate': nrm((DEPTH, d, SHARED_DIM), d ** -0.5),
        'shared_w_up': nrm((DEPTH, d, SHARED_DIM), d ** -0.5),
        'shared_w_down': nrm((DEPTH, SHARED_DIM, d), SHARED_DIM ** -0.5 * BETA),
    }


def reference(x_prompt, x_sample, cache_mla_ckv, cache_mla_kpe, state_ret_fwd, state_ret_bwd,
              cache_swa_k, cache_swa_v, c, c_ctx, w_mod, b_mod, ln1_g, ln1_b, ln2_g, ln2_b,
              mla_ret_w_in, mla_q_norm, mla_w_uq, mla_kv_norm, mla_w_uk, mla_w_uv,
              ret_decay_fwd, ret_decay_bwd, ret_gn_g, even_w_out, swa_w_in, swa_sink, swa_w_out,
              moe_router, moe_router_bias, moe_w_gate, moe_w_up, moe_w_down,
              shared_w_gate, shared_w_up, shared_w_down):
    rows = x_sample.shape[1] // GRID_W
    rope_mla = _axial_rope(rows, MLA_ROPE_DIM)
    rope_swa = _axial_rope(rows, SWA_HEAD_DIM)
    xp, xs = x_prompt, x_sample
    ckv_l, kpe_l, rf_l, rb_l, sk_l, sv_l = [], [], [], [], [], []
    for l in range(DEPTH):
        mp = _modulation(c_ctx[None, :], w_mod[l], b_mod[l])
        ms = _modulation(c, w_mod[l], b_mod[l])
        hp = _modulate(xp, mp[0], mp[1])
        hs = _modulate(xs, ms[0], ms[1])
        if l % 2 == 0:
            e = l // 2
            ew = (mla_ret_w_in[e], mla_q_norm[e], mla_w_uq[e], mla_kv_norm[e], mla_w_uk[e], mla_w_uv[e],
                  ret_decay_fwd[e], ret_decay_bwd[e], ret_gn_g[e], even_w_out[e])
            out_p, (ckv, kpe, s_f, s_b) = _even_mixer(hp, *ew, rope=None, ctx=None)
            ckv_l.append(ckv)
            kpe_l.append(kpe)
            rf_l.append(s_f)
            rb_l.append(s_b)
            out_s, _ = _even_mixer(hs, *ew, rope=rope_mla,
                                   ctx=(cache_mla_ckv[:, e], cache_mla_kpe[:, e],
                                        state_ret_fwd[:, e], state_ret_bwd[:, e]))
        else:
            o = l // 2
            ow = (swa_w_in[o], swa_sink[o], swa_w_out[o])
            out_p, (k_c, v_c) = _odd_mixer(hp, *ow, rope=None, ctx=None)
            sk_l.append(k_c)
            sv_l.append(v_c)
            out_s, _ = _odd_mixer(hs, *ow, rope=rope_swa, ctx=(cache_swa_k[:, o], cache_swa_v[:, o]))
        xp = _layernorm(ALPHA * xp + mp[2] * out_p, ln1_g[l], ln1_b[l])
        xs = _layernorm(ALPHA * xs + ms[2] * out_s, ln1_g[l], ln1_b[l])
        mw = (moe_router[l], moe_router_bias[l], moe_w_gate[l], moe_w_up[l], moe_w_down[l],
              shared_w_gate[l], shared_w_up[l], shared_w_down[l])
        xp = _layernorm(ALPHA * xp + mp[5] * _moe(_modulate(xp, mp[3], mp[4]), *mw), ln2_g[l], ln2_b[l])
        xs = _layernorm(ALPHA * xs + ms[5] * _moe(_modulate(xs, ms[3], ms[4]), *mw), ln2_g[l], ln2_b[l])
    new_mla_ckv = jnp.stack(ckv_l, axis=1)
    new_mla_kpe = jnp.stack(kpe_l, axis=1)
    new_ret_fwd = jnp.stack(rf_l, axis=1)
    new_ret_bwd = jnp.stack(rb_l, axis=1)
    new_swa_k = jnp.stack(sk_l, axis=1)
    new_swa_v = jnp.stack(sv_l, axis=1)
    return (xp, xs, new_mla_ckv, new_mla_kpe, new_ret_fwd, new_ret_bwd, new_swa_k, new_swa_v)
```

```python
import functools

import jax
import jax.numpy as jnp
from jax import lax
from jax.experimental import pallas as pl
from jax.experimental.pallas import tpu as pltpu

F32 = jnp.float32
BF16 = jnp.bfloat16

D = 1024
DEPTH = 4
GRID_W = 64
ALPHA = (2.0 * DEPTH) ** 0.25
LN_EPS = 1e-5
RMS_EPS = 1e-6
ROPE_BASE = 10000.0
NEG_INF = -1e30
MLA_HEADS = 8
MLA_NOPE = 64
MLA_ROPE = 32
MLA_V = 64
MLA_QR = 256
MLA_KVR = 128
MLA_HP = 128
RET_HEADS = 8
RET_HD = 64
RET_CHUNK = 128
RET_W = RET_HEADS * RET_HD
EVEN_P = 512 + 4 * RET_W
SWA_HEADS = 16
SWA_KVH = 4
SWA_HD = 64
WINDOW = 128
SWA_Q = SWA_HEADS * SWA_HD
SWA_KV = SWA_KVH * SWA_HD
N_EXP = 64
TOP_K = 8
N_GROUPS = 8
TOPK_GROUPS = 4
EXP_D = 256
ROUTED_SCALE = 2.5

TM = 256
SUB = 8
LANE = 128
EXP_TM = 256
VMEM_BIG = 56 * 1024 * 1024


def _cparams(n_axes, vmem=None):
    return pltpu.CompilerParams(dimension_semantics=("arbitrary",) * n_axes, vmem_limit_bytes=vmem)


def _dot(a, b):
    return jnp.dot(a, b, preferred_element_type=F32)


def _dot_nt(a, b):
    return lax.dot_general(a, b, (((1,), (1,)), ((), ())), preferred_element_type=F32)


def _dot_tn(a, b):
    return lax.dot_general(a, b, (((0,), (0,)), ((), ())), preferred_element_type=F32)


def _layernorm(z, g, b):
    mu = jnp.mean(z, axis=-1, keepdims=True)
    zc = z - mu
    var = jnp.mean(zc * zc, axis=-1, keepdims=True)
    return zc * lax.rsqrt(var + LN_EPS) * g + b


def _rmsnorm(x, g):
    return x * lax.rsqrt(jnp.mean(x * x, axis=-1, keepdims=True) + RMS_EPS) * g


def _rope(x, c, sp, sm, shift):
    w = x.shape[-1]
    return x * c + pltpu.roll(x, shift, 1) * sp + pltpu.roll(x, w - shift, 1) * sm


def _mod_kernel(c_ref, w_ref, b_ref, o_ref):
    s = jax.nn.silu(c_ref[...]).astype(BF16)
    o_ref[0] = _dot(s, w_ref[0].astype(BF16)) + b_ref[0]


def _modulation(cond8, w_mod, b_mod):
    nt = 4
    tn = 6 * D // nt
    return pl.pallas_call(
        _mod_kernel,
        grid=(DEPTH, nt),
        in_specs=[pl.BlockSpec((8, D), lambda l, j: (0, 0)),
                  pl.BlockSpec((1, D, tn), lambda l, j: (l, 0, j)),
                  pl.BlockSpec((1, 1, tn), lambda l, j: (l, 0, j))],
        out_specs=pl.BlockSpec((1, 8, tn), lambda l, j: (l, 0, j)),
        out_shape=jax.ShapeDtypeStruct((DEPTH, 8, 6 * D), F32),
        compiler_params=_cparams(2, 40 * 1024 * 1024),
        name="modulation",
    )(cond8, w_mod, b_mod.reshape(DEPTH, 1, 6 * D))


def _mod_row(i, per_batch):
    return 0 if per_batch is None else 1 + i // per_batch


def _modmm_kernel(x_ref, m_ref, w_ref, o_ref, *, shift_i, scale_i):
    h = x_ref[...] * (1.0 + m_ref[0, scale_i:scale_i + 1, :]) + m_ref[0, shift_i:shift_i + 1, :]
    o_ref[...] = _dot(h.astype(BF16), w_ref[...])


def _modmm(x, mod, w, per_batch):
    rows = x.shape[0]
    n = w.shape[1]
    return pl.pallas_call(
        functools.partial(_modmm_kernel, shift_i=0, scale_i=1),
        grid=(rows // TM,),
        in_specs=[pl.BlockSpec((TM, D), lambda i: (i, 0)),
                  pl.BlockSpec((1, 6, D), lambda i: (_mod_row(i, per_batch), 0, 0)),
                  pl.BlockSpec((D, n), lambda i: (0, 0))],
        out_specs=pl.BlockSpec((TM, n), lambda i: (i, 0)),
        out_shape=jax.ShapeDtypeStruct((rows, n), F32),
        compiler_params=_cparams(1, 40 * 1024 * 1024),
        name="modmm",
    )(x, mod, w)


def _mla_prep_kernel(p_ref, c_ref, sp_ref, sm_ref, qn_ref, kvn_ref, wuq_ref, wuk_ref, wuv_ref,
                     q_ref, k_ref, v_ref, ckv_ref, kpe_ref):
    p = p_ref[...]
    c, sp, sm = c_ref[...], sp_ref[...], sm_ref[...]
    qn = _rmsnorm(p[:, 0:MLA_QR], qn_ref[...])
    q = _dot(qn.astype(BF16), wuq_ref[...])
    ckv = _rmsnorm(p[:, MLA_QR:MLA_QR + MLA_KVR], kvn_ref[...])
    ckv_ref[...] = ckv
    kpe = _rope(p[:, MLA_QR + MLA_KVR:512], c, sp, sm, MLA_ROPE // 4)
    kpe_ref[...] = kpe
    ckv_b = ckv.astype(BF16)
    kn = _dot(ckv_b, wuk_ref[...])
    for h in range(MLA_HEADS):
        sl = slice(MLA_HP * h, MLA_HP * (h + 1))
        q_ref[:, sl] = _rope(q[:, sl], c, sp, sm, MLA_ROPE // 4).astype(BF16)
        k_ref[:, sl] = (kn[:, sl] + kpe).astype(BF16)
    v_ref[...] = _dot(ckv_b, wuv_ref[...]).astype(BF16)


def _mla_prep(proj, tabs, ew, per_batch):
    rows = proj.shape[0]
    c, sp, sm = tabs

    def tab_idx(i):
        return (0 if per_batch is None else 1 + i % per_batch, 0)

    tab_spec = pl.BlockSpec((TM, LANE), tab_idx)
    full = lambda a: pl.BlockSpec(a.shape, lambda i: (0,) * a.ndim)
    return pl.pallas_call(
        _mla_prep_kernel,
        grid=(rows // TM,),
        in_specs=[pl.BlockSpec((TM, 512), lambda i: (i, 0)), tab_spec, tab_spec, tab_spec,
                  full(ew["q_norm"]), full(ew["kv_norm"]), full(ew["w_uq"]), full(ew["w_uk"]), full(ew["w_uv"])],
        out_specs=[pl.BlockSpec((TM, MLA_HEADS * MLA_HP), lambda i: (i, 0)),
                   pl.BlockSpec((TM, MLA_HEADS * MLA_HP), lambda i: (i, 0)),
                   pl.BlockSpec((TM, MLA_HEADS * MLA_V), lambda i: (i, 0)),
                   pl.BlockSpec((TM, MLA_KVR), lambda i: (i, 0)),
                   pl.BlockSpec((TM, LANE), lambda i: (i, 0))],
        out_shape=[jax.ShapeDtypeStruct((rows, MLA_HEADS * MLA_HP), BF16),
                   jax.ShapeDtypeStruct((rows, MLA_HEADS * MLA_HP), BF16),
                   jax.ShapeDtypeStruct((rows, MLA_HEADS * MLA_V), BF16),
                   jax.ShapeDtypeStruct((rows, MLA_KVR), F32),
                   jax.ShapeDtypeStruct((rows, LANE), F32)],
        compiler_params=_cparams(1),
        name="mla_prep",
    )(proj, c, sp, sm, ew["q_norm"], ew["kv_norm"], ew["w_uq"], ew["w_uk"], ew["w_uv"])


def _mla_ctx_kernel(ckv_ref, kpe_ref, wuk_ref, wuv_ref, k_ref, v_ref):
    ckv_b = ckv_ref[...].astype(BF16)
    kn = _dot(ckv_b, wuk_ref[...])
    kpe = kpe_ref[...]
    for h in range(MLA_HEADS):
        sl = slice(MLA_HP * h, MLA_HP * (h + 1))
        k_ref[:, sl] = (kn[:, sl] + kpe).astype(BF16)
    v_ref[...] = _dot(ckv_b, wuv_ref[...]).astype(BF16)


def _mla_ctx(ckv, kpe_pad, ew):
    rows = ckv.shape[0]
    full = lambda a: pl.BlockSpec(a.shape, lambda i: (0,) * a.ndim)
    return pl.pallas_call(
        _mla_ctx_kernel,
        grid=(rows // TM,),
        in_specs=[pl.BlockSpec((TM, MLA_KVR), lambda i: (i, 0)), pl.BlockSpec((TM, LANE), lambda i: (i, 0)),
                  full(ew["w_uk"]), full(ew["w_uv"])],
        out_specs=[pl.BlockSpec((TM, MLA_HEADS * MLA_HP), lambda i: (i, 0)),
                   pl.BlockSpec((TM, MLA_HEADS * MLA_V), lambda i: (i, 0))],
        out_shape=[jax.ShapeDtypeStruct((rows, MLA_HEADS * MLA_HP), BF16),
                   jax.ShapeDtypeStruct((rows, MLA_HEADS * MLA_V), BF16)],
        compiler_params=_cparams(1),
        name="mla_ctx",
    )(ckv, kpe_pad, ew["w_uk"], ew["w_uv"])


def _attn_kernel(*refs, n_heads, group, dqk, dv, scale, has_ctx, has_sink, window, tq, seq):
    refs = list(refs)
    sink_ref = refs.pop(0) if has_sink else None
    q_ref, k_ref, v_ref = refs[:3]
    kc_ref, vc_ref = (refs[3], refs[4]) if has_ctx else (None, None)
    o_ref = refs[-1]
    i = pl.program_id(1)
    if window:
        kw = tq + 2 * window
        start = pl.multiple_of(jnp.clip(i * tq - window, 0, seq - kw), LANE)
        qpos = i * tq + lax.broadcasted_iota(jnp.int32, (tq, kw), 0)
        kpos = start + lax.broadcasted_iota(jnp.int32, (tq, kw), 1)
        valid = jnp.abs(qpos - kpos) <= window
    for h in range(n_heads):
        hk = h // group
        q = q_ref[:, h * dqk:(h + 1) * dqk]
        if window:
            k = k_ref[pl.ds(start, kw), hk * dqk:(hk + 1) * dqk]
            v = v_ref[pl.ds(start, kw), hk * dv:(hk + 1) * dv]
            s = jnp.where(valid, _dot_nt(q, k) * scale, NEG_INF)
        else:
            k = k_ref[:, hk * dqk:(hk + 1) * dqk]
            v = v_ref[:, hk * dv:(hk + 1) * dv]
            s = _dot_nt(q, k) * scale
        m = jnp.max(s, axis=-1, keepdims=True)
        if has_ctx:
            sc = _dot_nt(q, kc_ref[:, hk * dqk:(hk + 1) * dqk]) * scale
            m = jnp.maximum(m, jnp.max(sc, axis=-1, keepdims=True))
        if has_sink:
            sk = sink_ref[h]
            m = jnp.maximum(m, sk)
        p = jnp.exp(s - m)
        l = jnp.sum(p, axis=-1, keepdims=True)
        o = _dot(p.astype(BF16), v)
        if has_ctx:
            pc = jnp.exp(sc - m)
            l = l + jnp.sum(pc, axis=-1, keepdims=True)
            o = o + _dot(pc.astype(BF16), vc_ref[:, hk * dv:(hk + 1) * dv])
        if has_sink:
            l = l + jnp.exp(sk - m)
        o_ref[:, h * dv:(h + 1) * dv] = (o / l).astype(o_ref.dtype)


def _attention(q, k, v, ctx, sink, *, n_batch, seq, n_heads, group, dqk, dv, scale, window):
    tq = TM
    nq = seq // tq
    n_kv = n_heads // group
    in_specs = []
    args = []
    if sink is not None:
        in_specs.append(pl.BlockSpec(memory_space=pltpu.SMEM))
        args.append(sink)
    in_specs += [pl.BlockSpec((tq, n_heads * dqk), lambda b, i: (b * nq + i, 0)),
                 pl.BlockSpec((seq, n_kv * dqk), lambda b, i: (b, 0)),
                 pl.BlockSpec((seq, n_kv * dv), lambda b, i: (b, 0))]
    args += [q, k, v]
    if ctx is not None:
        kc, vc = ctx
        sc = kc.shape[0] // n_batch
        in_specs += [pl.BlockSpec((sc, n_kv * dqk), lambda b, i: (b, 0)),
                     pl.BlockSpec((sc, n_kv * dv), lambda b, i: (b, 0))]
        args += [kc, vc]
    kern = functools.partial(_attn_kernel, n_heads=n_heads, group=group, dqk=dqk, dv=dv, scale=scale,
                             has_ctx=ctx is not None, has_sink=sink is not None, window=window, tq=tq, seq=seq)
    return pl.pallas_call(
        kern,
        grid=(n_batch, nq),
        in_specs=in_specs,
        out_specs=pl.BlockSpec((tq, n_heads * dv), lambda b, i: (b * nq + i, 0)),
        out_shape=jax.ShapeDtypeStruct((n_batch * seq, n_heads * dv), BF16),
        compiler_params=_cparams(2, 40 * 1024 * 1024),
        name="attention",
    )(*args)


def _ret_kernel(*refs, seq, has_init, out_state):
    refs = list(refs)
    df_ref, db_ref, rq_ref, rk_ref, rv_ref, rg_ref, gn_ref = refs[:7]
    pos = 7
    if has_init:
        s0f_ref, s0b_ref = refs[pos], refs[pos + 1]
        pos += 2
    o_ref = refs[pos]
    pos += 1
    if out_state:
        sf_ref, sb_ref = refs[pos], refs[pos + 1]
        pos += 2
    of_scr, ob_scr = refs[pos], refs[pos + 1]

    pair = pl.program_id(1)
    n_chunks = seq // RET_CHUNK
    idx_c = lax.broadcasted_iota(jnp.int32, (RET_CHUNK, 1), 0).astype(F32)
    diff = (lax.broadcasted_iota(jnp.int32, (RET_CHUNK, RET_CHUNK), 0)
            - lax.broadcasted_iota(jnp.int32, (RET_CHUNK, RET_CHUNK), 1)).astype(F32)
    for hh in range(2):
        h = 2 * pair + hh
        hs = slice(RET_HD * hh, RET_HD * (hh + 1))
        for fwd in (True, False):
            d = jnp.full((1, 1), (df_ref if fwd else db_ref)[h], F32)
            lg = jnp.minimum(d, 0.0) - jnp.log1p(jnp.exp(-jnp.abs(d)))
            dd = diff if fwd else -diff
            mask = jnp.where(dd >= 0, jnp.exp(lg * jnp.maximum(dd, 0.0)), 0.0)
            if fwd:
                q_dec = jnp.exp(lg * (idx_c + 1.0))
                k_dec = jnp.exp(lg * (RET_CHUNK - 1.0 - idx_c))
            else:
                q_dec = jnp.exp(lg * (RET_CHUNK - idx_c))
                k_dec = jnp.exp(lg * idx_c)
            c_dec = jnp.exp(lg * RET_CHUNK)
            scr = of_scr if fwd else ob_scr
            if has_init:
                state0 = (s0f_ref if fwd else s0b_ref)[0, hh]
            else:
                state0 = jnp.zeros((RET_HD, RET_HD), F32)

            def chunk(ci, state, fwd=fwd, mask=mask, q_dec=q_dec, k_dec=k_dec, c_dec=c_dec, scr=scr, hs=hs):
                cidx = ci if fwd else n_chunks - 1 - ci
                r0 = pl.multiple_of(cidx * RET_CHUNK, RET_CHUNK)
                q = rq_ref[pl.ds(r0, RET_CHUNK), hs].astype(BF16)
                k = rk_ref[pl.ds(r0, RET_CHUNK), hs] * (RET_HD ** -0.5)
                v = rv_ref[pl.ds(r0, RET_CHUNK), hs].astype(BF16)
                attn = _dot_nt(q, k.astype(BF16)) * mask
                o = _dot(attn.astype(BF16), v) + _dot(q, state.astype(BF16)) * q_dec
                scr[pl.ds(r0, RET_CHUNK), hs] = o
                return state * c_dec + _dot_tn((k * k_dec).astype(BF16), v)

            state = lax.fori_loop(0, n_chunks, chunk, state0)
            if out_state:
                (sf_ref if fwd else sb_ref)[0, hh] = state

    def head_norm(x):
        mu = jnp.mean(x, axis=-1, keepdims=True)
        xc = x - mu
        return xc * lax.rsqrt(jnp.mean(xc * xc, axis=-1, keepdims=True) + LN_EPS)

    for hh in range(2):
        hs = slice(RET_HD * hh, RET_HD * (hh + 1))
        o = head_norm(of_scr[:, hs]) + head_norm(ob_scr[:, hs])
        o_ref[:, hs] = (o * gn_ref[:, hs] * jax.nn.silu(rg_ref[:, hs])).astype(o_ref.dtype)


def _retention(proj, dec_f, dec_b, gn, init, *, n_batch, seq, out_state):
    pairs = RET_HEADS // 2
    col0 = 512 // LANE

    def col_spec(k):
        return pl.BlockSpec((seq, LANE), lambda b, p: (b, col0 + k * pairs + p))

    smem = pl.BlockSpec(memory_space=pltpu.SMEM)
    st_spec = pl.BlockSpec((1, 2, RET_HD, RET_HD), lambda b, p: (b, p, 0, 0))
    in_specs = [smem, smem, col_spec(0), col_spec(1), col_spec(2), col_spec(3),
                pl.BlockSpec((1, LANE), lambda b, p: (0, p))]
    args = [dec_f, dec_b, proj, proj, proj, proj, gn]
    if init is not None:
        in_specs += [st_spec, st_spec]
        args += list(init)
    out_specs = [pl.BlockSpec((seq, LANE), lambda b, p: (b, p))]
    out_shape = [jax.ShapeDtypeStruct((n_batch * seq, RET_W), BF16)]
    if out_state:
        out_specs += [st_spec, st_spec]
        out_shape += [jax.ShapeDtypeStruct((n_batch, RET_HEADS, RET_HD, RET_HD), F32)] * 2
    kern = functools.partial(_ret_kernel, seq=seq, has_init=init is not None, out_state=out_state)
    return pl.pallas_call(
        kern,
        grid=(n_batch, pairs),
        in_specs=in_specs,
        out_specs=out_specs,
        out_shape=out_shape,
        scratch_shapes=[pltpu.VMEM((seq, LANE), F32), pltpu.VMEM((seq, LANE), F32)],
        compiler_params=_cparams(2),
        name="retention",
    )(*args)


def _swa_prep_kernel(p_ref, c_ref, sp_ref, sm_ref, q_ref, k_ref, v_ref):
    c, sp, sm = c_ref[...], sp_ref[...], sm_ref[...]
    for j in range(SWA_Q // LANE):
        sl = slice(LANE * j, LANE * (j + 1))
        q_ref[:, sl] = _rope(p_ref[:, sl], c, sp, sm, SWA_HD // 4).astype(BF16)
    for j in range(SWA_KV // LANE):
        sl = slice(LANE * j, LANE * (j + 1))
        k_ref[:, sl] = _rope(p_ref[:, SWA_Q + LANE * j:SWA_Q + LANE * (j + 1)], c, sp, sm, SWA_HD // 4).astype(BF16)
    v_ref[...] = p_ref[:, SWA_Q + SWA_KV:].astype(BF16)


def _swa_prep(proj, tabs, per_batch):
    rows = proj.shape[0]
    c, sp, sm = tabs

    def tab_idx(i):
        return (0 if per_batch is None else 1 + i % per_batch, 0)

    tab_spec = pl.BlockSpec((TM, LANE), tab_idx)
    return pl.pallas_call(
        _swa_prep_kernel,
        grid=(rows // TM,),
        in_specs=[pl.BlockSpec((TM, SWA_Q + 2 * SWA_KV), lambda i: (i, 0)), tab_spec, tab_spec, tab_spec],
        out_specs=[pl.BlockSpec((TM, SWA_Q), lambda i: (i, 0)),
                   pl.BlockSpec((TM, SWA_KV), lambda i: (i, 0)),
                   pl.BlockSpec((TM, SWA_KV), lambda i: (i, 0))],
        out_shape=[jax.ShapeDtypeStruct((rows, SWA_Q), BF16),
                   jax.ShapeDtypeStruct((rows, SWA_KV), BF16),
                   jax.ShapeDtypeStruct((rows, SWA_KV), BF16)],
        compiler_params=_cparams(1),
        name="swa_prep",
    )(proj, c, sp, sm)


def _to_row_tiles(ref, x):
    rows = x.shape[0]
    for s in range(D // LANE):
        ref[pl.ds(s, rows, stride=SUB), :] = x[:, LANE * s:LANE * (s + 1)]


def _from_row_tiles(ref, rows):
    return jnp.concatenate([ref[pl.ds(s, rows, stride=SUB), :] for s in range(D // LANE)], axis=1)


def _post_kernel(*refs, n_parts):
    a_refs = refs[:n_parts]
    w_ref, x_ref, m_ref, g_ref, b_ref, rh_ref, rl_ref, x1_ref, h2_ref, lg_ref = refs[n_parts:]
    out = None
    off = 0
    for a_ref in a_refs:
        kk = a_ref.shape[1]
        part = _dot(a_ref[...], w_ref[off:off + kk, :])
        out = part if out is None else out + part
        off += kk
    z = ALPHA * x_ref[...] + m_ref[0, 2:3, :] * out
    x1 = _layernorm(z, g_ref[...], b_ref[...])
    x1_ref[...] = x1
    h2 = x1 * (1.0 + m_ref[0, 4:5, :]) + m_ref[0, 3:4, :]
    _to_row_tiles(h2_ref, h2)
    hi = h2.astype(BF16)
    lo = (h2 - hi.astype(F32)).astype(BF16)
    rh = rh_ref[...]
    lg_ref[...] = _dot(hi, rh) + _dot(lo, rh) + _dot(hi, rl_ref[...])


def _post_mixer(parts, w_out, x, mod, ln_g, ln_b, r_hi, r_lo, per_batch):
    rows = x.shape[0]
    full = lambda a: pl.BlockSpec(a.shape, lambda i: (0,) * a.ndim)
    in_specs = [pl.BlockSpec((TM, a.shape[1]), lambda i: (i, 0)) for a in parts]
    in_specs += [full(w_out), pl.BlockSpec((TM, D), lambda i: (i, 0)),
                 pl.BlockSpec((1, 6, D), lambda i: (_mod_row(i, per_batch), 0, 0)),
                 full(ln_g), full(ln_b), full(r_hi), full(r_lo)]
    return pl.pallas_call(
        functools.partial(_post_kernel, n_parts=len(parts)),
        grid=(rows // TM,),
        in_specs=in_specs,
        out_specs=[pl.BlockSpec((TM, D), lambda i: (i, 0)),
                   pl.BlockSpec((TM * SUB, LANE), lambda i: (i, 0)),
                   pl.BlockSpec((TM, LANE), lambda i: (i, 0))],
        out_shape=[jax.ShapeDtypeStruct((rows, D), F32),
                   jax.ShapeDtypeStruct((rows * SUB, LANE), F32),
                   jax.ShapeDtypeStruct((rows, LANE), F32)],
        compiler_params=_cparams(1),
        name="post_mixer",
    )(*parts, w_out, x, mod, ln_g, ln_b, r_hi, r_lo)


def _route_kernel(lg_ref, bias_ref, idx_ref, rank_ref, w_ref, cnt_ref, carry_ref):
    i = pl.program_id(0)

    @pl.when(i == 0)
    def _():
        carry_ref[...] = jnp.zeros_like(carry_ref)

    t = lg_ref.shape[0]
    gsz = N_EXP // N_GROUPS
    scores = jax.nn.sigmoid(lg_ref[...].T[:N_EXP])
    sel = scores + bias_ref[...]
    g3 = sel.reshape(N_GROUPS, gsz, t)
    sub_iota = lax.broadcasted_iota(jnp.int32, g3.shape, 1)
    m1 = jnp.max(g3, axis=1)
    first = jnp.min(jnp.where(g3 == m1[:, None, :], sub_iota, gsz), axis=1)
    m2 = jnp.max(jnp.where(sub_iota == first[:, None, :], -jnp.inf, g3), axis=1)
    grp = m1 + m2
    g_iota = lax.broadcasted_iota(jnp.int32, grp.shape, 0)
    gmask = jnp.zeros(grp.shape, jnp.bool_)
    for _ in range(TOPK_GROUPS):
        gm = jnp.max(grp, axis=0, keepdims=True)
        gi = jnp.min(jnp.where(grp == gm, g_iota, N_GROUPS), axis=0, keepdims=True)
        hit = g_iota == gi
        gmask = jnp.logical_or(gmask, hit)
        grp = jnp.where(hit, -jnp.inf, grp)
    emask = jnp.broadcast_to(gmask[:, None, :], g3.shape).reshape(N_EXP, t)
    cur = jnp.where(emask, sel, NEG_INF)
    e_iota = lax.broadcasted_iota(jnp.int32, cur.shape, 0)
    hits = []
    member = jnp.zeros(cur.shape, F32)
    for _ in range(TOP_K):
        cm = jnp.max(cur, axis=0, keepdims=True)
        ci = jnp.min(jnp.where(cur == cm, e_iota, N_EXP), axis=0, keepdims=True)
        hit = e_iota == ci
        hits.append((hit, ci))
        member = member + hit.astype(F32)
        cur = jnp.where(hit, -jnp.inf, cur)
    tri = (lax.broadcasted_iota(jnp.int32, (t, t), 0) < lax.broadcasted_iota(jnp.int32, (t, t), 1)).astype(BF16)
    before = _dot(member.astype(BF16), tri) + carry_ref[:, 0:1]
    ws = [jnp.sum(jnp.where(hit, scores, 0.0), axis=0, keepdims=True) for hit, _ in hits]
    wsum = ws[0]
    for w in ws[1:]:
        wsum = wsum + w
    for k, (hit, ci) in enumerate(hits):
        idx_ref[k:k + 1, :] = ci
        rank_ref[k:k + 1, :] = jnp.sum(jnp.where(hit, before, 0.0), axis=0, keepdims=True).astype(jnp.int32)
        w_ref[k:k + 1, :] = ws[k] / wsum * ROUTED_SCALE
    total = carry_ref[...] + jnp.sum(member, axis=1, keepdims=True)
    carry_ref[...] = total
    cnt_ref[...] = total.astype(jnp.int32)


def _route(logits, bias_col):
    rows = logits.shape[0]
    row_spec = pl.BlockSpec((TOP_K, TM), lambda i: (0, i))
    return pl.pallas_call(
        _route_kernel,
        grid=(rows // TM,),
        in_specs=[pl.BlockSpec((TM, LANE), lambda i: (i, 0)), pl.BlockSpec((N_EXP, 1), lambda i: (0, 0))],
        out_specs=[row_spec, row_spec, row_spec, pl.BlockSpec((N_EXP, LANE), lambda i: (0, 0))],
        out_shape=[jax.ShapeDtypeStruct((TOP_K, rows), jnp.int32),
                   jax.ShapeDtypeStruct((TOP_K, rows), jnp.int32),
                   jax.ShapeDtypeStruct((TOP_K, rows), F32),
                   jax.ShapeDtypeStruct((N_EXP, LANE), jnp.int32)],
        scratch_shapes=[pltpu.VMEM((N_EXP, LANE), F32)],
        compiler_params=_cparams(1),
        name="route",
    )(logits, bias_col)


def _experts_kernel(tile_ref, exp_ref, lo_ref, hi_ref, first_ref, tok_ref,
                    x_ref, wcol_ref, wg_ref, wu_ref, wd_ref, y_ref, xbuf, obuf, *, trash):
    i = pl.program_id(0)
    lo = lo_ref[i]
    hi = hi_ref[i]
    base = tile_ref[i] * EXP_TM

    @pl.when(i == 0)
    def _():
        y_ref[...] = jnp.zeros_like(y_ref)

    @pl.when(hi > lo)
    def _():
        @pl.when(first_ref[i] == 1)
        def _():
            def gather(j, c):
                for u in range(SUB):
                    r = j * SUB + u
                    xbuf[pl.ds(pl.multiple_of(r * SUB, SUB), SUB), :] = x_ref[tok_ref[base + r]]
                return c
            lax.fori_loop(0, EXP_TM // SUB, gather, 0)

        x = _from_row_tiles(xbuf, EXP_TM).astype(BF16)
        g = _dot(x, wg_ref[...].astype(BF16))
        u = _dot(x, wu_ref[...].astype(BF16))
        h = (jax.nn.silu(g) * u).astype(BF16)
        y = _dot(h, wd_ref[...].astype(BF16))
        rows = lax.broadcasted_iota(jnp.int32, (EXP_TM, 1), 0)
        y = jnp.where(jnp.logical_and(rows >= lo, rows < hi), y * wcol_ref[...], 0.0)
        _to_row_tiles(obuf, y)

        def scatter(j, c):
            vals = []
            for u in range(SUB):
                r = j * SUB + u
                t = jnp.where(jnp.logical_and(r >= lo, r < hi), tok_ref[base + r], trash)
                vals.append((t, y_ref[t] + obuf[pl.ds(pl.multiple_of(r * SUB, SUB), SUB), :]))
            for t, v in vals:
                y_ref[t] = v
            return c
        lax.fori_loop(lo // SUB, (hi + SUB - 1) // SUB, scatter, 0)


def _experts(items, tok_sorted, h2_tiles, w_col, wg, wu, wd, layer):
    n_tok = h2_tiles.shape[0]
    n_items = items[0].shape[0]
    wmap = lambda i, tile, exp, lo, hi, first, tok: (layer, exp[i], 0, 0)
    gs = pltpu.PrefetchScalarGridSpec(
        num_scalar_prefetch=6,
        grid=(n_items,),
        in_specs=[pl.BlockSpec(memory_space=pltpu.VMEM),
                  pl.BlockSpec((EXP_TM, 1), lambda i, tile, *_: (tile[i], 0)),
                  pl.BlockSpec((None, None, D, EXP_D), wmap),
                  pl.BlockSpec((None, None, D, EXP_D), wmap),
                  pl.BlockSpec((None, None, EXP_D, D), wmap)],
        out_specs=pl.BlockSpec(memory_space=pltpu.VMEM),
        scratch_shapes=[pltpu.VMEM((EXP_TM * SUB, LANE), F32), pltpu.VMEM((EXP_TM * SUB, LANE), F32)],
    )
    return pl.pallas_call(
        functools.partial(_experts_kernel, trash=n_tok),
        grid_spec=gs,
        out_shape=jax.ShapeDtypeStruct((n_tok + SUB, SUB, LANE), F32),
        compiler_params=_cparams(1, VMEM_BIG),
        name="experts",
    )(*items, tok_sorted, h2_tiles, w_col, wg, wu, wd)


def _dispatch_plan(idx, rank, w, cnt):
    n_tok = idx.shape[1]
    n_assign = n_tok * TOP_K
    n_tiles = n_assign // EXP_TM
    n_items = n_tiles + N_EXP
    off = jnp.cumsum(cnt) - cnt
    end = off + cnt
    dest = (off[idx] + rank).reshape(-1)
    tok = jnp.broadcast_to(jnp.arange(n_tok, dtype=jnp.int32)[None, :], idx.shape).reshape(-1)
    tok_sorted = jnp.zeros((n_assign,), jnp.int32).at[dest].set(tok, unique_indices=True)
    w_sorted = jnp.zeros((n_assign,), F32).at[dest].set(w.reshape(-1), unique_indices=True)
    first_tile = off // EXP_TM
    last_tile = jnp.maximum(end - 1, 0) // EXP_TM
    n_t = jnp.where(cnt > 0, last_tile - first_tile + 1, 0)
    item_end = jnp.cumsum(n_t)
    item_start = item_end - n_t
    total = item_end[-1]
    ii = jnp.arange(n_items, dtype=jnp.int32)
    valid = ii < total
    e_i = jnp.minimum(jnp.searchsorted(item_end, jnp.minimum(ii, total - 1), side="right"), N_EXP - 1).astype(jnp.int32)
    tile_i = jnp.where(valid, first_tile[e_i] + (ii - item_start[e_i]), n_tiles - 1).astype(jnp.int32)
    lo_i = jnp.where(valid, jnp.maximum(off[e_i] - tile_i * EXP_TM, 0), 0).astype(jnp.int32)
    hi_i = jnp.where(valid, jnp.minimum(end[e_i] - tile_i * EXP_TM, EXP_TM), 0).astype(jnp.int32)
    prev_tile = jnp.concatenate([jnp.full((1,), -1, jnp.int32), tile_i[:-1]])
    first_i = jnp.logical_and(valid, tile_i != prev_tile).astype(jnp.int32)
    return (tile_i, e_i, lo_i, hi_i, first_i), tok_sorted, w_sorted.reshape(n_assign, 1)


def _final_kernel(y_ref, x1_ref, m_ref, sg_ref, su_ref, sd_ref, g_ref, b_ref, o_ref):
    x1 = x1_ref[...]
    routed = _from_row_tiles(y_ref, TM)
    h2 = (x1 * (1.0 + m_ref[0, 4:5, :]) + m_ref[0, 3:4, :]).astype(BF16)
    act = (jax.nn.silu(_dot(h2, sg_ref[...])) * _dot(h2, su_ref[...])).astype(BF16)
    shared = _dot(act, sd_ref[...])
    z = ALPHA * x1 + m_ref[0, 5:6, :] * (routed + shared)
    o_ref[...] = _layernorm(z, g_ref[...], b_ref[...])


def _final(y_tiles, x1, mod, sg, su, sd, ln_g, ln_b, per_batch):
    rows = x1.shape[0]
    full = lambda a: pl.BlockSpec(a.shape, lambda i: (0,) * a.ndim)
    return pl.pallas_call(
        _final_kernel,
        grid=(rows // TM,),
        in_specs=[pl.BlockSpec((TM * SUB, LANE), lambda i: (i, 0)),
                  pl.BlockSpec((TM, D), lambda i: (i, 0)),
                  pl.BlockSpec((1, 6, D), lambda i: (_mod_row(i, per_batch), 0, 0)),
                  full(sg), full(su), full(sd), full(ln_g), full(ln_b)],
        out_specs=pl.BlockSpec((TM, D), lambda i: (i, 0)),
        out_shape=jax.ShapeDtypeStruct((rows, D), F32),
        compiler_params=_cparams(1),
        name="final",
    )(y_tiles, x1, mod, sg, su, sd, ln_g, ln_b)


def _rope_tables(rot_dim, lane_lo, n_rot, n_rows):
    nf = rot_dim // 4
    t = jnp.arange(n_rows)
    r = (t // GRID_W).astype(F32)
    col = (t % GRID_W).astype(F32)
    freqs = ROPE_BASE ** (-jnp.arange(nf, dtype=F32) / nf)
    lane = jnp.arange(LANE)
    j = (lane - lane_lo) % rot_dim
    in_rot = jnp.logical_and(lane >= lane_lo, lane < lane_lo + n_rot * rot_dim)
    half = j // (2 * nf)
    second = (j % (2 * nf)) >= nf
    f = freqs[j % nf]
    ang = jnp.where(half[None, :] == 0, r[:, None], col[:, None]) * f[None, :]
    cos = jnp.where(in_rot[None, :], jnp.cos(ang), 1.0)
    sin = jnp.where(in_rot[None, :], jnp.sin(ang), 0.0)
    sp = jnp.where(second[None, :], sin, 0.0)
    sm = jnp.where(second[None, :], 0.0, -sin)
    ident = lambda v: jnp.full((TM, LANE), v, F32)
    return (jnp.concatenate([ident(1.0), cos], axis=0), jnp.concatenate([ident(0.0), sp], axis=0),
            jnp.concatenate([ident(0.0), sm], axis=0))


def _even_weights(w_in, q_norm, w_uq, kv_norm, w_uk, w_uv, w_out):
    z = lambda n: jnp.zeros((D, n), F32)
    mla_in = MLA_QR + MLA_KVR + MLA_ROPE
    w_in_p = jnp.concatenate([w_in[:, :MLA_QR + MLA_KVR], z(MLA_NOPE), w_in[:, MLA_QR + MLA_KVR:mla_in],
                              z(MLA_HP - MLA_NOPE - MLA_ROPE), w_in[:, mla_in:]], axis=1)
    uq = w_uq.reshape(MLA_QR, MLA_HEADS, MLA_NOPE + MLA_ROPE)
    uq = jnp.pad(uq, ((0, 0), (0, 0), (0, MLA_HP - MLA_NOPE - MLA_ROPE))).reshape(MLA_QR, MLA_HEADS * MLA_HP)
    uk = w_uk.reshape(MLA_KVR, MLA_HEADS, MLA_NOPE)
    uk = jnp.pad(uk, ((0, 0), (0, 0), (0, MLA_HP - MLA_NOPE))).reshape(MLA_KVR, MLA_HEADS * MLA_HP)
    return {"w_in": w_in_p.astype(BF16), "q_norm": q_norm.reshape(1, -1), "kv_norm": kv_norm.reshape(1, -1),
            "w_uq": uq.astype(BF16), "w_uk": uk.astype(BF16), "w_uv": w_uv.astype(BF16), "w_out": w_out.astype(BF16)}


def kernel(x_prompt, x_sample, cache_mla_ckv, cache_mla_kpe, state_ret_fwd, state_ret_bwd, cache_swa_k, cache_swa_v, c, c_ctx, w_mod, b_mod, ln1_g, ln1_b, ln2_g, ln2_b, mla_ret_w_in, mla_q_norm, mla_w_uq, mla_kv_norm, mla_w_uk, mla_w_uv, ret_decay_fwd, ret_decay_bwd, ret_gn_g, even_w_out, swa_w_in, swa_sink, swa_w_out, moe_router, moe_router_bias, moe_w_gate, moe_w_up, moe_w_down, shared_w_gate, shared_w_up, shared_w_down):
    n_p, seq_p, _ = x_prompt.shape
    n_s, seq_s, _ = x_sample.shape
    past = cache_mla_ckv.shape[2]
    groups = [
        dict(x=x_prompt.reshape(n_p * seq_p, D), nb=n_p, seq=seq_p, per_batch=None),
        dict(x=x_sample.reshape(n_s * seq_s, D), nb=n_s, seq=seq_s, per_batch=seq_s // TM),
    ]
    cond8 = jnp.zeros((8, D), F32).at[0].set(c_ctx).at[1:1 + n_s].set(c)
    mods = _modulation(cond8, w_mod, b_mod).reshape(DEPTH, 8, 6, D)
    tabs_mla = _rope_tables(MLA_ROPE, MLA_NOPE, 1, seq_s)
    tabs_swa = _rope_tables(SWA_HD, 0, LANE // SWA_HD, seq_s)

    outs = {k: [] for k in ("ckv", "kpe", "rf", "rb", "sk", "sv")}
    for l in range(DEPTH):
        mod = mods[l]
        r_pad = jnp.pad(moe_router[l], ((0, 0), (0, LANE - N_EXP)))
        r_hi = r_pad.astype(BF16)
        r_lo = (r_pad - r_hi.astype(F32)).astype(BF16)
        bias_col = moe_router_bias[l].reshape(N_EXP, 1)
        sg, su, sd = (shared_w_gate[l].astype(BF16), shared_w_up[l].astype(BF16), shared_w_down[l].astype(BF16))
        if l % 2 == 0:
            e = l // 2
            ew = _even_weights(mla_ret_w_in[e], mla_q_norm[e], mla_w_uq[e], mla_kv_norm[e], mla_w_uk[e],
                               mla_w_uv[e], even_w_out[e])
            kpe_ctx = jnp.pad(cache_mla_kpe[:, e].reshape(n_s * past, MLA_ROPE),
                              ((0, 0), (MLA_NOPE, MLA_HP - MLA_NOPE - MLA_ROPE)))
            ctx_kv = _mla_ctx(cache_mla_ckv[:, e].reshape(n_s * past, MLA_KVR), kpe_ctx, ew)
            gn = ret_gn_g[e].reshape(1, RET_W)
        else:
            o = l // 2
            w_in_o = swa_w_in[o].astype(BF16)
            w_out_o = swa_w_out[o].astype(BF16)
            ctx_swa = (cache_swa_k[:, o].reshape(n_s * past, SWA_KV).astype(BF16),
                       cache_swa_v[:, o].reshape(n_s * past, SWA_KV).astype(BF16))
        for gi, g in enumerate(groups):
            x, nb, seq, per_batch = g["x"], g["nb"], g["seq"], g["per_batch"]
            is_sample = gi == 1
            if l % 2 == 0:
                proj = _modmm(x, mod, ew["w_in"], per_batch)
                q, k, v, ckv, kpe = _mla_prep(proj, tabs_mla, ew, per_batch)
                o_mla = _attention(q, k, v, ctx_kv if is_sample else None, None, n_batch=nb, seq=seq,
                                   n_heads=MLA_HEADS, group=1, dqk=MLA_HP, dv=MLA_V,
                                   scale=(MLA_NOPE + MLA_ROPE) ** -0.5, window=0)
                init = (state_ret_fwd[:, e], state_ret_bwd[:, e]) if is_sample else None
                ret = _retention(proj, ret_decay_fwd[e], ret_decay_bwd[e], gn, init, n_batch=nb, seq=seq,
                                 out_state=not is_sample)
                if is_sample:
                    o_ret = ret[0]
                else:
                    o_ret, s_f, s_b = ret
                    outs["ckv"].append(ckv.reshape(nb, seq, MLA_KVR))
                    outs["kpe"].append(kpe[:, MLA_NOPE:MLA_NOPE + MLA_ROPE].reshape(nb, seq, MLA_ROPE))
                    outs["rf"].append(s_f)
                    outs["rb"].append(s_b)
                parts, w_out = [o_mla, o_ret], ew["w_out"]
            else:
                proj = _modmm(x, mod, w_in_o, per_batch)
                q, k, v = _swa_prep(proj, tabs_swa, per_batch)
                o_swa = _attention(q, k, v, ctx_swa if is_sample else None, swa_sink[o], n_batch=nb, seq=seq,
                                   n_heads=SWA_HEADS, group=SWA_HEADS // SWA_KVH, dqk=SWA_HD, dv=SWA_HD,
                                   scale=SWA_HD ** -0.5, window=WINDOW if is_sample else 0)
                if not is_sample:
                    outs["sk"].append(proj[:, SWA_Q:SWA_Q + SWA_KV].reshape(nb, seq, SWA_KVH, SWA_HD))
                    outs["sv"].append(proj[:, SWA_Q + SWA_KV:].reshape(nb, seq, SWA_KVH, SWA_HD))
                parts, w_out = [o_swa], w_out_o
            x1, h2_tiles, logits = _post_mixer(parts, w_out, x, mod, ln1_g[l].reshape(1, D), ln1_b[l].reshape(1, D),
                                               r_hi, r_lo, per_batch)
            idx, rank, w, cnt = _route(logits, bias_col)
            items, tok_sorted, w_col = _dispatch_plan(idx, rank, w, cnt[:, 0])
            y_tiles = _experts(items, tok_sorted, h2_tiles.reshape(-1, SUB, LANE), w_col,
                               moe_w_gate, moe_w_up, moe_w_down, l)
            g["x"] = _final(y_tiles.reshape(-1, LANE), x1, mod, sg, su, sd,
                            ln2_g[l].reshape(1, D), ln2_b[l].reshape(1, D), per_batch)
    y_prompt = groups[0]["x"].reshape(n_p, seq_p, D)
    y_sample = groups[1]["x"].reshape(n_s, seq_s, D)
    return (y_prompt, y_sample, jnp.stack(outs["ckv"], axis=1), jnp.stack(outs["kpe"], axis=1),
            jnp.stack(outs["rf"], axis=1), jnp.stack(outs["rb"], axis=1),
            jnp.stack(outs["sk"], axis=1), jnp.stack(outs["sv"], axis=1))
```

```python
import functools

import jax
import jax.numpy as jnp
from jax import lax
from jax.experimental import pallas as pl
from jax.experimental.pallas import tpu as pltpu

F32 = jnp.float32
BF16 = jnp.bfloat16

D = 1024
DEPTH = 4
GRID_W = 64
ALPHA = (2.0 * DEPTH) ** 0.25
LN_EPS = 1e-5
RMS_EPS = 1e-6
ROPE_BASE = 10000.0
NEG_INF = -1e30
MLA_HEADS = 8
MLA_NOPE = 64
MLA_ROPE = 32
MLA_V = 64
MLA_QR = 256
MLA_KVR = 128
MLA_HP = 128
RET_HEADS = 8
RET_HD = 64
RET_CHUNK = 128
RET_W = RET_HEADS * RET_HD
EVEN_P = 512 + 4 * RET_W
SWA_HEADS = 16
SWA_KVH = 4
SWA_HD = 64
WINDOW = 128
SWA_Q = SWA_HEADS * SWA_HD
SWA_KV = SWA_KVH * SWA_HD
N_EXP = 64
TOP_K = 8
N_GROUPS = 8
TOPK_GROUPS = 4
EXP_D = 256
ROUTED_SCALE = 2.5

TM = 256
SUB = 8
LANE = 128
EXP_TM = 256
VMEM_BIG = 56 * 1024 * 1024


def _cparams(n_axes, vmem=None):
    return pltpu.CompilerParams(dimension_semantics=("arbitrary",) * n_axes, vmem_limit_bytes=vmem)


def _dot(a, b):
    return jnp.dot(a, b, preferred_element_type=F32)


def _dot_nt(a, b):
    return lax.dot_general(a, b, (((1,), (1,)), ((), ())), preferred_element_type=F32)


def _dot_tn(a, b):
    return lax.dot_general(a, b, (((0,), (0,)), ((), ())), preferred_element_type=F32)


def _layernorm(z, g, b):
    mu = jnp.mean(z, axis=-1, keepdims=True)
    zc = z - mu
    var = jnp.mean(zc * zc, axis=-1, keepdims=True)
    return zc * lax.rsqrt(var + LN_EPS) * g + b


def _rmsnorm(x, g):
    return x * lax.rsqrt(jnp.mean(x * x, axis=-1, keepdims=True) + RMS_EPS) * g


def _rope(x, c, sp, sm, shift):
    w = x.shape[-1]
    return x * c + pltpu.roll(x, shift, 1) * sp + pltpu.roll(x, w - shift, 1) * sm


def _mod_kernel(c_ref, w_ref, b_ref, o_ref):
    s = jax.nn.silu(c_ref[...]).astype(BF16)
    o_ref[0] = _dot(s, w_ref[0].astype(BF16)) + b_ref[0]


def _modulation(cond8, w_mod, b_mod):
    nt = 4
    tn = 6 * D // nt
    return pl.pallas_call(
        _mod_kernel,
        grid=(DEPTH, nt),
        in_specs=[pl.BlockSpec((8, D), lambda l, j: (0, 0)),
                  pl.BlockSpec((1, D, tn), lambda l, j: (l, 0, j)),
                  pl.BlockSpec((1, 1, tn), lambda l, j: (l, 0, j))],
        out_specs=pl.BlockSpec((1, 8, tn), lambda l, j: (l, 0, j)),
        out_shape=jax.ShapeDtypeStruct((DEPTH, 8, 6 * D), F32),
        compiler_params=_cparams(2, 40 * 1024 * 1024),
        name="modulation",
    )(cond8, w_mod, b_mod.reshape(DEPTH, 1, 6 * D))


def _mod_row(i, per_batch):
    return 0 if per_batch is None else 1 + i // per_batch


def _modmm_kernel(x_ref, m_ref, w_ref, o_ref, *, shift_i, scale_i):
    h = x_ref[...] * (1.0 + m_ref[0, scale_i:scale_i + 1, :]) + m_ref[0, shift_i:shift_i + 1, :]
    o_ref[...] = _dot(h.astype(BF16), w_ref[...])


def _modmm(x, mod, w, per_batch):
    rows = x.shape[0]
    n = w.shape[1]
    return pl.pallas_call(
        functools.partial(_modmm_kernel, shift_i=0, scale_i=1),
        grid=(rows // TM,),
        in_specs=[pl.BlockSpec((TM, D), lambda i: (i, 0)),
                  pl.BlockSpec((1, 6, D), lambda i: (_mod_row(i, per_batch), 0, 0)),
                  pl.BlockSpec((D, n), lambda i: (0, 0))],
        out_specs=pl.BlockSpec((TM, n), lambda i: (i, 0)),
        out_shape=jax.ShapeDtypeStruct((rows, n), F32),
        compiler_params=_cparams(1, 40 * 1024 * 1024),
        name="modmm",
    )(x, mod, w)


def _mla_prep_kernel(p_ref, c_ref, sp_ref, sm_ref, qn_ref, kvn_ref, wuq_ref, wuk_ref, wuv_ref,
                     q_ref, k_ref, v_ref, ckv_ref, kpe_ref):
    p = p_ref[...]
    c, sp, sm = c_ref[...], sp_ref[...], sm_ref[...]
    qn = _rmsnorm(p[:, 0:MLA_QR], qn_ref[...])
    q = _dot(qn.astype(BF16), wuq_ref[...])
    ckv = _rmsnorm(p[:, MLA_QR:MLA_QR + MLA_KVR], kvn_ref[...])
    ckv_ref[...] = ckv
    kpe = _rope(p[:, MLA_QR + MLA_KVR:512], c, sp, sm, MLA_ROPE // 4)
    kpe_ref[...] = kpe
    ckv_b = ckv.astype(BF16)
    kn = _dot(ckv_b, wuk_ref[...])
    for h in range(MLA_HEADS):
        sl = slice(MLA_HP * h, MLA_HP * (h + 1))
        q_ref[:, sl] = _rope(q[:, sl], c, sp, sm, MLA_ROPE // 4).astype(BF16)
        k_ref[:, sl] = (kn[:, sl] + kpe).astype(BF16)
    v_ref[...] = _dot(ckv_b, wuv_ref[...]).astype(BF16)


def _mla_prep(proj, tabs, ew, per_batch):
    rows = proj.shape[0]
    c, sp, sm = tabs

    def tab_idx(i):
        return (0 if per_batch is None else 1 + i % per_batch, 0)

    tab_spec = pl.BlockSpec((TM, LANE), tab_idx)
    full = lambda a: pl.BlockSpec(a.shape, lambda i: (0,) * a.ndim)
    return pl.pallas_call(
        _mla_prep_kernel,
        grid=(rows // TM,),
        in_specs=[pl.BlockSpec((TM, 512), lambda i: (i, 0)), tab_spec, tab_spec, tab_spec,
                  full(ew["q_norm"]), full(ew["kv_norm"]), full(ew["w_uq"]), full(ew["w_uk"]), full(ew["w_uv"])],
        out_specs=[pl.BlockSpec((TM, MLA_HEADS * MLA_HP), lambda i: (i, 0)),
                   pl.BlockSpec((TM, MLA_HEADS * MLA_HP), lambda i: (i, 0)),
                   pl.BlockSpec((TM, MLA_HEADS * MLA_V), lambda i: (i, 0)),
                   pl.BlockSpec((TM, MLA_KVR), lambda i: (i, 0)),
                   pl.BlockSpec((TM, LANE), lambda i: (i, 0))],
        out_shape=[jax.ShapeDtypeStruct((rows, MLA_HEADS * MLA_HP), BF16),
                   jax.ShapeDtypeStruct((rows, MLA_HEADS * MLA_HP), BF16),
                   jax.ShapeDtypeStruct((rows, MLA_HEADS * MLA_V), BF16),
                   jax.ShapeDtypeStruct((rows, MLA_KVR), F32),
                   jax.ShapeDtypeStruct((rows, LANE), F32)],
        compiler_params=_cparams(1),
        name="mla_prep",
    )(proj, c, sp, sm, ew["q_norm"], ew["kv_norm"], ew["w_uq"], ew["w_uk"], ew["w_uv"])


def _mla_ctx_kernel(ckv_ref, kpe_ref, wuk_ref, wuv_ref, k_ref, v_ref):
    ckv_b = ckv_ref[...].astype(BF16)
    kn = _dot(ckv_b, wuk_ref[...])
    kpe = kpe_ref[...]
    for h in range(MLA_HEADS):
        sl = slice(MLA_HP * h, MLA_HP * (h + 1))
        k_ref[:, sl] = (kn[:, sl] + kpe).astype(BF16)
    v_ref[...] = _dot(ckv_b, wuv_ref[...]).astype(BF16)


def _mla_ctx(ckv, kpe_pad, ew):
    rows = ckv.shape[0]
    full = lambda a: pl.BlockSpec(a.shape, lambda i: (0,) * a.ndim)
    return pl.pallas_call(
        _mla_ctx_kernel,
        grid=(rows // TM,),
        in_specs=[pl.BlockSpec((TM, MLA_KVR), lambda i: (i, 0)), pl.BlockSpec((TM, LANE), lambda i: (i, 0)),
                  full(ew["w_uk"]), full(ew["w_uv"])],
        out_specs=[pl.BlockSpec((TM, MLA_HEADS * MLA_HP), lambda i: (i, 0)),
                   pl.BlockSpec((TM, MLA_HEADS * MLA_V), lambda i: (i, 0))],
        out_shape=[jax.ShapeDtypeStruct((rows, MLA_HEADS * MLA_HP), BF16),
                   jax.ShapeDtypeStruct((rows, MLA_HEADS * MLA_V), BF16)],
        compiler_params=_cparams(1),
        name="mla_ctx",
    )(ckv, kpe_pad, ew["w_uk"], ew["w_uv"])


def _attn_kernel(*refs, n_heads, group, dqk, dv, scale, has_ctx, has_sink, window, tq, seq):
    refs = list(refs)
    sink_ref = refs.pop(0) if has_sink else None
    q_ref, k_ref, v_ref = refs[:3]
    kc_ref, vc_ref = (refs[3], refs[4]) if has_ctx else (None, None)
    o_ref = refs[-1]
    i = pl.program_id(1)
    if window:
        kw = tq + 2 * window
        start = pl.multiple_of(jnp.clip(i * tq - window, 0, seq - kw), LANE)
        qpos = i * tq + lax.broadcasted_iota(jnp.int32, (tq, kw), 0)
        kpos = start + lax.broadcasted_iota(jnp.int32, (tq, kw), 1)
        valid = jnp.abs(qpos - kpos) <= window
    for h in range(n_heads):
        hk = h // group
        q = q_ref[:, h * dqk:(h + 1) * dqk]
        if window:
            k = k_ref[pl.ds(start, kw), hk * dqk:(hk + 1) * dqk]
            v = v_ref[pl.ds(start, kw), hk * dv:(hk + 1) * dv]
            s = jnp.where(valid, _dot_nt(q, k) * scale, NEG_INF)
        else:
            k = k_ref[:, hk * dqk:(hk + 1) * dqk]
            v = v_ref[:, hk * dv:(hk + 1) * dv]
            s = _dot_nt(q, k) * scale
        m = jnp.max(s, axis=-1, keepdims=True)
        if has_ctx:
            sc = _dot_nt(q, kc_ref[:, hk * dqk:(hk + 1) * dqk]) * scale
            m = jnp.maximum(m, jnp.max(sc, axis=-1, keepdims=True))
        if has_sink:
            sk = sink_ref[h]
            m = jnp.maximum(m, sk)
        p = jnp.exp(s - m)
        l = jnp.sum(p, axis=-1, keepdims=True)
        o = _dot(p.astype(BF16), v)
        if has_ctx:
            pc = jnp.exp(sc - m)
            l = l + jnp.sum(pc, axis=-1, keepdims=True)
            o = o + _dot(pc.astype(BF16), vc_ref[:, hk * dv:(hk + 1) * dv])
        if has_sink:
            l = l + jnp.exp(sk - m)
        o_ref[:, h * dv:(h + 1) * dv] = (o / l).astype(o_ref.dtype)


def _attention(q, k, v, ctx, sink, *, n_batch, seq, n_heads, group, dqk, dv, scale, window):
    tq = TM
    nq = seq // tq
    n_kv = n_heads // group
    in_specs = []
    args = []
    if sink is not None:
        in_specs.append(pl.BlockSpec(memory_space=pltpu.SMEM))
        args.append(sink)
    in_specs += [pl.BlockSpec((tq, n_heads * dqk), lambda b, i: (b * nq + i, 0)),
                 pl.BlockSpec((seq, n_kv * dqk), lambda b, i: (b, 0)),
                 pl.BlockSpec((seq, n_kv * dv), lambda b, i: (b, 0))]
    args += [q, k, v]
    if ctx is not None:
        kc, vc = ctx
        sc = kc.shape[0] // n_batch
        in_specs += [pl.BlockSpec((sc, n_kv * dqk), lambda b, i: (b, 0)),
                     pl.BlockSpec((sc, n_kv * dv), lambda b, i: (b, 0))]
        args += [kc, vc]
    kern = functools.partial(_attn_kernel, n_heads=n_heads, group=group, dqk=dqk, dv=dv, scale=scale,
                             has_ctx=ctx is not None, has_sink=sink is not None, window=window, tq=tq, seq=seq)
    return pl.pallas_call(
        kern,
        grid=(n_batch, nq),
        in_specs=in_specs,
        out_specs=pl.BlockSpec((tq, n_heads * dv), lambda b, i: (b * nq + i, 0)),
        out_shape=jax.ShapeDtypeStruct((n_batch * seq, n_heads * dv), BF16),
        compiler_params=_cparams(2, 40 * 1024 * 1024),
        name="attention",
    )(*args)


def _ret_kernel(*refs, seq, has_init, out_state):
    refs = list(refs)
    df_ref, db_ref, rq_ref, rk_ref, rv_ref, rg_ref, gn_ref = refs[:7]
    pos = 7
    if has_init:
        s0f_ref, s0b_ref = refs[pos], refs[pos + 1]
        pos += 2
    o_ref = refs[pos]
    pos += 1
    if out_state:
        sf_ref, sb_ref = refs[pos], refs[pos + 1]
        pos += 2
    of_scr, ob_scr = refs[pos], refs[pos + 1]

    pair = pl.program_id(1)
    n_chunks = seq // RET_CHUNK
    idx_c = lax.broadcasted_iota(jnp.int32, (RET_CHUNK, 1), 0).astype(F32)
    diff = (lax.broadcasted_iota(jnp.int32, (RET_CHUNK, RET_CHUNK), 0)
            - lax.broadcasted_iota(jnp.int32, (RET_CHUNK, RET_CHUNK), 1)).astype(F32)
    for hh in range(2):
        h = 2 * pair + hh
        hs = slice(RET_HD * hh, RET_HD * (hh + 1))
        rows = [slice(RET_CHUNK * ci, RET_CHUNK * (ci + 1)) for ci in range(n_chunks)]
        qs = [rq_ref[r, hs].astype(BF16) for r in rows]
        ks = [rk_ref[r, hs] * (RET_HD ** -0.5) for r in rows]
        vs = [rv_ref[r, hs].astype(BF16) for r in rows]
        qk = [_dot_nt(q, k.astype(BF16)) for q, k in zip(qs, ks)]
        for fwd in (True, False):
            d = jnp.full((1, 1), (df_ref if fwd else db_ref)[h], F32)
            lg = jnp.minimum(d, 0.0) - jnp.log1p(jnp.exp(-jnp.abs(d)))
            dd = diff if fwd else -diff
            mask = jnp.where(dd >= 0, jnp.exp(lg * jnp.maximum(dd, 0.0)), 0.0)
            if fwd:
                q_dec = jnp.exp(lg * (idx_c + 1.0))
                k_dec = jnp.exp(lg * (RET_CHUNK - 1.0 - idx_c))
            else:
                q_dec = jnp.exp(lg * (RET_CHUNK - idx_c))
                k_dec = jnp.exp(lg * idx_c)
            c_dec = jnp.exp(lg * RET_CHUNK)
            scr = of_scr if fwd else ob_scr
            if has_init:
                state0 = (s0f_ref if fwd else s0b_ref)[0, hh]
            else:
                state0 = jnp.zeros((RET_HD, RET_HD), F32)

            state = state0
            for ci in (range(n_chunks) if fwd else reversed(range(n_chunks))):
                o = _dot((qk[ci] * mask).astype(BF16), vs[ci]) + _dot(qs[ci], state.astype(BF16)) * q_dec
                scr[rows[ci], hs] = o
                state = state * c_dec + _dot_tn((ks[ci] * k_dec).astype(BF16), vs[ci])
            if out_state:
                (sf_ref if fwd else sb_ref)[0, hh] = state

    def head_norm(x):
        mu = jnp.mean(x, axis=-1, keepdims=True)
        xc = x - mu
        return xc * lax.rsqrt(jnp.mean(xc * xc, axis=-1, keepdims=True) + LN_EPS)

    for hh in range(2):
        hs = slice(RET_HD * hh, RET_HD * (hh + 1))
        o = head_norm(of_scr[:, hs]) + head_norm(ob_scr[:, hs])
        o_ref[:, hs] = (o * gn_ref[:, hs] * jax.nn.silu(rg_ref[:, hs])).astype(o_ref.dtype)


def _retention(proj, dec_f, dec_b, gn, init, *, n_batch, seq, out_state):
    pairs = RET_HEADS // 2
    col0 = 512 // LANE

    def col_spec(k):
        return pl.BlockSpec((seq, LANE), lambda b, p: (b, col0 + k * pairs + p))

    smem = pl.BlockSpec(memory_space=pltpu.SMEM)
    st_spec = pl.BlockSpec((1, 2, RET_HD, RET_HD), lambda b, p: (b, p, 0, 0))
    in_specs = [smem, smem, col_spec(0), col_spec(1), col_spec(2), col_spec(3),
                pl.BlockSpec((1, LANE), lambda b, p: (0, p))]
    args = [dec_f, dec_b, proj, proj, proj, proj, gn]
    if init is not None:
        in_specs += [st_spec, st_spec]
        args += list(init)
    out_specs = [pl.BlockSpec((seq, LANE), lambda b, p: (b, p))]
    out_shape = [jax.ShapeDtypeStruct((n_batch * seq, RET_W), BF16)]
    if out_state:
        out_specs += [st_spec, st_spec]
        out_shape += [jax.ShapeDtypeStruct((n_batch, RET_HEADS, RET_HD, RET_HD), F32)] * 2
    kern = functools.partial(_ret_kernel, seq=seq, has_init=init is not None, out_state=out_state)
    return pl.pallas_call(
        kern,
        grid=(n_batch, pairs),
        in_specs=in_specs,
        out_specs=out_specs,
        out_shape=out_shape,
        scratch_shapes=[pltpu.VMEM((seq, LANE), F32), pltpu.VMEM((seq, LANE), F32)],
        compiler_params=_cparams(2),
        name="retention",
    )(*args)


def _swa_prep_kernel(p_ref, c_ref, sp_ref, sm_ref, q_ref, k_ref, v_ref):
    c, sp, sm = c_ref[...], sp_ref[...], sm_ref[...]
    for j in range(SWA_Q // LANE):
        sl = slice(LANE * j, LANE * (j + 1))
        q_ref[:, sl] = _rope(p_ref[:, sl], c, sp, sm, SWA_HD // 4).astype(BF16)
    for j in range(SWA_KV // LANE):
        sl = slice(LANE * j, LANE * (j + 1))
        k_ref[:, sl] = _rope(p_ref[:, SWA_Q + LANE * j:SWA_Q + LANE * (j + 1)], c, sp, sm, SWA_HD // 4).astype(BF16)
    v_ref[...] = p_ref[:, SWA_Q + SWA_KV:].astype(BF16)


def _swa_prep(proj, tabs, per_batch):
    rows = proj.shape[0]
    c, sp, sm = tabs

    def tab_idx(i):
        return (0 if per_batch is None else 1 + i % per_batch, 0)

    tab_spec = pl.BlockSpec((TM, LANE), tab_idx)
    return pl.pallas_call(
        _swa_prep_kernel,
        grid=(rows // TM,),
        in_specs=[pl.BlockSpec((TM, SWA_Q + 2 * SWA_KV), lambda i: (i, 0)), tab_spec, tab_spec, tab_spec],
        out_specs=[pl.BlockSpec((TM, SWA_Q), lambda i: (i, 0)),
                   pl.BlockSpec((TM, SWA_KV), lambda i: (i, 0)),
                   pl.BlockSpec((TM, SWA_KV), lambda i: (i, 0))],
        out_shape=[jax.ShapeDtypeStruct((rows, SWA_Q), BF16),
                   jax.ShapeDtypeStruct((rows, SWA_KV), BF16),
                   jax.ShapeDtypeStruct((rows, SWA_KV), BF16)],
        compiler_params=_cparams(1),
        name="swa_prep",
    )(proj, c, sp, sm)


def _to_row_tiles(ref, x):
    rows = x.shape[0]
    for s in range(D // LANE):
        ref[pl.ds(s, rows, stride=SUB), :] = x[:, LANE * s:LANE * (s + 1)]


def _from_row_tiles(ref, rows):
    return jnp.concatenate([ref[pl.ds(s, rows, stride=SUB), :] for s in range(D // LANE)], axis=1)


def _post_kernel(*refs, n_parts):
    a_refs = refs[:n_parts]
    w_ref, x_ref, m_ref, g_ref, b_ref, rh_ref, rl_ref, x1_ref, h2_ref, lg_ref = refs[n_parts:]
    out = None
    off = 0
    for a_ref in a_refs:
        kk = a_ref.shape[1]
        part = _dot(a_ref[...], w_ref[off:off + kk, :])
        out = part if out is None else out + part
        off += kk
    z = ALPHA * x_ref[...] + m_ref[0, 2:3, :] * out
    x1 = _layernorm(z, g_ref[...], b_ref[...])
    x1_ref[...] = x1
    h2 = x1 * (1.0 + m_ref[0, 4:5, :]) + m_ref[0, 3:4, :]
    _to_row_tiles(h2_ref, h2)
    hi = h2.astype(BF16)
    lo = (h2 - hi.astype(F32)).astype(BF16)
    rh = rh_ref[...]
    lg_ref[...] = _dot(hi, rh) + _dot(lo, rh) + _dot(hi, rl_ref[...])


def _post_mixer(parts, w_out, x, mod, ln_g, ln_b, r_hi, r_lo, per_batch):
    rows = x.shape[0]
    full = lambda a: pl.BlockSpec(a.shape, lambda i: (0,) * a.ndim)
    in_specs = [pl.BlockSpec((TM, a.shape[1]), lambda i: (i, 0)) for a in parts]
    in_specs += [full(w_out), pl.BlockSpec((TM, D), lambda i: (i, 0)),
                 pl.BlockSpec((1, 6, D), lambda i: (_mod_row(i, per_batch), 0, 0)),
                 full(ln_g), full(ln_b), full(r_hi), full(r_lo)]
    return pl.pallas_call(
        functools.partial(_post_kernel, n_parts=len(parts)),
        grid=(rows // TM,),
        in_specs=in_specs,
        out_specs=[pl.BlockSpec((TM, D), lambda i: (i, 0)),
                   pl.BlockSpec((TM * SUB, LANE), lambda i: (i, 0)),
                   pl.BlockSpec((TM, LANE), lambda i: (i, 0))],
        out_shape=[jax.ShapeDtypeStruct((rows, D), F32),
                   jax.ShapeDtypeStruct((rows * SUB, LANE), F32),
                   jax.ShapeDtypeStruct((rows, LANE), F32)],
        compiler_params=_cparams(1),
        name="post_mixer",
    )(*parts, w_out, x, mod, ln_g, ln_b, r_hi, r_lo)


def _route_kernel(lg_ref, bias_ref, idx_ref, rank_ref, w_ref, cnt_ref, carry_ref):
    i = pl.program_id(0)

    @pl.when(i == 0)
    def _():
        carry_ref[...] = jnp.zeros_like(carry_ref)

    t = lg_ref.shape[0]
    gsz = N_EXP // N_GROUPS
    scores = jax.nn.sigmoid(lg_ref[...].T[:N_EXP])
    sel = scores + bias_ref[...]
    g3 = sel.reshape(N_GROUPS, gsz, t)
    sub_iota = lax.broadcasted_iota(jnp.int32, g3.shape, 1)
    m1 = jnp.max(g3, axis=1)
    first = jnp.min(jnp.where(g3 == m1[:, None, :], sub_iota, gsz), axis=1)
    m2 = jnp.max(jnp.where(sub_iota == first[:, None, :], -jnp.inf, g3), axis=1)
    grp = m1 + m2
    g_iota = lax.broadcasted_iota(jnp.int32, grp.shape, 0)
    gmask = jnp.zeros(grp.shape, jnp.bool_)
    for _ in range(TOPK_GROUPS):
        gm = jnp.max(grp, axis=0, keepdims=True)
        gi = jnp.min(jnp.where(grp == gm, g_iota, N_GROUPS), axis=0, keepdims=True)
        hit = g_iota == gi
        gmask = jnp.logical_or(gmask, hit)
        grp = jnp.where(hit, -jnp.inf, grp)
    emask = jnp.broadcast_to(gmask[:, None, :], g3.shape).reshape(N_EXP, t)
    cur = jnp.where(emask, sel, NEG_INF)
    e_iota = lax.broadcasted_iota(jnp.int32, cur.shape, 0)
    hits = []
    member = jnp.zeros(cur.shape, F32)
    for _ in range(TOP_K):
        cm = jnp.max(cur, axis=0, keepdims=True)
        ci = jnp.min(jnp.where(cur == cm, e_iota, N_EXP), axis=0, keepdims=True)
        hit = e_iota == ci
        hits.append((hit, ci))
        member = member + hit.astype(F32)
        cur = jnp.where(hit, -jnp.inf, cur)
    tri = (lax.broadcasted_iota(jnp.int32, (t, t), 0) < lax.broadcasted_iota(jnp.int32, (t, t), 1)).astype(BF16)
    before = _dot(member.astype(BF16), tri) + carry_ref[:, 0:1]
    ws = [jnp.sum(jnp.where(hit, scores, 0.0), axis=0, keepdims=True) for hit, _ in hits]
    wsum = ws[0]
    for w in ws[1:]:
        wsum = wsum + w
    for k, (hit, ci) in enumerate(hits):
        idx_ref[k:k + 1, :] = ci
        rank_ref[k:k + 1, :] = jnp.sum(jnp.where(hit, before, 0.0), axis=0, keepdims=True).astype(jnp.int32)
        w_ref[k:k + 1, :] = ws[k] / wsum * ROUTED_SCALE
    total = carry_ref[...] + jnp.sum(member, axis=1, keepdims=True)
    carry_ref[...] = total
    cnt_ref[...] = total.astype(jnp.int32)


def _route(logits, bias_col):
    rows = logits.shape[0]
    row_spec = pl.BlockSpec((TOP_K, TM), lambda i: (0, i))
    return pl.pallas_call(
        _route_kernel,
        grid=(rows // TM,),
        in_specs=[pl.BlockSpec((TM, LANE), lambda i: (i, 0)), pl.BlockSpec((N_EXP, 1), lambda i: (0, 0))],
        out_specs=[row_spec, row_spec, row_spec, pl.BlockSpec((N_EXP, LANE), lambda i: (0, 0))],
        out_shape=[jax.ShapeDtypeStruct((TOP_K, rows), jnp.int32),
                   jax.ShapeDtypeStruct((TOP_K, rows), jnp.int32),
                   jax.ShapeDtypeStruct((TOP_K, rows), F32),
                   jax.ShapeDtypeStruct((N_EXP, LANE), jnp.int32)],
        scratch_shapes=[pltpu.VMEM((N_EXP, LANE), F32)],
        compiler_params=_cparams(1),
        name="route",
    )(logits, bias_col)


def _experts_kernel(tile_ref, exp_ref, lo_ref, hi_ref, first_ref, tok_ref, wrow_ref,
                    x_ref, wg_ref, wu_ref, wd_ref, y_ref, xbuf, obuf, *, trash):
    i = pl.program_id(0)
    lo = lo_ref[i]
    hi = hi_ref[i]
    base = tile_ref[i] * EXP_TM

    @pl.when(i == 0)
    def _():
        y_ref[...] = jnp.zeros_like(y_ref)

    @pl.when(hi > lo)
    def _():
        @pl.when(first_ref[i] == 1)
        def _():
            def gather(j, c):
                for u in range(SUB):
                    r = j * SUB + u
                    xbuf[pl.ds(pl.multiple_of(r * SUB, SUB), SUB), :] = x_ref[tok_ref[base + r]]
                return c
            lax.fori_loop(0, EXP_TM // SUB, gather, 0)

        x = _from_row_tiles(xbuf, EXP_TM).astype(BF16)
        g = _dot(x, wg_ref[...].astype(BF16))
        u = _dot(x, wu_ref[...].astype(BF16))
        h = (jax.nn.silu(g) * u).astype(BF16)
        _to_row_tiles(obuf, _dot(h, wd_ref[...].astype(BF16)))

        def scatter(j, c):
            vals = []
            for u in range(SUB):
                r = j * SUB + u
                t = jnp.where(jnp.logical_and(r >= lo, r < hi), tok_ref[base + r], trash)
                row = obuf[pl.ds(pl.multiple_of(r * SUB, SUB), SUB), :]
                vals.append((t, y_ref[t] + wrow_ref[base + r] * row))
            for t, v in vals:
                y_ref[t] = v
            return c
        lax.fori_loop(lo // SUB, (hi + SUB - 1) // SUB, scatter, 0)


def _experts(items, tok_sorted, w_sorted, h2_tiles, wg, wu, wd, layer):
    n_tok = h2_tiles.shape[0]
    n_items = items[0].shape[0]
    wmap = lambda i, tile, exp, *_: (layer, exp[i], 0, 0)
    gs = pltpu.PrefetchScalarGridSpec(
        num_scalar_prefetch=7,
        grid=(n_items,),
        in_specs=[pl.BlockSpec(memory_space=pltpu.VMEM),
                  pl.BlockSpec((None, None, D, EXP_D), wmap),
                  pl.BlockSpec((None, None, D, EXP_D), wmap),
                  pl.BlockSpec((None, None, EXP_D, D), wmap)],
        out_specs=pl.BlockSpec(memory_space=pltpu.VMEM),
        scratch_shapes=[pltpu.VMEM((EXP_TM * SUB, LANE), F32), pltpu.VMEM((EXP_TM * SUB, LANE), F32)],
    )
    return pl.pallas_call(
        functools.partial(_experts_kernel, trash=n_tok),
        grid_spec=gs,
        out_shape=jax.ShapeDtypeStruct((n_tok + SUB, SUB, LANE), F32),
        compiler_params=_cparams(1, VMEM_BIG),
        name="experts",
    )(*items, tok_sorted, w_sorted, h2_tiles, wg, wu, wd)


def _invert_kernel(dest_ref, w_ref, tok_ref, ws_ref, *, n_tok):
    def body(j, c):
        for u in range(SUB):
            a = j * SUB + u
            p = dest_ref[a]
            tok_ref[p] = a & (n_tok - 1)
            ws_ref[p] = w_ref[a]
        return c
    lax.fori_loop(0, dest_ref.shape[0] // SUB, body, 0)


def _invert(dest, w, n_tok):
    assert n_tok & (n_tok - 1) == 0
    smem = pl.BlockSpec(memory_space=pltpu.SMEM)
    return pl.pallas_call(
        functools.partial(_invert_kernel, n_tok=n_tok),
        in_specs=[smem, smem],
        out_specs=[smem, smem],
        out_shape=[jax.ShapeDtypeStruct(dest.shape, jnp.int32), jax.ShapeDtypeStruct(dest.shape, F32)],
        name="invert",
    )(dest, w)


def _dispatch_plan(idx, rank, w, cnt):
    n_tok = idx.shape[1]
    n_assign = n_tok * TOP_K
    n_tiles = n_assign // EXP_TM
    n_items = n_tiles + N_EXP
    off = jnp.cumsum(cnt) - cnt
    end = off + cnt
    e_ids = jnp.arange(N_EXP, dtype=jnp.int32)
    dest = (jnp.sum(jnp.where(idx[:, :, None] == e_ids, off, 0), axis=-1) + rank).reshape(-1)
    tok_sorted, w_sorted = _invert(dest, w.reshape(-1), n_tok)
    first_tile = off // EXP_TM
    last_tile = jnp.maximum(end - 1, 0) // EXP_TM
    n_t = jnp.where(cnt > 0, last_tile - first_tile + 1, 0)
    item_end = jnp.cumsum(n_t)
    item_start = item_end - n_t
    total = item_end[-1]
    ii = jnp.arange(n_items, dtype=jnp.int32)
    valid = ii < total
    e_i = jnp.sum(item_end[None, :] <= jnp.minimum(ii, total - 1)[:, None], axis=1).astype(jnp.int32)
    e_i = jnp.minimum(e_i, N_EXP - 1)
    tile_i = jnp.where(valid, first_tile[e_i] + (ii - item_start[e_i]), n_tiles - 1).astype(jnp.int32)
    lo_i = jnp.where(valid, jnp.maximum(off[e_i] - tile_i * EXP_TM, 0), 0).astype(jnp.int32)
    hi_i = jnp.where(valid, jnp.minimum(end[e_i] - tile_i * EXP_TM, EXP_TM), 0).astype(jnp.int32)
    prev_tile = jnp.concatenate([jnp.full((1,), -1, jnp.int32), tile_i[:-1]])
    first_i = jnp.logical_and(valid, tile_i != prev_tile).astype(jnp.int32)
    return (tile_i, e_i, lo_i, hi_i, first_i), tok_sorted, w_sorted


def _final_kernel(y_ref, x1_ref, m_ref, sg_ref, su_ref, sd_ref, g_ref, b_ref, o_ref):
    x1 = x1_ref[...]
    routed = _from_row_tiles(y_ref, TM)
    h2 = (x1 * (1.0 + m_ref[0, 4:5, :]) + m_ref[0, 3:4, :]).astype(BF16)
    act = (jax.nn.silu(_dot(h2, sg_ref[...])) * _dot(h2, su_ref[...])).astype(BF16)
    shared = _dot(act, sd_ref[...])
    z = ALPHA * x1 + m_ref[0, 5:6, :] * (routed + shared)
    o_ref[...] = _layernorm(z, g_ref[...], b_ref[...])


def _final(y_tiles, x1, mod, sg, su, sd, ln_g, ln_b, per_batch):
    rows = x1.shape[0]
    full = lambda a: pl.BlockSpec(a.shape, lambda i: (0,) * a.ndim)
    return pl.pallas_call(
        _final_kernel,
        grid=(rows // TM,),
        in_specs=[pl.BlockSpec((TM * SUB, LANE), lambda i: (i, 0)),
                  pl.BlockSpec((TM, D), lambda i: (i, 0)),
                  pl.BlockSpec((1, 6, D), lambda i: (_mod_row(i, per_batch), 0, 0)),
                  full(sg), full(su), full(sd), full(ln_g), full(ln_b)],
        out_specs=pl.BlockSpec((TM, D), lambda i: (i, 0)),
        out_shape=jax.ShapeDtypeStruct((rows, D), F32),
        compiler_params=_cparams(1),
        name="final",
    )(y_tiles, x1, mod, sg, su, sd, ln_g, ln_b)


def _rope_tables(rot_dim, lane_lo, n_rot, n_rows):
    nf = rot_dim // 4
    t = jnp.arange(n_rows)
    r = (t // GRID_W).astype(F32)
    col = (t % GRID_W).astype(F32)
    freqs = ROPE_BASE ** (-jnp.arange(nf, dtype=F32) / nf)
    lane = jnp.arange(LANE)
    j = (lane - lane_lo) % rot_dim
    in_rot = jnp.logical_and(lane >= lane_lo, lane < lane_lo + n_rot * rot_dim)
    half = j // (2 * nf)
    second = (j % (2 * nf)) >= nf
    f = freqs[j % nf]
    ang = jnp.where(half[None, :] == 0, r[:, None], col[:, None]) * f[None, :]
    cos = jnp.where(in_rot[None, :], jnp.cos(ang), 1.0)
    sin = jnp.where(in_rot[None, :], jnp.sin(ang), 0.0)
    sp = jnp.where(second[None, :], sin, 0.0)
    sm = jnp.where(second[None, :], 0.0, -sin)
    ident = lambda v: jnp.full((TM, LANE), v, F32)
    return (jnp.concatenate([ident(1.0), cos], axis=0), jnp.concatenate([ident(0.0), sp], axis=0),
            jnp.concatenate([ident(0.0), sm], axis=0))


def _even_weights(w_in, q_norm, w_uq, kv_norm, w_uk, w_uv, w_out):
    z = lambda n: jnp.zeros((D, n), F32)
    mla_in = MLA_QR + MLA_KVR + MLA_ROPE
    w_in_p = jnp.concatenate([w_in[:, :MLA_QR + MLA_KVR], z(MLA_NOPE), w_in[:, MLA_QR + MLA_KVR:mla_in],
                              z(MLA_HP - MLA_NOPE - MLA_ROPE), w_in[:, mla_in:]], axis=1)
    uq = w_uq.reshape(MLA_QR, MLA_HEADS, MLA_NOPE + MLA_ROPE)
    uq = jnp.pad(uq, ((0, 0), (0, 0), (0, MLA_HP - MLA_NOPE - MLA_ROPE))).reshape(MLA_QR, MLA_HEADS * MLA_HP)
    uk = w_uk.reshape(MLA_KVR, MLA_HEADS, MLA_NOPE)
    uk = jnp.pad(uk, ((0, 0), (0, 0), (0, MLA_HP - MLA_NOPE))).reshape(MLA_KVR, MLA_HEADS * MLA_HP)
    return {"w_in": w_in_p.astype(BF16), "q_norm": q_norm.reshape(1, -1), "kv_norm": kv_norm.reshape(1, -1),
            "w_uq": uq.astype(BF16), "w_uk": uk.astype(BF16), "w_uv": w_uv.astype(BF16), "w_out": w_out.astype(BF16)}


def kernel(x_prompt, x_sample, cache_mla_ckv, cache_mla_kpe, state_ret_fwd, state_ret_bwd, cache_swa_k, cache_swa_v, c, c_ctx, w_mod, b_mod, ln1_g, ln1_b, ln2_g, ln2_b, mla_ret_w_in, mla_q_norm, mla_w_uq, mla_kv_norm, mla_w_uk, mla_w_uv, ret_decay_fwd, ret_decay_bwd, ret_gn_g, even_w_out, swa_w_in, swa_sink, swa_w_out, moe_router, moe_router_bias, moe_w_gate, moe_w_up, moe_w_down, shared_w_gate, shared_w_up, shared_w_down):
    n_p, seq_p, _ = x_prompt.shape
    n_s, seq_s, _ = x_sample.shape
    past = cache_mla_ckv.shape[2]
    groups = [
        dict(x=x_prompt.reshape(n_p * seq_p, D), nb=n_p, seq=seq_p, per_batch=None),
        dict(x=x_sample.reshape(n_s * seq_s, D), nb=n_s, seq=seq_s, per_batch=seq_s // TM),
    ]
    cond8 = jnp.zeros((8, D), F32).at[0].set(c_ctx).at[1:1 + n_s].set(c)
    mods = _modulation(cond8, w_mod, b_mod).reshape(DEPTH, 8, 6, D)
    tabs_mla = _rope_tables(MLA_ROPE, MLA_NOPE, 1, seq_s)
    tabs_swa = _rope_tables(SWA_HD, 0, LANE // SWA_HD, seq_s)

    outs = {k: [] for k in ("ckv", "kpe", "rf", "rb", "sk", "sv")}
    for l in range(DEPTH):
        mod = mods[l]
        r_pad = jnp.pad(moe_router[l], ((0, 0), (0, LANE - N_EXP)))
        r_hi = r_pad.astype(BF16)
        r_lo = (r_pad - r_hi.astype(F32)).astype(BF16)
        bias_col = moe_router_bias[l].reshape(N_EXP, 1)
        sg, su, sd = (shared_w_gate[l].astype(BF16), shared_w_up[l].astype(BF16), shared_w_down[l].astype(BF16))
        if l % 2 == 0:
            e = l // 2
            ew = _even_weights(mla_ret_w_in[e], mla_q_norm[e], mla_w_uq[e], mla_kv_norm[e], mla_w_uk[e],
                               mla_w_uv[e], even_w_out[e])
            kpe_ctx = jnp.pad(cache_mla_kpe[:, e].reshape(n_s * past, MLA_ROPE),
                              ((0, 0), (MLA_NOPE, MLA_HP - MLA_NOPE - MLA_ROPE)))
            ctx_kv = _mla_ctx(cache_mla_ckv[:, e].reshape(n_s * past, MLA_KVR), kpe_ctx, ew)
            gn = ret_gn_g[e].reshape(1, RET_W)
        else:
            o = l // 2
            w_in_o = swa_w_in[o].astype(BF16)
            w_out_o = swa_w_out[o].astype(BF16)
            ctx_swa = (cache_swa_k[:, o].reshape(n_s * past, SWA_KV).astype(BF16),
                       cache_swa_v[:, o].reshape(n_s * past, SWA_KV).astype(BF16))
        for gi, g in enumerate(groups):
            x, nb, seq, per_batch = g["x"], g["nb"], g["seq"], g["per_batch"]
            is_sample = gi == 1
            if l % 2 == 0:
                proj = _modmm(x, mod, ew["w_in"], per_batch)
                q, k, v, ckv, kpe = _mla_prep(proj, tabs_mla, ew, per_batch)
                o_mla = _attention(q, k, v, ctx_kv if is_sample else None, None, n_batch=nb, seq=seq,
                                   n_heads=MLA_HEADS, group=1, dqk=MLA_HP, dv=MLA_V,
                                   scale=(MLA_NOPE + MLA_ROPE) ** -0.5, window=0)
                init = (state_ret_fwd[:, e], state_ret_bwd[:, e]) if is_sample else None
                ret = _retention(proj, ret_decay_fwd[e], ret_decay_bwd[e], gn, init, n_batch=nb, seq=seq,
                                 out_state=not is_sample)
                if is_sample:
                    o_ret = ret[0]
                else:
                    o_ret, s_f, s_b = ret
                    outs["ckv"].append(ckv.reshape(nb, seq, MLA_KVR))
                    outs["kpe"].append(kpe[:, MLA_NOPE:MLA_NOPE + MLA_ROPE].reshape(nb, seq, MLA_ROPE))
                    outs["rf"].append(s_f)
                    outs["rb"].append(s_b)
                parts, w_out = [o_mla, o_ret], ew["w_out"]
            else:
                proj = _modmm(x, mod, w_in_o, per_batch)
                q, k, v = _swa_prep(proj, tabs_swa, per_batch)
                o_swa = _attention(q, k, v, ctx_swa if is_sample else None, swa_sink[o], n_batch=nb, seq=seq,
                                   n_heads=SWA_HEADS, group=SWA_HEADS // SWA_KVH, dqk=SWA_HD, dv=SWA_HD,
                                   scale=SWA_HD ** -0.5, window=WINDOW if is_sample else 0)
                if not is_sample:
                    outs["sk"].append(proj[:, SWA_Q:SWA_Q + SWA_KV].reshape(nb, seq, SWA_KVH, SWA_HD))
                    outs["sv"].append(proj[:, SWA_Q + SWA_KV:].reshape(nb, seq, SWA_KVH, SWA_HD))
                parts, w_out = [o_swa], w_out_o
            x1, h2_tiles, logits = _post_mixer(parts, w_out, x, mod, ln1_g[l].reshape(1, D), ln1_b[l].reshape(1, D),
                                               r_hi, r_lo, per_batch)
            idx, rank, w, cnt = _route(logits, bias_col)
            items, tok_sorted, w_sorted = _dispatch_plan(idx, rank, w, cnt[:, 0])
            y_tiles = _experts(items, tok_sorted, w_sorted, h2_tiles.reshape(-1, SUB, LANE),
                               moe_w_gate, moe_w_up, moe_w_down, l)
            g["x"] = _final(y_tiles.reshape(-1, LANE), x1, mod, sg, su, sd,
                            ln2_g[l].reshape(1, D), ln2_b[l].reshape(1, D), per_batch)
    y_prompt = groups[0]["x"].reshape(n_p, seq_p, D)
    y_sample = groups[1]["x"].reshape(n_s, seq_s, D)
    return (y_prompt, y_sample, jnp.stack(outs["ckv"], axis=1), jnp.stack(outs["kpe"], axis=1),
            jnp.stack(outs["rf"], axis=1), jnp.stack(outs["rb"], axis=1),
            jnp.stack(outs["sk"], axis=1), jnp.stack(outs["sv"], axis=1))
```

```python
import functools

import jax
import jax.numpy as jnp
from jax import lax
from jax.experimental import pallas as pl
from jax.experimental.pallas import tpu as pltpu

F32 = jnp.float32
BF16 = jnp.bfloat16

D = 1024
DEPTH = 4
GRID_W = 64
ALPHA = (2.0 * DEPTH) ** 0.25
LN_EPS = 1e-5
RMS_EPS = 1e-6
ROPE_BASE = 10000.0
NEG_INF = -1e30
MLA_HEADS = 8
MLA_NOPE = 64
MLA_ROPE = 32
MLA_V = 64
MLA_QR = 256
MLA_KVR = 128
MLA_HP = 128
RET_HEADS = 8
RET_HD = 64
RET_CHUNK = 256
RET_W = RET_HEADS * RET_HD
EVEN_P = 512 + 4 * RET_W
SWA_HEADS = 16
SWA_KVH = 4
SWA_HD = 64
WINDOW = 128
SWA_Q = SWA_HEADS * SWA_HD
SWA_KV = SWA_KVH * SWA_HD
N_EXP = 64
TOP_K = 8
N_GROUPS = 8
TOPK_GROUPS = 4
EXP_D = 256
ROUTED_SCALE = 2.5

TM = 256
SUB = 8
LANE = 128
EXP_TM = 256
VMEM_BIG = 56 * 1024 * 1024


def _cparams(n_axes, vmem=None):
    return pltpu.CompilerParams(dimension_semantics=("arbitrary",) * n_axes, vmem_limit_bytes=vmem)


def _dot(a, b):
    return jnp.dot(a, b, preferred_element_type=F32)


def _dot_nt(a, b):
    return lax.dot_general(a, b, (((1,), (1,)), ((), ())), preferred_element_type=F32)


def _dot_tn(a, b):
    return lax.dot_general(a, b, (((0,), (0,)), ((), ())), preferred_element_type=F32)


def _layernorm(z, g, b):
    mu = jnp.mean(z, axis=-1, keepdims=True)
    zc = z - mu
    var = jnp.mean(zc * zc, axis=-1, keepdims=True)
    return zc * lax.rsqrt(var + LN_EPS) * g + b


def _rmsnorm(x, g):
    return x * lax.rsqrt(jnp.mean(x * x, axis=-1, keepdims=True) + RMS_EPS) * g


def _rope(x, c, sp, sm, shift):
    w = x.shape[-1]
    return x * c + pltpu.roll(x, shift, 1) * sp + pltpu.roll(x, w - shift, 1) * sm


def _mod_kernel(c_ref, w_ref, b_ref, o_ref):
    s = jax.nn.silu(c_ref[...]).astype(BF16)
    o_ref[0] = _dot(s, w_ref[0].astype(BF16)) + b_ref[0]


def _modulation(cond8, w_mod, b_mod):
    nt = 4
    tn = 6 * D // nt
    return pl.pallas_call(
        _mod_kernel,
        grid=(DEPTH, nt),
        in_specs=[pl.BlockSpec((8, D), lambda l, j: (0, 0)),
                  pl.BlockSpec((1, D, tn), lambda l, j: (l, 0, j)),
                  pl.BlockSpec((1, 1, tn), lambda l, j: (l, 0, j))],
        out_specs=pl.BlockSpec((1, 8, tn), lambda l, j: (l, 0, j)),
        out_shape=jax.ShapeDtypeStruct((DEPTH, 8, 6 * D), F32),
        compiler_params=_cparams(2, 40 * 1024 * 1024),
        name="modulation",
    )(cond8, w_mod, b_mod.reshape(DEPTH, 1, 6 * D))


def _mod_row(i, per_batch):
    return 0 if per_batch is None else 1 + i // per_batch


def _modmm_kernel(x_ref, m_ref, w_ref, o_ref, *, shift_i, scale_i):
    h = x_ref[...] * (1.0 + m_ref[0, scale_i:scale_i + 1, :]) + m_ref[0, shift_i:shift_i + 1, :]
    o_ref[...] = _dot(h.astype(BF16), w_ref[...])


def _modmm(x, mod, w, per_batch):
    rows = x.shape[0]
    n = w.shape[1]
    return pl.pallas_call(
        functools.partial(_modmm_kernel, shift_i=0, scale_i=1),
        grid=(rows // TM,),
        in_specs=[pl.BlockSpec((TM, D), lambda i: (i, 0)),
                  pl.BlockSpec((1, 6, D), lambda i: (_mod_row(i, per_batch), 0, 0)),
                  pl.BlockSpec((D, n), lambda i: (0, 0))],
        out_specs=pl.BlockSpec((TM, n), lambda i: (i, 0)),
        out_shape=jax.ShapeDtypeStruct((rows, n), F32),
        compiler_params=_cparams(1, 40 * 1024 * 1024),
        name="modmm",
    )(x, mod, w)


def _mla_prep_kernel(p_ref, c_ref, sp_ref, sm_ref, qn_ref, kvn_ref, wuq_ref, wuk_ref, wuv_ref,
                     q_ref, k_ref, v_ref, ckv_ref, kpe_ref):
    p = p_ref[...]
    c, sp, sm = c_ref[...], sp_ref[...], sm_ref[...]
    qn = _rmsnorm(p[:, 0:MLA_QR], qn_ref[...])
    q = _dot(qn.astype(BF16), wuq_ref[...])
    ckv = _rmsnorm(p[:, MLA_QR:MLA_QR + MLA_KVR], kvn_ref[...])
    ckv_ref[...] = ckv
    kpe = _rope(p[:, MLA_QR + MLA_KVR:512], c, sp, sm, MLA_ROPE // 4)
    kpe_ref[...] = kpe
    ckv_b = ckv.astype(BF16)
    kn = _dot(ckv_b, wuk_ref[...])
    for h in range(MLA_HEADS):
        sl = slice(MLA_HP * h, MLA_HP * (h + 1))
        q_ref[:, sl] = _rope(q[:, sl], c, sp, sm, MLA_ROPE // 4).astype(BF16)
        k_ref[:, sl] = (kn[:, sl] + kpe).astype(BF16)
    v_ref[...] = _dot(ckv_b, wuv_ref[...]).astype(BF16)


def _mla_prep(proj, tabs, ew, per_batch):
    rows = proj.shape[0]
    c, sp, sm = tabs

    def tab_idx(i):
        return (0 if per_batch is None else 1 + i % per_batch, 0)

    tab_spec = pl.BlockSpec((TM, LANE), tab_idx)
    full = lambda a: pl.BlockSpec(a.shape, lambda i: (0,) * a.ndim)
    return pl.pallas_call(
        _mla_prep_kernel,
        grid=(rows // TM,),
        in_specs=[pl.BlockSpec((TM, 512), lambda i: (i, 0)), tab_spec, tab_spec, tab_spec,
                  full(ew["q_norm"]), full(ew["kv_norm"]), full(ew["w_uq"]), full(ew["w_uk"]), full(ew["w_uv"])],
        out_specs=[pl.BlockSpec((TM, MLA_HEADS * MLA_HP), lambda i: (i, 0)),
                   pl.BlockSpec((TM, MLA_HEADS * MLA_HP), lambda i: (i, 0)),
                   pl.BlockSpec((TM, MLA_HEADS * MLA_V), lambda i: (i, 0)),
                   pl.BlockSpec((TM, MLA_KVR), lambda i: (i, 0)),
                   pl.BlockSpec((TM, LANE), lambda i: (i, 0))],
        out_shape=[jax.ShapeDtypeStruct((rows, MLA_HEADS * MLA_HP), BF16),
                   jax.ShapeDtypeStruct((rows, MLA_HEADS * MLA_HP), BF16),
                   jax.ShapeDtypeStruct((rows, MLA_HEADS * MLA_V), BF16),
                   jax.ShapeDtypeStruct((rows, MLA_KVR), F32),
                   jax.ShapeDtypeStruct((rows, LANE), F32)],
        compiler_params=_cparams(1),
        name="mla_prep",
    )(proj, c, sp, sm, ew["q_norm"], ew["kv_norm"], ew["w_uq"], ew["w_uk"], ew["w_uv"])


def _mla_ctx_kernel(ckv_ref, kpe_ref, wuk_ref, wuv_ref, k_ref, v_ref):
    ckv_b = ckv_ref[...].astype(BF16)
    kn = _dot(ckv_b, wuk_ref[...])
    kpe = kpe_ref[...]
    for h in range(MLA_HEADS):
        sl = slice(MLA_HP * h, MLA_HP * (h + 1))
        k_ref[:, sl] = (kn[:, sl] + kpe).astype(BF16)
    v_ref[...] = _dot(ckv_b, wuv_ref[...]).astype(BF16)


def _mla_ctx(ckv, kpe_pad, ew):
    rows = ckv.shape[0]
    full = lambda a: pl.BlockSpec(a.shape, lambda i: (0,) * a.ndim)
    return pl.pallas_call(
        _mla_ctx_kernel,
        grid=(rows // TM,),
        in_specs=[pl.BlockSpec((TM, MLA_KVR), lambda i: (i, 0)), pl.BlockSpec((TM, LANE), lambda i: (i, 0)),
                  full(ew["w_uk"]), full(ew["w_uv"])],
        out_specs=[pl.BlockSpec((TM, MLA_HEADS * MLA_HP), lambda i: (i, 0)),
                   pl.BlockSpec((TM, MLA_HEADS * MLA_V), lambda i: (i, 0))],
        out_shape=[jax.ShapeDtypeStruct((rows, MLA_HEADS * MLA_HP), BF16),
                   jax.ShapeDtypeStruct((rows, MLA_HEADS * MLA_V), BF16)],
        compiler_params=_cparams(1),
        name="mla_ctx",
    )(ckv, kpe_pad, ew["w_uk"], ew["w_uv"])


def _attn_kernel(*refs, n_heads, group, dqk, dv, scale, has_ctx, has_sink, window, tq, seq):
    refs = list(refs)
    sink_ref = refs.pop(0) if has_sink else None
    q_ref, k_ref, v_ref = refs[:3]
    kc_ref, vc_ref = (refs[3], refs[4]) if has_ctx else (None, None)
    o_ref = refs[-1]
    i = pl.program_id(1)
    if window:
        kw = tq + 2 * window
        start = pl.multiple_of(jnp.clip(i * tq - window, 0, seq - kw), LANE)
        qpos = i * tq + lax.broadcasted_iota(jnp.int32, (tq, kw), 0)
        kpos = start + lax.broadcasted_iota(jnp.int32, (tq, kw), 1)
        valid = jnp.abs(qpos - kpos) <= window
    for h in range(n_heads):
        hk = h // group
        q = q_ref[:, h * dqk:(h + 1) * dqk]
        if window:
            k = k_ref[pl.ds(start, kw), hk * dqk:(hk + 1) * dqk]
            v = v_ref[pl.ds(start, kw), hk * dv:(hk + 1) * dv]
            s = jnp.where(valid, _dot_nt(q, k) * scale, NEG_INF)
        else:
            k = k_ref[:, hk * dqk:(hk + 1) * dqk]
            v = v_ref[:, hk * dv:(hk + 1) * dv]
            s = _dot_nt(q, k) * scale
        m = jnp.max(s, axis=-1, keepdims=True)
        if has_ctx:
            sc = _dot_nt(q, kc_ref[:, hk * dqk:(hk + 1) * dqk]) * scale
            m = jnp.maximum(m, jnp.max(sc, axis=-1, keepdims=True))
        if has_sink:
            sk = sink_ref[h]
            m = jnp.maximum(m, sk)
        p = jnp.exp(s - m)
        l = jnp.sum(p, axis=-1, keepdims=True)
        o = _dot(p.astype(BF16), v)
        if has_ctx:
            pc = jnp.exp(sc - m)
            l = l + jnp.sum(pc, axis=-1, keepdims=True)
            o = o + _dot(pc.astype(BF16), vc_ref[:, hk * dv:(hk + 1) * dv])
        if has_sink:
            l = l + jnp.exp(sk - m)
        o_ref[:, h * dv:(h + 1) * dv] = (o / l).astype(o_ref.dtype)


def _attention(q, k, v, ctx, sink, *, n_batch, seq, n_heads, group, dqk, dv, scale, window):
    tq = TM
    nq = seq // tq
    n_kv = n_heads // group
    in_specs = []
    args = []
    if sink is not None:
        in_specs.append(pl.BlockSpec(memory_space=pltpu.SMEM))
        args.append(sink)
    in_specs += [pl.BlockSpec((tq, n_heads * dqk), lambda b, i: (b * nq + i, 0)),
                 pl.BlockSpec((seq, n_kv * dqk), lambda b, i: (b, 0)),
                 pl.BlockSpec((seq, n_kv * dv), lambda b, i: (b, 0))]
    args += [q, k, v]
    if ctx is not None:
        kc, vc = ctx
        sc = kc.shape[0] // n_batch
        in_specs += [pl.BlockSpec((sc, n_kv * dqk), lambda b, i: (b, 0)),
                     pl.BlockSpec((sc, n_kv * dv), lambda b, i: (b, 0))]
        args += [kc, vc]
    kern = functools.partial(_attn_kernel, n_heads=n_heads, group=group, dqk=dqk, dv=dv, scale=scale,
                             has_ctx=ctx is not None, has_sink=sink is not None, window=window, tq=tq, seq=seq)
    return pl.pallas_call(
        kern,
        grid=(n_batch, nq),
        in_specs=in_specs,
        out_specs=pl.BlockSpec((tq, n_heads * dv), lambda b, i: (b * nq + i, 0)),
        out_shape=jax.ShapeDtypeStruct((n_batch * seq, n_heads * dv), BF16),
        compiler_params=_cparams(2, 40 * 1024 * 1024),
        name="attention",
    )(*args)


def _ret_kernel(*refs, seq, has_init, out_state):
    refs = list(refs)
    df_ref, db_ref, rq_ref, rk_ref, rv_ref, rg_ref, gn_ref = refs[:7]
    pos = 7
    if has_init:
        s0f_ref, s0b_ref = refs[pos], refs[pos + 1]
        pos += 2
    o_ref = refs[pos]
    pos += 1
    if out_state:
        sf_ref, sb_ref = refs[pos], refs[pos + 1]
        pos += 2
    of_scr, ob_scr = refs[pos], refs[pos + 1]

    pair = pl.program_id(1)
    n_chunks = seq // RET_CHUNK
    idx_c = lax.broadcasted_iota(jnp.int32, (RET_CHUNK, 1), 0).astype(F32)
    diff = (lax.broadcasted_iota(jnp.int32, (RET_CHUNK, RET_CHUNK), 0)
            - lax.broadcasted_iota(jnp.int32, (RET_CHUNK, RET_CHUNK), 1)).astype(F32)
    for hh in range(2):
        h = 2 * pair + hh
        hs = slice(RET_HD * hh, RET_HD * (hh + 1))
        rows = [slice(RET_CHUNK * ci, RET_CHUNK * (ci + 1)) for ci in range(n_chunks)]
        qs = [rq_ref[r, hs].astype(BF16) for r in rows]
        ks = [rk_ref[r, hs] * (RET_HD ** -0.5) for r in rows]
        vs = [rv_ref[r, hs].astype(BF16) for r in rows]
        qk = [_dot_nt(q, k.astype(BF16)) for q, k in zip(qs, ks)]
        for fwd in (True, False):
            d = jnp.full((1, 1), (df_ref if fwd else db_ref)[h], F32)
            lg = jnp.minimum(d, 0.0) - jnp.log1p(jnp.exp(-jnp.abs(d)))
            dd = diff if fwd else -diff
            mask = jnp.where(dd >= 0, jnp.exp(lg * jnp.maximum(dd, 0.0)), 0.0)
            if fwd:
                q_dec = jnp.exp(lg * (idx_c + 1.0))
                k_dec = jnp.exp(lg * (RET_CHUNK - 1.0 - idx_c))
            else:
                q_dec = jnp.exp(lg * (RET_CHUNK - idx_c))
                k_dec = jnp.exp(lg * idx_c)
            c_dec = jnp.exp(lg * RET_CHUNK)
            scr = of_scr if fwd else ob_scr
            if has_init:
                state0 = (s0f_ref if fwd else s0b_ref)[0, hh]
            else:
                state0 = jnp.zeros((RET_HD, RET_HD), F32)

            state = state0
            for ci in (range(n_chunks) if fwd else reversed(range(n_chunks))):
                o = _dot((qk[ci] * mask).astype(BF16), vs[ci]) + _dot(qs[ci], state.astype(BF16)) * q_dec
                scr[rows[ci], hs] = o
                state = state * c_dec + _dot_tn((ks[ci] * k_dec).astype(BF16), vs[ci])
            if out_state:
                (sf_ref if fwd else sb_ref)[0, hh] = state

    def head_norm(x):
        mu = jnp.mean(x, axis=-1, keepdims=True)
        xc = x - mu
        return xc * lax.rsqrt(jnp.mean(xc * xc, axis=-1, keepdims=True) + LN_EPS)

    for hh in range(2):
        hs = slice(RET_HD * hh, RET_HD * (hh + 1))
        o = head_norm(of_scr[:, hs]) + head_norm(ob_scr[:, hs])
        o_ref[:, hs] = (o * gn_ref[:, hs] * jax.nn.silu(rg_ref[:, hs])).astype(o_ref.dtype)


def _retention(proj, dec_f, dec_b, gn, init, *, n_batch, seq, out_state):
    pairs = RET_HEADS // 2
    col0 = 512 // LANE

    def col_spec(k):
        return pl.BlockSpec((seq, LANE), lambda b, p: (b, col0 + k * pairs + p))

    smem = pl.BlockSpec(memory_space=pltpu.SMEM)
    st_spec = pl.BlockSpec((1, 2, RET_HD, RET_HD), lambda b, p: (b, p, 0, 0))
    in_specs = [smem, smem, col_spec(0), col_spec(1), col_spec(2), col_spec(3),
                pl.BlockSpec((1, LANE), lambda b, p: (0, p))]
    args = [dec_f, dec_b, proj, proj, proj, proj, gn]
    if init is not None:
        in_specs += [st_spec, st_spec]
        args += list(init)
    out_specs = [pl.BlockSpec((seq, LANE), lambda b, p: (b, p))]
    out_shape = [jax.ShapeDtypeStruct((n_batch * seq, RET_W), BF16)]
    if out_state:
        out_specs += [st_spec, st_spec]
        out_shape += [jax.ShapeDtypeStruct((n_batch, RET_HEADS, RET_HD, RET_HD), F32)] * 2
    kern = functools.partial(_ret_kernel, seq=seq, has_init=init is not None, out_state=out_state)
    return pl.pallas_call(
        kern,
        grid=(n_batch, pairs),
        in_specs=in_specs,
        out_specs=out_specs,
        out_shape=out_shape,
        scratch_shapes=[pltpu.VMEM((seq, LANE), F32), pltpu.VMEM((seq, LANE), F32)],
        compiler_params=_cparams(2),
        name="retention",
    )(*args)


def _swa_prep_kernel(p_ref, c_ref, sp_ref, sm_ref, q_ref, k_ref, v_ref):
    c, sp, sm = c_ref[...], sp_ref[...], sm_ref[...]
    for j in range(SWA_Q // LANE):
        sl = slice(LANE * j, LANE * (j + 1))
        q_ref[:, sl] = _rope(p_ref[:, sl], c, sp, sm, SWA_HD // 4).astype(BF16)
    for j in range(SWA_KV // LANE):
        sl = slice(LANE * j, LANE * (j + 1))
        k_ref[:, sl] = _rope(p_ref[:, SWA_Q + LANE * j:SWA_Q + LANE * (j + 1)], c, sp, sm, SWA_HD // 4).astype(BF16)
    v_ref[...] = p_ref[:, SWA_Q + SWA_KV:].astype(BF16)


def _swa_prep(proj, tabs, per_batch):
    rows = proj.shape[0]
    c, sp, sm = tabs

    def tab_idx(i):
        return (0 if per_batch is None else 1 + i % per_batch, 0)

    tab_spec = pl.BlockSpec((TM, LANE), tab_idx)
    return pl.pallas_call(
        _swa_prep_kernel,
        grid=(rows // TM,),
        in_specs=[pl.BlockSpec((TM, SWA_Q + 2 * SWA_KV), lambda i: (i, 0)), tab_spec, tab_spec, tab_spec],
        out_specs=[pl.BlockSpec((TM, SWA_Q), lambda i: (i, 0)),
                   pl.BlockSpec((TM, SWA_KV), lambda i: (i, 0)),
                   pl.BlockSpec((TM, SWA_KV), lambda i: (i, 0))],
        out_shape=[jax.ShapeDtypeStruct((rows, SWA_Q), BF16),
                   jax.ShapeDtypeStruct((rows, SWA_KV), BF16),
                   jax.ShapeDtypeStruct((rows, SWA_KV), BF16)],
        compiler_params=_cparams(1),
        name="swa_prep",
    )(proj, c, sp, sm)


def _to_row_tiles(ref, x):
    rows = x.shape[0]
    for s in range(D // LANE):
        ref[pl.ds(s, rows, stride=SUB), :] = x[:, LANE * s:LANE * (s + 1)]


def _from_row_tiles(ref, rows):
    return jnp.concatenate([ref[pl.ds(s, rows, stride=SUB), :] for s in range(D // LANE)], axis=1)


def _post_kernel(*refs, n_parts):
    a_refs = refs[:n_parts]
    w_ref, x_ref, m_ref, g_ref, b_ref, rh_ref, rl_ref, x1_ref, h2_ref, lg_ref = refs[n_parts:]
    out = None
    off = 0
    for a_ref in a_refs:
        kk = a_ref.shape[1]
        part = _dot(a_ref[...], w_ref[off:off + kk, :])
        out = part if out is None else out + part
        off += kk
    z = ALPHA * x_ref[...] + m_ref[0, 2:3, :] * out
    x1 = _layernorm(z, g_ref[...], b_ref[...])
    x1_ref[...] = x1
    h2 = x1 * (1.0 + m_ref[0, 4:5, :]) + m_ref[0, 3:4, :]
    _to_row_tiles(h2_ref, h2)
    hi = h2.astype(BF16)
    lo = (h2 - hi.astype(F32)).astype(BF16)
    rh = rh_ref[...]
    lg_ref[...] = _dot(hi, rh) + _dot(lo, rh) + _dot(hi, rl_ref[...])


def _post_mixer(parts, w_out, x, mod, ln_g, ln_b, r_hi, r_lo, per_batch):
    rows = x.shape[0]
    full = lambda a: pl.BlockSpec(a.shape, lambda i: (0,) * a.ndim)
    in_specs = [pl.BlockSpec((TM, a.shape[1]), lambda i: (i, 0)) for a in parts]
    in_specs += [full(w_out), pl.BlockSpec((TM, D), lambda i: (i, 0)),
                 pl.BlockSpec((1, 6, D), lambda i: (_mod_row(i, per_batch), 0, 0)),
                 full(ln_g), full(ln_b), full(r_hi), full(r_lo)]
    return pl.pallas_call(
        functools.partial(_post_kernel, n_parts=len(parts)),
        grid=(rows // TM,),
        in_specs=in_specs,
        out_specs=[pl.BlockSpec((TM, D), lambda i: (i, 0)),
                   pl.BlockSpec((TM * SUB, LANE), lambda i: (i, 0)),
                   pl.BlockSpec((TM, LANE), lambda i: (i, 0))],
        out_shape=[jax.ShapeDtypeStruct((rows, D), F32),
                   jax.ShapeDtypeStruct((rows * SUB, LANE), F32),
                   jax.ShapeDtypeStruct((rows, LANE), F32)],
        compiler_params=_cparams(1),
        name="post_mixer",
    )(*parts, w_out, x, mod, ln_g, ln_b, r_hi, r_lo)


def _route_kernel(lg_ref, bias_ref, idx_ref, rank_ref, w_ref, cnt_ref, carry_ref):
    i = pl.program_id(0)

    @pl.when(i == 0)
    def _():
        carry_ref[...] = jnp.zeros_like(carry_ref)

    t = lg_ref.shape[0]
    gsz = N_EXP // N_GROUPS
    scores = jax.nn.sigmoid(lg_ref[...].T[:N_EXP])
    sel = scores + bias_ref[...]
    g3 = sel.reshape(N_GROUPS, gsz, t)
    sub_iota = lax.broadcasted_iota(jnp.int32, g3.shape, 1)
    m1 = jnp.max(g3, axis=1)
    first = jnp.min(jnp.where(g3 == m1[:, None, :], sub_iota, gsz), axis=1)
    m2 = jnp.max(jnp.where(sub_iota == first[:, None, :], -jnp.inf, g3), axis=1)
    grp = m1 + m2
    g_iota = lax.broadcasted_iota(jnp.int32, grp.shape, 0)
    gmask = jnp.zeros(grp.shape, jnp.bool_)
    for _ in range(TOPK_GROUPS):
        gm = jnp.max(grp, axis=0, keepdims=True)
        gi = jnp.min(jnp.where(grp == gm, g_iota, N_GROUPS), axis=0, keepdims=True)
        hit = g_iota == gi
        gmask = jnp.logical_or(gmask, hit)
        grp = jnp.where(hit, -jnp.inf, grp)
    emask = jnp.broadcast_to(gmask[:, None, :], g3.shape).reshape(N_EXP, t)
    cur = jnp.where(emask, sel, NEG_INF)
    e_iota = lax.broadcasted_iota(jnp.int32, cur.shape, 0)
    hits = []
    member = jnp.zeros(cur.shape, F32)
    for _ in range(TOP_K):
        cm = jnp.max(cur, axis=0, keepdims=True)
        ci = jnp.min(jnp.where(cur == cm, e_iota, N_EXP), axis=0, keepdims=True)
        hit = e_iota == ci
        hits.append((hit, ci))
        member = member + hit.astype(F32)
        cur = jnp.where(hit, -jnp.inf, cur)
    tri = (lax.broadcasted_iota(jnp.int32, (t, t), 0) < lax.broadcasted_iota(jnp.int32, (t, t), 1)).astype(BF16)
    before = _dot(member.astype(BF16), tri) + carry_ref[:, 0:1]
    ws = [jnp.sum(jnp.where(hit, scores, 0.0), axis=0, keepdims=True) for hit, _ in hits]
    wsum = ws[0]
    for w in ws[1:]:
        wsum = wsum + w
    for k, (hit, ci) in enumerate(hits):
        idx_ref[k:k + 1, :] = ci
        rank_ref[k:k + 1, :] = jnp.sum(jnp.where(hit, before, 0.0), axis=0, keepdims=True).astype(jnp.int32)
        w_ref[k:k + 1, :] = ws[k] / wsum * ROUTED_SCALE
    total = carry_ref[...] + jnp.sum(member, axis=1, keepdims=True)
    carry_ref[...] = total
    cnt_ref[...] = total.astype(jnp.int32)


def _route(logits, bias_col):
    rows = logits.shape[0]
    row_spec = pl.BlockSpec((TOP_K, TM), lambda i: (0, i))
    return pl.pallas_call(
        _route_kernel,
        grid=(rows // TM,),
        in_specs=[pl.BlockSpec((TM, LANE), lambda i: (i, 0)), pl.BlockSpec((N_EXP, 1), lambda i: (0, 0))],
        out_specs=[row_spec, row_spec, row_spec, pl.BlockSpec((N_EXP, LANE), lambda i: (0, 0))],
        out_shape=[jax.ShapeDtypeStruct((TOP_K, rows), jnp.int32),
                   jax.ShapeDtypeStruct((TOP_K, rows), jnp.int32),
                   jax.ShapeDtypeStruct((TOP_K, rows), F32),
                   jax.ShapeDtypeStruct((N_EXP, LANE), jnp.int32)],
        scratch_shapes=[pltpu.VMEM((N_EXP, LANE), F32)],
        compiler_params=_cparams(1),
        name="route",
    )(logits, bias_col)


def _experts_kernel(exp_ref, nvalid_ref, asg_ref, w_ref,
                    x_ref, wg_ref, wu_ref, wd_ref, y_ref, xbuf, obuf, wg_b, wu_b, wd_b, *, n_tok, n_assign):
    i = pl.program_id(0)
    nvalid = nvalid_ref[i]
    base = i * EXP_TM

    @pl.when(i == 0)
    def _():
        y_ref[...] = jnp.zeros_like(y_ref)

    @pl.when(jnp.logical_or(i == 0, exp_ref[i] != exp_ref[jnp.maximum(i - 1, 0)]))
    def _():
        wg_b[...] = wg_ref[...].astype(BF16)
        wu_b[...] = wu_ref[...].astype(BF16)
        wd_b[...] = wd_ref[...].astype(BF16)

    @pl.when(nvalid > 0)
    def _():
        def gather(j, c):
            for u in range(SUB):
                r = j * SUB + u
                a = jnp.minimum(asg_ref[base + r], n_assign - 1)
                xbuf[pl.ds(pl.multiple_of(r * SUB, SUB), SUB), :] = x_ref[a & (n_tok - 1)]
            return c
        lax.fori_loop(0, EXP_TM // SUB, gather, 0)

        x = _from_row_tiles(xbuf, EXP_TM).astype(BF16)
        g = _dot(x, wg_b[...])
        u = _dot(x, wu_b[...])
        h = (jax.nn.silu(g) * u).astype(BF16)
        _to_row_tiles(obuf, _dot(h, wd_b[...]))

        def scatter(j, c):
            vals = []
            for u in range(SUB):
                r = j * SUB + u
                a = asg_ref[base + r]
                ac = jnp.minimum(a, n_assign - 1)
                t = jnp.where(a >= n_assign, n_tok, ac & (n_tok - 1))
                row = obuf[pl.ds(pl.multiple_of(r * SUB, SUB), SUB), :]
                vals.append((t, y_ref[t] + w_ref[ac] * row))
            for t, v in vals:
                y_ref[t] = v
            return c
        lax.fori_loop(0, (nvalid + SUB - 1) // SUB, scatter, 0)


def _experts(tile_exp, tile_nvalid, asg_sorted, w_flat, h2_tiles, wg, wu, wd, layer):
    n_tok = h2_tiles.shape[0]
    n_items = tile_exp.shape[0]
    wmap = lambda i, exp, *_: (layer, exp[i], 0, 0)
    gs = pltpu.PrefetchScalarGridSpec(
        num_scalar_prefetch=4,
        grid=(n_items,),
        in_specs=[pl.BlockSpec(memory_space=pltpu.VMEM),
                  pl.BlockSpec((None, None, D, EXP_D), wmap),
                  pl.BlockSpec((None, None, D, EXP_D), wmap),
                  pl.BlockSpec((None, None, EXP_D, D), wmap)],
        out_specs=pl.BlockSpec(memory_space=pltpu.VMEM),
        scratch_shapes=[pltpu.VMEM((EXP_TM * SUB, LANE), F32), pltpu.VMEM((EXP_TM * SUB, LANE), F32),
                        pltpu.VMEM((D, EXP_D), BF16), pltpu.VMEM((D, EXP_D), BF16), pltpu.VMEM((EXP_D, D), BF16)],
    )
    return pl.pallas_call(
        functools.partial(_experts_kernel, n_tok=n_tok, n_assign=w_flat.shape[0]),
        grid_spec=gs,
        out_shape=jax.ShapeDtypeStruct((n_tok + SUB, SUB, LANE), F32),
        compiler_params=_cparams(1, VMEM_BIG),
        name="experts",
    )(tile_exp, tile_nvalid, asg_sorted, w_flat, h2_tiles, wg, wu, wd)


def _invert_kernel(dest_ref, seg_ref, cnt_ref, asg_ref, *, unroll):
    n_assign = dest_ref.shape[0]

    def fill(e, c):
        def one(p, c2):
            asg_ref[p] = n_assign
            return c2
        return lax.fori_loop(seg_ref[e] + cnt_ref[e], seg_ref[e + 1], one, c)
    lax.fori_loop(0, N_EXP, fill, 0)

    def body(j, c):
        for u in range(unroll):
            a = j * unroll + u
            asg_ref[dest_ref[a]] = a
        return c
    lax.fori_loop(0, n_assign // unroll, body, 0)


def _invert(dest, seg, cnt, n_slots):
    smem = pl.BlockSpec(memory_space=pltpu.SMEM)
    return pl.pallas_call(
        functools.partial(_invert_kernel, unroll=16),
        in_specs=[smem, smem, smem],
        out_specs=smem,
        out_shape=jax.ShapeDtypeStruct((n_slots,), jnp.int32),
        name="invert",
    )(dest, seg, cnt)


def _dispatch_plan(idx, rank, cnt):
    n_tok = idx.shape[1]
    assert n_tok & (n_tok - 1) == 0
    n_tiles = n_tok * TOP_K // EXP_TM + N_EXP
    tiles_e = (cnt + EXP_TM - 1) // EXP_TM
    tile_end = jnp.cumsum(tiles_e)
    tile_start = tile_end - tiles_e
    seg = jnp.concatenate([jnp.zeros((1,), jnp.int32), tile_end * EXP_TM]).astype(jnp.int32)
    e_ids = jnp.arange(N_EXP, dtype=jnp.int32)
    dest = (jnp.sum(jnp.where(idx[:, :, None] == e_ids, seg[:N_EXP], 0), axis=-1) + rank).reshape(-1)
    asg_sorted = _invert(dest, seg, cnt, n_tiles * EXP_TM)
    ii = jnp.arange(n_tiles, dtype=jnp.int32)
    used = ii < tile_end[-1]
    tile_exp = jnp.sum(tile_end[None, :] <= jnp.minimum(ii, tile_end[-1] - 1)[:, None], axis=1).astype(jnp.int32)
    tile_exp = jnp.minimum(tile_exp, N_EXP - 1)
    left = cnt[tile_exp] - (ii - tile_start[tile_exp]) * EXP_TM
    tile_nvalid = jnp.where(used, jnp.clip(left, 0, EXP_TM), 0).astype(jnp.int32)
    return tile_exp, tile_nvalid, asg_sorted


def _final_kernel(y_ref, x1_ref, m_ref, sg_ref, su_ref, sd_ref, g_ref, b_ref, o_ref):
    x1 = x1_ref[...]
    routed = _from_row_tiles(y_ref, TM)
    h2 = (x1 * (1.0 + m_ref[0, 4:5, :]) + m_ref[0, 3:4, :]).astype(BF16)
    act = (jax.nn.silu(_dot(h2, sg_ref[...])) * _dot(h2, su_ref[...])).astype(BF16)
    shared = _dot(act, sd_ref[...])
    z = ALPHA * x1 + m_ref[0, 5:6, :] * (routed + shared)
    o_ref[...] = _layernorm(z, g_ref[...], b_ref[...])


def _final(y_tiles, x1, mod, sg, su, sd, ln_g, ln_b, per_batch):
    rows = x1.shape[0]
    full = lambda a: pl.BlockSpec(a.shape, lambda i: (0,) * a.ndim)
    return pl.pallas_call(
        _final_kernel,
        grid=(rows // TM,),
        in_specs=[pl.BlockSpec((TM * SUB, LANE), lambda i: (i, 0)),
                  pl.BlockSpec((TM, D), lambda i: (i, 0)),
                  pl.BlockSpec((1, 6, D), lambda i: (_mod_row(i, per_batch), 0, 0)),
                  full(sg), full(su), full(sd), full(ln_g), full(ln_b)],
        out_specs=pl.BlockSpec((TM, D), lambda i: (i, 0)),
        out_shape=jax.ShapeDtypeStruct((rows, D), F32),
        compiler_params=_cparams(1),
        name="final",
    )(y_tiles, x1, mod, sg, su, sd, ln_g, ln_b)


def _rope_tables(rot_dim, lane_lo, n_rot, n_rows):
    nf = rot_dim // 4
    t = jnp.arange(n_rows)
    r = (t // GRID_W).astype(F32)
    col = (t % GRID_W).astype(F32)
    freqs = ROPE_BASE ** (-jnp.arange(nf, dtype=F32) / nf)
    lane = jnp.arange(LANE)
    j = (lane - lane_lo) % rot_dim
    in_rot = jnp.logical_and(lane >= lane_lo, lane < lane_lo + n_rot * rot_dim)
    half = j // (2 * nf)
    second = (j % (2 * nf)) >= nf
    f = freqs[j % nf]
    ang = jnp.where(half[None, :] == 0, r[:, None], col[:, None]) * f[None, :]
    cos = jnp.where(in_rot[None, :], jnp.cos(ang), 1.0)
    sin = jnp.where(in_rot[None, :], jnp.sin(ang), 0.0)
    sp = jnp.where(second[None, :], sin, 0.0)
    sm = jnp.where(second[None, :], 0.0, -sin)
    ident = lambda v: jnp.full((TM, LANE), v, F32)
    return (jnp.concatenate([ident(1.0), cos], axis=0), jnp.concatenate([ident(0.0), sp], axis=0),
            jnp.concatenate([ident(0.0), sm], axis=0))


def _even_weights(w_in, q_norm, w_uq, kv_norm, w_uk, w_uv, w_out):
    z = lambda n: jnp.zeros((D, n), F32)
    mla_in = MLA_QR + MLA_KVR + MLA_ROPE
    w_in_p = jnp.concatenate([w_in[:, :MLA_QR + MLA_KVR], z(MLA_NOPE), w_in[:, MLA_QR + MLA_KVR:mla_in],
                              z(MLA_HP - MLA_NOPE - MLA_ROPE), w_in[:, mla_in:]], axis=1)
    uq = w_uq.reshape(MLA_QR, MLA_HEADS, MLA_NOPE + MLA_ROPE)
    uq = jnp.pad(uq, ((0, 0), (0, 0), (0, MLA_HP - MLA_NOPE - MLA_ROPE))).reshape(MLA_QR, MLA_HEADS * MLA_HP)
    uk = w_uk.reshape(MLA_KVR, MLA_HEADS, MLA_NOPE)
    uk = jnp.pad(uk, ((0, 0), (0, 0), (0, MLA_HP - MLA_NOPE))).reshape(MLA_KVR, MLA_HEADS * MLA_HP)
    return {"w_in": w_in_p.astype(BF16), "q_norm": q_norm.reshape(1, -1), "kv_norm": kv_norm.reshape(1, -1),
            "w_uq": uq.astype(BF16), "w_uk": uk.astype(BF16), "w_uv": w_uv.astype(BF16), "w_out": w_out.astype(BF16)}


def kernel(x_prompt, x_sample, cache_mla_ckv, cache_mla_kpe, state_ret_fwd, state_ret_bwd, cache_swa_k, cache_swa_v, c, c_ctx, w_mod, b_mod, ln1_g, ln1_b, ln2_g, ln2_b, mla_ret_w_in, mla_q_norm, mla_w_uq, mla_kv_norm, mla_w_uk, mla_w_uv, ret_decay_fwd, ret_decay_bwd, ret_gn_g, even_w_out, swa_w_in, swa_sink, swa_w_out, moe_router, moe_router_bias, moe_w_gate, moe_w_up, moe_w_down, shared_w_gate, shared_w_up, shared_w_down):
    n_p, seq_p, _ = x_prompt.shape
    n_s, seq_s, _ = x_sample.shape
    past = cache_mla_ckv.shape[2]
    groups = [
        dict(x=x_prompt.reshape(n_p * seq_p, D), nb=n_p, seq=seq_p, per_batch=None),
        dict(x=x_sample.reshape(n_s * seq_s, D), nb=n_s, seq=seq_s, per_batch=seq_s // TM),
    ]
    cond8 = jnp.zeros((8, D), F32).at[0].set(c_ctx).at[1:1 + n_s].set(c)
    mods = _modulation(cond8, w_mod, b_mod).reshape(DEPTH, 8, 6, D)
    tabs_mla = _rope_tables(MLA_ROPE, MLA_NOPE, 1, seq_s)
    tabs_swa = _rope_tables(SWA_HD, 0, LANE // SWA_HD, seq_s)

    outs = {k: [] for k in ("ckv", "kpe", "rf", "rb", "sk", "sv")}
    for l in range(DEPTH):
        mod = mods[l]
        r_pad = jnp.pad(moe_router[l], ((0, 0), (0, LANE - N_EXP)))
        r_hi = r_pad.astype(BF16)
        r_lo = (r_pad - r_hi.astype(F32)).astype(BF16)
        bias_col = moe_router_bias[l].reshape(N_EXP, 1)
        sg, su, sd = (shared_w_gate[l].astype(BF16), shared_w_up[l].astype(BF16), shared_w_down[l].astype(BF16))
        if l % 2 == 0:
            e = l // 2
            ew = _even_weights(mla_ret_w_in[e], mla_q_norm[e], mla_w_uq[e], mla_kv_norm[e], mla_w_uk[e],
                               mla_w_uv[e], even_w_out[e])
            kpe_ctx = jnp.pad(cache_mla_kpe[:, e].reshape(n_s * past, MLA_ROPE),
                              ((0, 0), (MLA_NOPE, MLA_HP - MLA_NOPE - MLA_ROPE)))
            ctx_kv = _mla_ctx(cache_mla_ckv[:, e].reshape(n_s * past, MLA_KVR), kpe_ctx, ew)
            gn = ret_gn_g[e].reshape(1, RET_W)
        else:
            o = l // 2
            w_in_o = swa_w_in[o].astype(BF16)
            w_out_o = swa_w_out[o].astype(BF16)
            ctx_swa = (cache_swa_k[:, o].reshape(n_s * past, SWA_KV).astype(BF16),
                       cache_swa_v[:, o].reshape(n_s * past, SWA_KV).astype(BF16))
        for gi, g in enumerate(groups):
            x, nb, seq, per_batch = g["x"], g["nb"], g["seq"], g["per_batch"]
            is_sample = gi == 1
            if l % 2 == 0:
                proj = _modmm(x, mod, ew["w_in"], per_batch)
                q, k, v, ckv, kpe = _mla_prep(proj, tabs_mla, ew, per_batch)
                o_mla = _attention(q, k, v, ctx_kv if is_sample else None, None, n_batch=nb, seq=seq,
                                   n_heads=MLA_HEADS, group=1, dqk=MLA_HP, dv=MLA_V,
                                   scale=(MLA_NOPE + MLA_ROPE) ** -0.5, window=0)
                init = (state_ret_fwd[:, e], state_ret_bwd[:, e]) if is_sample else None
                ret = _retention(proj, ret_decay_fwd[e], ret_decay_bwd[e], gn, init, n_batch=nb, seq=seq,
                                 out_state=not is_sample)
                if is_sample:
                    o_ret = ret[0]
                else:
                    o_ret, s_f, s_b = ret
                    outs["ckv"].append(ckv.reshape(nb, seq, MLA_KVR))
                    outs["kpe"].append(kpe[:, MLA_NOPE:MLA_NOPE + MLA_ROPE].reshape(nb, seq, MLA_ROPE))
                    outs["rf"].append(s_f)
                    outs["rb"].append(s_b)
                parts, w_out = [o_mla, o_ret], ew["w_out"]
            else:
                proj = _modmm(x, mod, w_in_o, per_batch)
                q, k, v = _swa_prep(proj, tabs_swa, per_batch)
                o_swa = _attention(q, k, v, ctx_swa if is_sample else None, swa_sink[o], n_batch=nb, seq=seq,
                                   n_heads=SWA_HEADS, group=SWA_HEADS // SWA_KVH, dqk=SWA_HD, dv=SWA_HD,
                                   scale=SWA_HD ** -0.5, window=WINDOW if is_sample else 0)
                if not is_sample:
                    outs["sk"].append(proj[:, SWA_Q:SWA_Q + SWA_KV].reshape(nb, seq, SWA_KVH, SWA_HD))
                    outs["sv"].append(proj[:, SWA_Q + SWA_KV:].reshape(nb, seq, SWA_KVH, SWA_HD))
                parts, w_out = [o_swa], w_out_o
            x1, h2_tiles, logits = _post_mixer(parts, w_out, x, mod, ln1_g[l].reshape(1, D), ln1_b[l].reshape(1, D),
                                               r_hi, r_lo, per_batch)
            idx, rank, w, cnt = _route(logits, bias_col)
            tile_exp, tile_nvalid, asg_sorted = _dispatch_plan(idx, rank, cnt[:, 0])
            y_tiles = _experts(tile_exp, tile_nvalid, asg_sorted, w.reshape(-1), h2_tiles.reshape(-1, SUB, LANE),
                               moe_w_gate, moe_w_up, moe_w_down, l)
            g["x"] = _final(y_tiles.reshape(-1, LANE), x1, mod, sg, su, sd,
                            ln2_g[l].reshape(1, D), ln2_b[l].reshape(1, D), per_batch)
    y_prompt = groups[0]["x"].reshape(n_p, seq_p, D)
    y_sample = groups[1]["x"].reshape(n_s, seq_s, D)
    return (y_prompt, y_sample, jnp.stack(outs["ckv"], axis=1), jnp.stack(outs["kpe"], axis=1),
            jnp.stack(outs["rf"], axis=1), jnp.stack(outs["rb"], axis=1),
            jnp.stack(outs["sk"], axis=1), jnp.stack(outs["sv"], axis=1))
```

```python
import functools

import jax
import jax.numpy as jnp
from jax import lax
from jax.experimental import pallas as pl
from jax.experimental.pallas import tpu as pltpu

F32 = jnp.float32
BF16 = jnp.bfloat16

D = 1024
DEPTH = 4
GRID_W = 64
ALPHA = (2.0 * DEPTH) ** 0.25
LN_EPS = 1e-5
RMS_EPS = 1e-6
ROPE_BASE = 10000.0
NEG_INF = -1e30
MLA_HEADS = 8
MLA_NOPE = 64
MLA_ROPE = 32
MLA_V = 64
MLA_QR = 256
MLA_KVR = 128
MLA_HP = 128
RET_HEADS = 8
RET_HD = 64
RET_CHUNK = 256
RET_W = RET_HEADS * RET_HD
EVEN_P = 512 + 4 * RET_W
SWA_HEADS = 16
SWA_KVH = 4
SWA_HD = 64
WINDOW = 128
SWA_Q = SWA_HEADS * SWA_HD
SWA_KV = SWA_KVH * SWA_HD
N_EXP = 64
TOP_K = 8
N_GROUPS = 8
TOPK_GROUPS = 4
EXP_D = 256
ROUTED_SCALE = 2.5

TM = 256
SUB = 8
LANE = 128
EXP_TM = 256
VMEM_BIG = 56 * 1024 * 1024


def _cparams(n_axes, vmem=None):
    return pltpu.CompilerParams(dimension_semantics=("arbitrary",) * n_axes, vmem_limit_bytes=vmem)


def _dot(a, b):
    return jnp.dot(a, b, preferred_element_type=F32)


def _dot_nt(a, b):
    return lax.dot_general(a, b, (((1,), (1,)), ((), ())), preferred_element_type=F32)


def _dot_tn(a, b):
    return lax.dot_general(a, b, (((0,), (0,)), ((), ())), preferred_element_type=F32)


def _layernorm(z, g, b):
    mu = jnp.mean(z, axis=-1, keepdims=True)
    zc = z - mu
    var = jnp.mean(zc * zc, axis=-1, keepdims=True)
    return zc * lax.rsqrt(var + LN_EPS) * g + b


def _rmsnorm(x, g):
    return x * lax.rsqrt(jnp.mean(x * x, axis=-1, keepdims=True) + RMS_EPS) * g


def _rope(x, c, sp, sm, shift):
    w = x.shape[-1]
    return x * c + pltpu.roll(x, shift, 1) * sp + pltpu.roll(x, w - shift, 1) * sm


def _mod_kernel(c_ref, w_ref, b_ref, o_ref):
    s = jax.nn.silu(c_ref[...]).astype(BF16)
    o_ref[0] = _dot(s, w_ref[0].astype(BF16)) + b_ref[0]


def _modulation(cond8, w_mod, b_mod):
    nt = 4
    tn = 6 * D // nt
    return pl.pallas_call(
        _mod_kernel,
        grid=(DEPTH, nt),
        in_specs=[pl.BlockSpec((8, D), lambda l, j: (0, 0)),
                  pl.BlockSpec((1, D, tn), lambda l, j: (l, 0, j)),
                  pl.BlockSpec((1, 1, tn), lambda l, j: (l, 0, j))],
        out_specs=pl.BlockSpec((1, 8, tn), lambda l, j: (l, 0, j)),
        out_shape=jax.ShapeDtypeStruct((DEPTH, 8, 6 * D), F32),
        compiler_params=_cparams(2, 40 * 1024 * 1024),
        name="modulation",
    )(cond8, w_mod, b_mod.reshape(DEPTH, 1, 6 * D))


def _mod_row(i, per_batch):
    return 0 if per_batch is None else 1 + i // per_batch


def _modmm_kernel(x_ref, m_ref, w_ref, o_ref, *, shift_i, scale_i):
    h = x_ref[...] * (1.0 + m_ref[0, scale_i:scale_i + 1, :]) + m_ref[0, shift_i:shift_i + 1, :]
    o_ref[...] = _dot(h.astype(BF16), w_ref[...])


def _modmm(x, mod, w, per_batch):
    rows = x.shape[0]
    n = w.shape[1]
    return pl.pallas_call(
        functools.partial(_modmm_kernel, shift_i=0, scale_i=1),
        grid=(rows // TM,),
        in_specs=[pl.BlockSpec((TM, D), lambda i: (i, 0)),
                  pl.BlockSpec((1, 6, D), lambda i: (_mod_row(i, per_batch), 0, 0)),
                  pl.BlockSpec((D, n), lambda i: (0, 0))],
        out_specs=pl.BlockSpec((TM, n), lambda i: (i, 0)),
        out_shape=jax.ShapeDtypeStruct((rows, n), F32),
        compiler_params=_cparams(1, 40 * 1024 * 1024),
        name="modmm",
    )(x, mod, w)


def _mla_prep_kernel(p_ref, c_ref, sp_ref, sm_ref, qn_ref, kvn_ref, wuq_ref, wuk_ref, wuv_ref,
                     q_ref, k_ref, v_ref, ckv_ref, kpe_ref):
    p = p_ref[...]
    c, sp, sm = c_ref[...], sp_ref[...], sm_ref[...]
    qn = _rmsnorm(p[:, 0:MLA_QR], qn_ref[...])
    q = _dot(qn.astype(BF16), wuq_ref[...])
    ckv = _rmsnorm(p[:, MLA_QR:MLA_QR + MLA_KVR], kvn_ref[...])
    ckv_ref[...] = ckv
    kpe = _rope(p[:, MLA_QR + MLA_KVR:512], c, sp, sm, MLA_ROPE // 4)
    kpe_ref[...] = kpe
    ckv_b = ckv.astype(BF16)
    kn = _dot(ckv_b, wuk_ref[...])
    for h in range(MLA_HEADS):
        sl = slice(MLA_HP * h, MLA_HP * (h + 1))
        q_ref[:, sl] = _rope(q[:, sl], c, sp, sm, MLA_ROPE // 4).astype(BF16)
        k_ref[:, sl] = (kn[:, sl] + kpe).astype(BF16)
    v_ref[...] = _dot(ckv_b, wuv_ref[...]).astype(BF16)


def _mla_prep(proj, tabs, ew, per_batch):
    rows = proj.shape[0]
    c, sp, sm = tabs

    def tab_idx(i):
        return (0 if per_batch is None else 1 + i % per_batch, 0)

    tab_spec = pl.BlockSpec((TM, LANE), tab_idx)
    full = lambda a: pl.BlockSpec(a.shape, lambda i: (0,) * a.ndim)
    return pl.pallas_call(
        _mla_prep_kernel,
        grid=(rows // TM,),
        in_specs=[pl.BlockSpec((TM, 512), lambda i: (i, 0)), tab_spec, tab_spec, tab_spec,
                  full(ew["q_norm"]), full(ew["kv_norm"]), full(ew["w_uq"]), full(ew["w_uk"]), full(ew["w_uv"])],
        out_specs=[pl.BlockSpec((TM, MLA_HEADS * MLA_HP), lambda i: (i, 0)),
                   pl.BlockSpec((TM, MLA_HEADS * MLA_HP), lambda i: (i, 0)),
                   pl.BlockSpec((TM, MLA_HEADS * MLA_V), lambda i: (i, 0)),
                   pl.BlockSpec((TM, MLA_KVR), lambda i: (i, 0)),
                   pl.BlockSpec((TM, LANE), lambda i: (i, 0))],
        out_shape=[jax.ShapeDtypeStruct((rows, MLA_HEADS * MLA_HP), BF16),
                   jax.ShapeDtypeStruct((rows, MLA_HEADS * MLA_HP), BF16),
                   jax.ShapeDtypeStruct((rows, MLA_HEADS * MLA_V), BF16),
                   jax.ShapeDtypeStruct((rows, MLA_KVR), F32),
                   jax.ShapeDtypeStruct((rows, LANE), F32)],
        compiler_params=_cparams(1),
        name="mla_prep",
    )(proj, c, sp, sm, ew["q_norm"], ew["kv_norm"], ew["w_uq"], ew["w_uk"], ew["w_uv"])


def _mla_ctx_kernel(ckv_ref, kpe_ref, wuk_ref, wuv_ref, k_ref, v_ref):
    ckv_b = ckv_ref[...].astype(BF16)
    kn = _dot(ckv_b, wuk_ref[...])
    kpe = kpe_ref[...]
    for h in range(MLA_HEADS):
        sl = slice(MLA_HP * h, MLA_HP * (h + 1))
        k_ref[:, sl] = (kn[:, sl] + kpe).astype(BF16)
    v_ref[...] = _dot(ckv_b, wuv_ref[...]).astype(BF16)


def _mla_ctx(ckv, kpe_pad, ew):
    rows = ckv.shape[0]
    full = lambda a: pl.BlockSpec(a.shape, lambda i: (0,) * a.ndim)
    return pl.pallas_call(
        _mla_ctx_kernel,
        grid=(rows // TM,),
        in_specs=[pl.BlockSpec((TM, MLA_KVR), lambda i: (i, 0)), pl.BlockSpec((TM, LANE), lambda i: (i, 0)),
                  full(ew["w_uk"]), full(ew["w_uv"])],
        out_specs=[pl.BlockSpec((TM, MLA_HEADS * MLA_HP), lambda i: (i, 0)),
                   pl.BlockSpec((TM, MLA_HEADS * MLA_V), lambda i: (i, 0))],
        out_shape=[jax.ShapeDtypeStruct((rows, MLA_HEADS * MLA_HP), BF16),
                   jax.ShapeDtypeStruct((rows, MLA_HEADS * MLA_V), BF16)],
        compiler_params=_cparams(1),
        name="mla_ctx",
    )(ckv, kpe_pad, ew["w_uk"], ew["w_uv"])


def _attn_kernel(*refs, n_heads, group, dqk, dv, scale, has_ctx, has_sink, window, tq, seq):
    refs = list(refs)
    sink_ref = refs.pop(0) if has_sink else None
    q_ref, k_ref, v_ref = refs[:3]
    kc_ref, vc_ref = (refs[3], refs[4]) if has_ctx else (None, None)
    o_ref = refs[-1]
    i = pl.program_id(1)
    if window:
        kw = tq + 2 * window
        start = pl.multiple_of(jnp.clip(i * tq - window, 0, seq - kw), LANE)
        qpos = i * tq + lax.broadcasted_iota(jnp.int32, (tq, kw), 0)
        kpos = start + lax.broadcasted_iota(jnp.int32, (tq, kw), 1)
        valid = jnp.abs(qpos - kpos) <= window
    for h in range(n_heads):
        hk = h // group
        q = q_ref[:, h * dqk:(h + 1) * dqk]
        if window:
            k = k_ref[pl.ds(start, kw), hk * dqk:(hk + 1) * dqk]
            v = v_ref[pl.ds(start, kw), hk * dv:(hk + 1) * dv]
            s = jnp.where(valid, _dot_nt(q, k) * scale, NEG_INF)
        else:
            k = k_ref[:, hk * dqk:(hk + 1) * dqk]
            v = v_ref[:, hk * dv:(hk + 1) * dv]
            s = _dot_nt(q, k) * scale
        m = jnp.max(s, axis=-1, keepdims=True)
        if has_ctx:
            sc = _dot_nt(q, kc_ref[:, hk * dqk:(hk + 1) * dqk]) * scale
            m = jnp.maximum(m, jnp.max(sc, axis=-1, keepdims=True))
        if has_sink:
            sk = sink_ref[h]
            m = jnp.maximum(m, sk)
        p = jnp.exp(s - m)
        l = jnp.sum(p, axis=-1, keepdims=True)
        o = _dot(p.astype(BF16), v)
        if has_ctx:
            pc = jnp.exp(sc - m)
            l = l + jnp.sum(pc, axis=-1, keepdims=True)
            o = o + _dot(pc.astype(BF16), vc_ref[:, hk * dv:(hk + 1) * dv])
        if has_sink:
            l = l + jnp.exp(sk - m)
        o_ref[:, h * dv:(h + 1) * dv] = (o / l).astype(o_ref.dtype)


def _attention(q, k, v, ctx, sink, *, n_batch, seq, n_heads, group, dqk, dv, scale, window):
    tq = TM
    nq = seq // tq
    n_kv = n_heads // group
    in_specs = []
    args = []
    if sink is not None:
        in_specs.append(pl.BlockSpec(memory_space=pltpu.SMEM))
        args.append(sink)
    in_specs += [pl.BlockSpec((tq, n_heads * dqk), lambda b, i: (b * nq + i, 0)),
                 pl.BlockSpec((seq, n_kv * dqk), lambda b, i: (b, 0)),
                 pl.BlockSpec((seq, n_kv * dv), lambda b, i: (b, 0))]
    args += [q, k, v]
    if ctx is not None:
        kc, vc = ctx
        sc = kc.shape[0] // n_batch
        in_specs += [pl.BlockSpec((sc, n_kv * dqk), lambda b, i: (b, 0)),
                     pl.BlockSpec((sc, n_kv * dv), lambda b, i: (b, 0))]
        args += [kc, vc]
    kern = functools.partial(_attn_kernel, n_heads=n_heads, group=group, dqk=dqk, dv=dv, scale=scale,
                             has_ctx=ctx is not None, has_sink=sink is not None, window=window, tq=tq, seq=seq)
    return pl.pallas_call(
        kern,
        grid=(n_batch, nq),
        in_specs=in_specs,
        out_specs=pl.BlockSpec((tq, n_heads * dv), lambda b, i: (b * nq + i, 0)),
        out_shape=jax.ShapeDtypeStruct((n_batch * seq, n_heads * dv), BF16),
        compiler_params=_cparams(2, 40 * 1024 * 1024),
        name="attention",
    )(*args)


def _ret_kernel(*refs, seq, has_init, out_state):
    refs = list(refs)
    df_ref, db_ref, rq_ref, rk_ref, rv_ref, rg_ref, gn_ref = refs[:7]
    pos = 7
    if has_init:
        s0f_ref, s0b_ref = refs[pos], refs[pos + 1]
        pos += 2
    o_ref = refs[pos]
    pos += 1
    if out_state:
        sf_ref, sb_ref = refs[pos], refs[pos + 1]
        pos += 2
    of_scr, ob_scr = refs[pos], refs[pos + 1]

    pair = pl.program_id(1)
    n_chunks = seq // RET_CHUNK
    idx_c = lax.broadcasted_iota(jnp.int32, (RET_CHUNK, 1), 0).astype(F32)
    diff = (lax.broadcasted_iota(jnp.int32, (RET_CHUNK, RET_CHUNK), 0)
            - lax.broadcasted_iota(jnp.int32, (RET_CHUNK, RET_CHUNK), 1)).astype(F32)
    for hh in range(2):
        h = 2 * pair + hh
        hs = slice(RET_HD * hh, RET_HD * (hh + 1))
        rows = [slice(RET_CHUNK * ci, RET_CHUNK * (ci + 1)) for ci in range(n_chunks)]
        qs = [rq_ref[r, hs].astype(BF16) for r in rows]
        ks = [rk_ref[r, hs] * (RET_HD ** -0.5) for r in rows]
        vs = [rv_ref[r, hs].astype(BF16) for r in rows]
        qk = [_dot_nt(q, k.astype(BF16)) for q, k in zip(qs, ks)]
        for fwd in (True, False):
            d = jnp.full((1, 1), (df_ref if fwd else db_ref)[h], F32)
            lg = jnp.minimum(d, 0.0) - jnp.log1p(jnp.exp(-jnp.abs(d)))
            dd = diff if fwd else -diff
            mask = jnp.where(dd >= 0, jnp.exp(lg * jnp.maximum(dd, 0.0)), 0.0)
            if fwd:
                q_dec = jnp.exp(lg * (idx_c + 1.0))
                k_dec = jnp.exp(lg * (RET_CHUNK - 1.0 - idx_c))
            else:
                q_dec = jnp.exp(lg * (RET_CHUNK - idx_c))
                k_dec = jnp.exp(lg * idx_c)
            c_dec = jnp.exp(lg * RET_CHUNK)
            scr = of_scr if fwd else ob_scr
            if has_init:
                state0 = (s0f_ref if fwd else s0b_ref)[0, hh]
            else:
                state0 = jnp.zeros((RET_HD, RET_HD), F32)

            state = state0
            for ci in (range(n_chunks) if fwd else reversed(range(n_chunks))):
                o = _dot((qk[ci] * mask).astype(BF16), vs[ci]) + _dot(qs[ci], state.astype(BF16)) * q_dec
                scr[rows[ci], hs] = o
                state = state * c_dec + _dot_tn((ks[ci] * k_dec).astype(BF16), vs[ci])
            if out_state:
                (sf_ref if fwd else sb_ref)[0, hh] = state

    def head_norm(x):
        mu = jnp.mean(x, axis=-1, keepdims=True)
        xc = x - mu
        return xc * lax.rsqrt(jnp.mean(xc * xc, axis=-1, keepdims=True) + LN_EPS)

    for hh in range(2):
        hs = slice(RET_HD * hh, RET_HD * (hh + 1))
        o = head_norm(of_scr[:, hs]) + head_norm(ob_scr[:, hs])
        o_ref[:, hs] = (o * gn_ref[:, hs] * jax.nn.silu(rg_ref[:, hs])).astype(o_ref.dtype)


def _retention(proj, dec_f, dec_b, gn, init, *, n_batch, seq, out_state):
    pairs = RET_HEADS // 2
    col0 = 512 // LANE

    def col_spec(k):
        return pl.BlockSpec((seq, LANE), lambda b, p: (b, col0 + k * pairs + p))

    smem = pl.BlockSpec(memory_space=pltpu.SMEM)
    st_spec = pl.BlockSpec((1, 2, RET_HD, RET_HD), lambda b, p: (b, p, 0, 0))
    in_specs = [smem, smem, col_spec(0), col_spec(1), col_spec(2), col_spec(3),
                pl.BlockSpec((1, LANE), lambda b, p: (0, p))]
    args = [dec_f, dec_b, proj, proj, proj, proj, gn]
    if init is not None:
        in_specs += [st_spec, st_spec]
        args += list(init)
    out_specs = [pl.BlockSpec((seq, LANE), lambda b, p: (b, p))]
    out_shape = [jax.ShapeDtypeStruct((n_batch * seq, RET_W), BF16)]
    if out_state:
        out_specs += [st_spec, st_spec]
        out_shape += [jax.ShapeDtypeStruct((n_batch, RET_HEADS, RET_HD, RET_HD), F32)] * 2
    kern = functools.partial(_ret_kernel, seq=seq, has_init=init is not None, out_state=out_state)
    return pl.pallas_call(
        kern,
        grid=(n_batch, pairs),
        in_specs=in_specs,
        out_specs=out_specs,
        out_shape=out_shape,
        scratch_shapes=[pltpu.VMEM((seq, LANE), F32), pltpu.VMEM((seq, LANE), F32)],
        compiler_params=_cparams(2),
        name="retention",
    )(*args)


def _swa_prep_kernel(p_ref, c_ref, sp_ref, sm_ref, q_ref, k_ref, v_ref):
    c, sp, sm = c_ref[...], sp_ref[...], sm_ref[...]
    for j in range(SWA_Q // LANE):
        sl = slice(LANE * j, LANE * (j + 1))
        q_ref[:, sl] = _rope(p_ref[:, sl], c, sp, sm, SWA_HD // 4).astype(BF16)
    for j in range(SWA_KV // LANE):
        sl = slice(LANE * j, LANE * (j + 1))
        k_ref[:, sl] = _rope(p_ref[:, SWA_Q + LANE * j:SWA_Q + LANE * (j + 1)], c, sp, sm, SWA_HD // 4).astype(BF16)
    v_ref[...] = p_ref[:, SWA_Q + SWA_KV:].astype(BF16)


def _swa_prep(proj, tabs, per_batch):
    rows = proj.shape[0]
    c, sp, sm = tabs

    def tab_idx(i):
        return (0 if per_batch is None else 1 + i % per_batch, 0)

    tab_spec = pl.BlockSpec((TM, LANE), tab_idx)
    return pl.pallas_call(
        _swa_prep_kernel,
        grid=(rows // TM,),
        in_specs=[pl.BlockSpec((TM, SWA_Q + 2 * SWA_KV), lambda i: (i, 0)), tab_spec, tab_spec, tab_spec],
        out_specs=[pl.BlockSpec((TM, SWA_Q), lambda i: (i, 0)),
                   pl.BlockSpec((TM, SWA_KV), lambda i: (i, 0)),
                   pl.BlockSpec((TM, SWA_KV), lambda i: (i, 0))],
        out_shape=[jax.ShapeDtypeStruct((rows, SWA_Q), BF16),
                   jax.ShapeDtypeStruct((rows, SWA_KV), BF16),
                   jax.ShapeDtypeStruct((rows, SWA_KV), BF16)],
        compiler_params=_cparams(1),
        name="swa_prep",
    )(proj, c, sp, sm)


def _to_row_tiles(ref, x):
    rows = x.shape[0]
    for s in range(D // LANE):
        ref[pl.ds(s, rows, stride=SUB), :] = x[:, LANE * s:LANE * (s + 1)]


def _from_row_tiles(ref, rows):
    return jnp.concatenate([ref[pl.ds(s, rows, stride=SUB), :] for s in range(D // LANE)], axis=1)


def _post_kernel(*refs, n_parts):
    a_refs = refs[:n_parts]
    w_ref, x_ref, m_ref, g_ref, b_ref, rh_ref, rl_ref, x1_ref, h2_ref, lg_ref = refs[n_parts:]
    out = None
    off = 0
    for a_ref in a_refs:
        kk = a_ref.shape[1]
        part = _dot(a_ref[...], w_ref[off:off + kk, :])
        out = part if out is None else out + part
        off += kk
    z = ALPHA * x_ref[...] + m_ref[0, 2:3, :] * out
    x1 = _layernorm(z, g_ref[...], b_ref[...])
    x1_ref[...] = x1
    h2 = x1 * (1.0 + m_ref[0, 4:5, :]) + m_ref[0, 3:4, :]
    _to_row_tiles(h2_ref, h2)
    hi = h2.astype(BF16)
    lo = (h2 - hi.astype(F32)).astype(BF16)
    rh = rh_ref[...]
    lg_ref[...] = _dot(hi, rh) + _dot(lo, rh) + _dot(hi, rl_ref[...])


def _post_mixer(parts, w_out, x, mod, ln_g, ln_b, r_hi, r_lo, per_batch):
    rows = x.shape[0]
    full = lambda a: pl.BlockSpec(a.shape, lambda i: (0,) * a.ndim)
    in_specs = [pl.BlockSpec((TM, a.shape[1]), lambda i: (i, 0)) for a in parts]
    in_specs += [full(w_out), pl.BlockSpec((TM, D), lambda i: (i, 0)),
                 pl.BlockSpec((1, 6, D), lambda i: (_mod_row(i, per_batch), 0, 0)),
                 full(ln_g), full(ln_b), full(r_hi), full(r_lo)]
    return pl.pallas_call(
        functools.partial(_post_kernel, n_parts=len(parts)),
        grid=(rows // TM,),
        in_specs=in_specs,
        out_specs=[pl.BlockSpec((TM, D), lambda i: (i, 0)),
                   pl.BlockSpec((TM * SUB, LANE), lambda i: (i, 0)),
                   pl.BlockSpec((TM, LANE), lambda i: (i, 0))],
        out_shape=[jax.ShapeDtypeStruct((rows, D), F32),
                   jax.ShapeDtypeStruct((rows * SUB, LANE), F32),
                   jax.ShapeDtypeStruct((rows, LANE), F32)],
        compiler_params=_cparams(1),
        name="post_mixer",
    )(*parts, w_out, x, mod, ln_g, ln_b, r_hi, r_lo)


def _route_kernel(lg_ref, bias_ref, idx_ref, rank_ref, w_ref, cnt_ref, carry_ref):
    i = pl.program_id(0)

    @pl.when(i == 0)
    def _():
        carry_ref[...] = jnp.zeros_like(carry_ref)

    t = lg_ref.shape[0]
    gsz = N_EXP // N_GROUPS
    scores = jax.nn.sigmoid(lg_ref[...].T[:N_EXP])
    sel = scores + bias_ref[...]
    g3 = sel.reshape(N_GROUPS, gsz, t)
    sub_iota = lax.broadcasted_iota(jnp.int32, g3.shape, 1)
    m1 = jnp.max(g3, axis=1)
    first = jnp.min(jnp.where(g3 == m1[:, None, :], sub_iota, gsz), axis=1)
    m2 = jnp.max(jnp.where(sub_iota == first[:, None, :], -jnp.inf, g3), axis=1)
    grp = m1 + m2
    g_iota = lax.broadcasted_iota(jnp.int32, grp.shape, 0)
    gmask = jnp.zeros(grp.shape, jnp.bool_)
    for _ in range(TOPK_GROUPS):
        gm = jnp.max(grp, axis=0, keepdims=True)
        gi = jnp.min(jnp.where(grp == gm, g_iota, N_GROUPS), axis=0, keepdims=True)
        hit = g_iota == gi
        gmask = jnp.logical_or(gmask, hit)
        grp = jnp.where(hit, -jnp.inf, grp)
    emask = jnp.broadcast_to(gmask[:, None, :], g3.shape).reshape(N_EXP, t)
    cur = jnp.where(emask, sel, NEG_INF)
    e_iota = lax.broadcasted_iota(jnp.int32, cur.shape, 0)
    hits = []
    member = jnp.zeros(cur.shape, F32)
    for _ in range(TOP_K):
        cm = jnp.max(cur, axis=0, keepdims=True)
        ci = jnp.min(jnp.where(cur == cm, e_iota, N_EXP), axis=0, keepdims=True)
        hit = e_iota == ci
        hits.append((hit, ci))
        member = member + hit.astype(F32)
        cur = jnp.where(hit, -jnp.inf, cur)
    tri = (lax.broadcasted_iota(jnp.int32, (t, t), 0) < lax.broadcasted_iota(jnp.int32, (t, t), 1)).astype(BF16)
    before = _dot(member.astype(BF16), tri) + carry_ref[:, 0:1]
    ws = [jnp.sum(jnp.where(hit, scores, 0.0), axis=0, keepdims=True) for hit, _ in hits]
    wsum = ws[0]
    for w in ws[1:]:
        wsum = wsum + w
    for k, (hit, ci) in enumerate(hits):
        idx_ref[k:k + 1, :] = ci
        rank_ref[k:k + 1, :] = jnp.sum(jnp.where(hit, before, 0.0), axis=0, keepdims=True).astype(jnp.int32)
        w_ref[k:k + 1, :] = ws[k] / wsum * ROUTED_SCALE
    total = carry_ref[...] + jnp.sum(member, axis=1, keepdims=True)
    carry_ref[...] = total
    cnt_ref[...] = total.astype(jnp.int32)


def _route(logits, bias_col):
    rows = logits.shape[0]
    row_spec = pl.BlockSpec((TOP_K, TM), lambda i: (0, i))
    return pl.pallas_call(
        _route_kernel,
        grid=(rows // TM,),
        in_specs=[pl.BlockSpec((TM, LANE), lambda i: (i, 0)), pl.BlockSpec((N_EXP, 1), lambda i: (0, 0))],
        out_specs=[row_spec, row_spec, row_spec, pl.BlockSpec((N_EXP, LANE), lambda i: (0, 0))],
        out_shape=[jax.ShapeDtypeStruct((TOP_K, rows), jnp.int32),
                   jax.ShapeDtypeStruct((TOP_K, rows), jnp.int32),
                   jax.ShapeDtypeStruct((TOP_K, rows), F32),
                   jax.ShapeDtypeStruct((N_EXP, LANE), jnp.int32)],
        scratch_shapes=[pltpu.VMEM((N_EXP, LANE), F32)],
        compiler_params=_cparams(1),
        name="route",
    )(logits, bias_col)


EXP_LEAD = 2 * EXP_TM


def _experts_kernel(exp_ref, used_ref, dest_ref, seg_ref, cnt_ref, w_ref,
                    x_ref, wg_ref, wu_ref, wd_ref, y_ref,
                    xbuf0, xbuf1, obuf0, obuf1, wg_b, wu_b, wd_b, asg, *, n_tok):
    s = pl.program_id(0)
    used = used_ref[0]
    n_assign = dest_ref.shape[0]
    pad_id = n_assign

    @pl.when(s == 0)
    def _():
        y_ref[...] = jnp.zeros_like(y_ref)
        for buf in (xbuf0, xbuf1, obuf0, obuf1):
            buf[...] = jnp.zeros_like(buf)

        def lead(j, c):
            for u in range(SUB):
                asg[j * SUB + u] = pad_id
            return c
        lax.fori_loop(0, EXP_LEAD // SUB, lead, 0)

        def fill(e, c):
            def one(p, c2):
                asg[p] = pad_id
                return c2
            return lax.fori_loop(seg_ref[e] + cnt_ref[e], seg_ref[e + 1], one, c)
        lax.fori_loop(0, N_EXP, fill, 0)

        def invert(j, c):
            for u in range(16):
                a = j * 16 + u
                asg[dest_ref[a]] = a
            return c
        lax.fori_loop(0, n_assign // 16, invert, 0)

    mm_tile = jnp.clip(s - 1, 0, used - 1)

    @pl.when(jnp.logical_or(s == 0, exp_ref[mm_tile] != exp_ref[jnp.maximum(mm_tile - 1, 0)]))
    def _():
        wg_b[...] = wg_ref[...].astype(BF16)
        wu_b[...] = wu_ref[...].astype(BF16)
        wd_b[...] = wd_ref[...].astype(BF16)

    def stages(x_gather, x_mm, o_mm, o_scatter):
        g_base = EXP_LEAD + jnp.minimum(s, used - 1) * EXP_TM
        for r in range(EXP_TM):
            t = jnp.minimum(asg[g_base + r] >> 3, n_tok - 1)
            x_gather[r * SUB:(r + 1) * SUB, :] = x_ref[t]

        x = _from_row_tiles(x_mm, EXP_TM).astype(BF16)
        h = (jax.nn.silu(_dot(x, wg_b[...])) * _dot(x, wu_b[...])).astype(BF16)
        _to_row_tiles(o_mm, _dot(h, wd_b[...]))

        s_base = s * EXP_TM
        for j in range(EXP_TM // SUB):
            vals = []
            for u in range(SUB):
                r = j * SUB + u
                a = asg[s_base + r]
                t = a >> 3
                vals.append((t, y_ref[t] + w_ref[a] * o_scatter[r * SUB:(r + 1) * SUB, :]))
            for t, v in vals:
                y_ref[t] = v

    @pl.when(s < used + 2)
    def _():
        @pl.when(s % 2 == 0)
        def _():
            stages(xbuf0, xbuf1, obuf1, obuf0)

        @pl.when(s % 2 == 1)
        def _():
            stages(xbuf1, xbuf0, obuf0, obuf1)


def _experts(tile_exp, used, dest, seg, cnt, w_flat, h2_tiles, wg, wu, wd, layer):
    n_tok = h2_tiles.shape[0]
    assert TOP_K == 8 and dest.shape[0] == n_tok * TOP_K and w_flat.shape[0] == dest.shape[0] + SUB
    n_tiles = tile_exp.shape[0]
    wmap = lambda s, exp, used, *_: (layer, exp[jnp.clip(s - 1, 0, used[0] - 1)], 0, 0)
    buf = pltpu.VMEM((EXP_TM * SUB, LANE), F32)
    gs = pltpu.PrefetchScalarGridSpec(
        num_scalar_prefetch=6,
        grid=(n_tiles + 2,),
        in_specs=[pl.BlockSpec(memory_space=pltpu.VMEM),
                  pl.BlockSpec((None, None, D, EXP_D), wmap),
                  pl.BlockSpec((None, None, D, EXP_D), wmap),
                  pl.BlockSpec((None, None, EXP_D, D), wmap)],
        out_specs=pl.BlockSpec(memory_space=pltpu.VMEM),
        scratch_shapes=[buf, buf, buf, buf,
                        pltpu.VMEM((D, EXP_D), BF16), pltpu.VMEM((D, EXP_D), BF16), pltpu.VMEM((EXP_D, D), BF16),
                        pltpu.SMEM((EXP_LEAD + n_tiles * EXP_TM,), jnp.int32)],
    )
    return pl.pallas_call(
        functools.partial(_experts_kernel, n_tok=n_tok),
        grid_spec=gs,
        out_shape=jax.ShapeDtypeStruct((n_tok + SUB, SUB, LANE), F32),
        compiler_params=_cparams(1, VMEM_BIG),
        name="experts",
    )(tile_exp, used, dest, seg, cnt, w_flat, h2_tiles, wg, wu, wd)


def _dispatch_plan(idx, rank, w, cnt):
    n_tok = idx.shape[1]
    n_tiles = n_tok * TOP_K // EXP_TM + N_EXP
    tiles_e = (cnt + EXP_TM - 1) // EXP_TM
    tile_end = jnp.cumsum(tiles_e)
    seg = (EXP_LEAD + jnp.concatenate([jnp.zeros((1,), jnp.int32), tile_end * EXP_TM])).astype(jnp.int32)
    e_ids = jnp.arange(N_EXP, dtype=jnp.int32)
    dest = jnp.sum(jnp.where(idx[:, :, None] == e_ids, seg[:N_EXP], 0), axis=-1) + rank
    ii = jnp.arange(n_tiles, dtype=jnp.int32)
    used = tile_end[-1:].astype(jnp.int32)
    tile_exp = jnp.sum(tile_end[None, :] <= jnp.minimum(ii, used - 1)[:, None], axis=1).astype(jnp.int32)
    tile_exp = jnp.minimum(tile_exp, N_EXP - 1)
    w_flat = jnp.concatenate([w.T.reshape(-1), jnp.zeros((SUB,), F32)])
    return tile_exp, used, dest.T.reshape(-1), seg, w_flat


def _final_kernel(y_ref, x1_ref, m_ref, sg_ref, su_ref, sd_ref, g_ref, b_ref, o_ref):
    x1 = x1_ref[...]
    routed = _from_row_tiles(y_ref, TM)
    h2 = (x1 * (1.0 + m_ref[0, 4:5, :]) + m_ref[0, 3:4, :]).astype(BF16)
    act = (jax.nn.silu(_dot(h2, sg_ref[...])) * _dot(h2, su_ref[...])).astype(BF16)
    shared = _dot(act, sd_ref[...])
    z = ALPHA * x1 + m_ref[0, 5:6, :] * (routed + shared)
    o_ref[...] = _layernorm(z, g_ref[...], b_ref[...])


def _final(y_tiles, x1, mod, sg, su, sd, ln_g, ln_b, per_batch):
    rows = x1.shape[0]
    full = lambda a: pl.BlockSpec(a.shape, lambda i: (0,) * a.ndim)
    return pl.pallas_call(
        _final_kernel,
        grid=(rows // TM,),
        in_specs=[pl.BlockSpec((TM * SUB, LANE), lambda i: (i, 0)),
                  pl.BlockSpec((TM, D), lambda i: (i, 0)),
                  pl.BlockSpec((1, 6, D), lambda i: (_mod_row(i, per_batch), 0, 0)),
                  full(sg), full(su), full(sd), full(ln_g), full(ln_b)],
        out_specs=pl.BlockSpec((TM, D), lambda i: (i, 0)),
        out_shape=jax.ShapeDtypeStruct((rows, D), F32),
        compiler_params=_cparams(1),
        name="final",
    )(y_tiles, x1, mod, sg, su, sd, ln_g, ln_b)


def _rope_tables(rot_dim, lane_lo, n_rot, n_rows):
    nf = rot_dim // 4
    t = jnp.arange(n_rows)
    r = (t // GRID_W).astype(F32)
    col = (t % GRID_W).astype(F32)
    freqs = ROPE_BASE ** (-jnp.arange(nf, dtype=F32) / nf)
    lane = jnp.arange(LANE)
    j = (lane - lane_lo) % rot_dim
    in_rot = jnp.logical_and(lane >= lane_lo, lane < lane_lo + n_rot * rot_dim)
    half = j // (2 * nf)
    second = (j % (2 * nf)) >= nf
    f = freqs[j % nf]
    ang = jnp.where(half[None, :] == 0, r[:, None], col[:, None]) * f[None, :]
    cos = jnp.where(in_rot[None, :], jnp.cos(ang), 1.0)
    sin = jnp.where(in_rot[None, :], jnp.sin(ang), 0.0)
    sp = jnp.where(second[None, :], sin, 0.0)
    sm = jnp.where(second[None, :], 0.0, -sin)
    ident = lambda v: jnp.full((TM, LANE), v, F32)
    return (jnp.concatenate([ident(1.0), cos], axis=0), jnp.concatenate([ident(0.0), sp], axis=0),
            jnp.concatenate([ident(0.0), sm], axis=0))


def _even_weights(w_in, q_norm, w_uq, kv_norm, w_uk, w_uv, w_out):
    z = lambda n: jnp.zeros((D, n), F32)
    mla_in = MLA_QR + MLA_KVR + MLA_ROPE
    w_in_p = jnp.concatenate([w_in[:, :MLA_QR + MLA_KVR], z(MLA_NOPE), w_in[:, MLA_QR + MLA_KVR:mla_in],
                              z(MLA_HP - MLA_NOPE - MLA_ROPE), w_in[:, mla_in:]], axis=1)
    uq = w_uq.reshape(MLA_QR, MLA_HEADS, MLA_NOPE + MLA_ROPE)
    uq = jnp.pad(uq, ((0, 0), (0, 0), (0, MLA_HP - MLA_NOPE - MLA_ROPE))).reshape(MLA_QR, MLA_HEADS * MLA_HP)
    uk = w_uk.reshape(MLA_KVR, MLA_HEADS, MLA_NOPE)
    uk = jnp.pad(uk, ((0, 0), (0, 0), (0, MLA_HP - MLA_NOPE))).reshape(MLA_KVR, MLA_HEADS * MLA_HP)
    return {"w_in": w_in_p.astype(BF16), "q_norm": q_norm.reshape(1, -1), "kv_norm": kv_norm.reshape(1, -1),
            "w_uq": uq.astype(BF16), "w_uk": uk.astype(BF16), "w_uv": w_uv.astype(BF16), "w_out": w_out.astype(BF16)}


def kernel(x_prompt, x_sample, cache_mla_ckv, cache_mla_kpe, state_ret_fwd, state_ret_bwd, cache_swa_k, cache_swa_v, c, c_ctx, w_mod, b_mod, ln1_g, ln1_b, ln2_g, ln2_b, mla_ret_w_in, mla_q_norm, mla_w_uq, mla_kv_norm, mla_w_uk, mla_w_uv, ret_decay_fwd, ret_decay_bwd, ret_gn_g, even_w_out, swa_w_in, swa_sink, swa_w_out, moe_router, moe_router_bias, moe_w_gate, moe_w_up, moe_w_down, shared_w_gate, shared_w_up, shared_w_down):
    n_p, seq_p, _ = x_prompt.shape
    n_s, seq_s, _ = x_sample.shape
    past = cache_mla_ckv.shape[2]
    groups = [
        dict(x=x_prompt.reshape(n_p * seq_p, D), nb=n_p, seq=seq_p, per_batch=None),
        dict(x=x_sample.reshape(n_s * seq_s, D), nb=n_s, seq=seq_s, per_batch=seq_s // TM),
    ]
    cond8 = jnp.zeros((8, D), F32).at[0].set(c_ctx).at[1:1 + n_s].set(c)
    mods = _modulation(cond8, w_mod, b_mod).reshape(DEPTH, 8, 6, D)
    tabs_mla = _rope_tables(MLA_ROPE, MLA_NOPE, 1, seq_s)
    tabs_swa = _rope_tables(SWA_HD, 0, LANE // SWA_HD, seq_s)

    outs = {k: [] for k in ("ckv", "kpe", "rf", "rb", "sk", "sv")}
    for l in range(DEPTH):
        mod = mods[l]
        r_pad = jnp.pad(moe_router[l], ((0, 0), (0, LANE - N_EXP)))
        r_hi = r_pad.astype(BF16)
        r_lo = (r_pad - r_hi.astype(F32)).astype(BF16)
        bias_col = moe_router_bias[l].reshape(N_EXP, 1)
        sg, su, sd = (shared_w_gate[l].astype(BF16), shared_w_up[l].astype(BF16), shared_w_down[l].astype(BF16))
        if l % 2 == 0:
            e = l // 2
            ew = _even_weights(mla_ret_w_in[e], mla_q_norm[e], mla_w_uq[e], mla_kv_norm[e], mla_w_uk[e],
                               mla_w_uv[e], even_w_out[e])
            kpe_ctx = jnp.pad(cache_mla_kpe[:, e].reshape(n_s * past, MLA_ROPE),
                              ((0, 0), (MLA_NOPE, MLA_HP - MLA_NOPE - MLA_ROPE)))
            ctx_kv = _mla_ctx(cache_mla_ckv[:, e].reshape(n_s * past, MLA_KVR), kpe_ctx, ew)
            gn = ret_gn_g[e].reshape(1, RET_W)
        else:
            o = l // 2
            w_in_o = swa_w_in[o].astype(BF16)
            w_out_o = swa_w_out[o].astype(BF16)
            ctx_swa = (cache_swa_k[:, o].reshape(n_s * past, SWA_KV).astype(BF16),
                       cache_swa_v[:, o].reshape(n_s * past, SWA_KV).astype(BF16))
        for gi, g in enumerate(groups):
            x, nb, seq, per_batch = g["x"], g["nb"], g["seq"], g["per_batch"]
            is_sample = gi == 1
            if l % 2 == 0:
                proj = _modmm(x, mod, ew["w_in"], per_batch)
                q, k, v, ckv, kpe = _mla_prep(proj, tabs_mla, ew, per_batch)
                o_mla = _attention(q, k, v, ctx_kv if is_sample else None, None, n_batch=nb, seq=seq,
                                   n_heads=MLA_HEADS, group=1, dqk=MLA_HP, dv=MLA_V,
                                   scale=(MLA_NOPE + MLA_ROPE) ** -0.5, window=0)
                init = (state_ret_fwd[:, e], state_ret_bwd[:, e]) if is_sample else None
                ret = _retention(proj, ret_decay_fwd[e], ret_decay_bwd[e], gn, init, n_batch=nb, seq=seq,
                                 out_state=not is_sample)
                if is_sample:
                    o_ret = ret[0]
                else:
                    o_ret, s_f, s_b = ret
                    outs["ckv"].append(ckv.reshape(nb, seq, MLA_KVR))
                    outs["kpe"].append(kpe[:, MLA_NOPE:MLA_NOPE + MLA_ROPE].reshape(nb, seq, MLA_ROPE))
                    outs["rf"].append(s_f)
                    outs["rb"].append(s_b)
                parts, w_out = [o_mla, o_ret], ew["w_out"]
            else:
                proj = _modmm(x, mod, w_in_o, per_batch)
                q, k, v = _swa_prep(proj, tabs_swa, per_batch)
                o_swa = _attention(q, k, v, ctx_swa if is_sample else None, swa_sink[o], n_batch=nb, seq=seq,
                                   n_heads=SWA_HEADS, group=SWA_HEADS // SWA_KVH, dqk=SWA_HD, dv=SWA_HD,
                                   scale=SWA_HD ** -0.5, window=WINDOW if is_sample else 0)
                if not is_sample:
                    outs["sk"].append(proj[:, SWA_Q:SWA_Q + SWA_KV].reshape(nb, seq, SWA_KVH, SWA_HD))
                    outs["sv"].append(proj[:, SWA_Q + SWA_KV:].reshape(nb, seq, SWA_KVH, SWA_HD))
                parts, w_out = [o_swa], w_out_o
            x1, h2_tiles, logits = _post_mixer(parts, w_out, x, mod, ln1_g[l].reshape(1, D), ln1_b[l].reshape(1, D),
                                               r_hi, r_lo, per_batch)
            idx, rank, w, cnt = _route(logits, bias_col)
            tile_exp, used, dest, seg, w_flat = _dispatch_plan(idx, rank, w, cnt[:, 0])
            y_tiles = _experts(tile_exp, used, dest, seg, cnt[:, 0], w_flat, h2_tiles.reshape(-1, SUB, LANE),
                               moe_w_gate, moe_w_up, moe_w_down, l)
            g["x"] = _final(y_tiles.reshape(-1, LANE), x1, mod, sg, su, sd,
                            ln2_g[l].reshape(1, D), ln2_b[l].reshape(1, D), per_batch)
    y_prompt = groups[0]["x"].reshape(n_p, seq_p, D)
    y_sample = groups[1]["x"].reshape(n_s, seq_s, D)
    return (y_prompt, y_sample, jnp.stack(outs["ckv"], axis=1), jnp.stack(outs["kpe"], axis=1),
            jnp.stack(outs["rf"], axis=1), jnp.stack(outs["rb"], axis=1),
            jnp.stack(outs["sk"], axis=1), jnp.stack(outs["sv"], axis=1))
```

```python
import functools

import jax
import jax.numpy as jnp
from jax import lax
from jax.experimental import pallas as pl
from jax.experimental.pallas import tpu as pltpu

F32 = jnp.float32
BF16 = jnp.bfloat16

D = 1024
DEPTH = 4
GRID_W = 64
ALPHA = (2.0 * DEPTH) ** 0.25
LN_EPS = 1e-5
RMS_EPS = 1e-6
ROPE_BASE = 10000.0
NEG_INF = -1e30
MLA_HEADS = 8
MLA_NOPE = 64
MLA_ROPE = 32
MLA_V = 64
MLA_QR = 256
MLA_KVR = 128
MLA_HP = 128
RET_HEADS = 8
RET_HD = 64
RET_CHUNK = 256
RET_W = RET_HEADS * RET_HD
EVEN_P = 512 + 4 * RET_W
SWA_HEADS = 16
SWA_KVH = 4
SWA_HD = 64
WINDOW = 128
SWA_Q = SWA_HEADS * SWA_HD
SWA_KV = SWA_KVH * SWA_HD
N_EXP = 64
TOP_K = 8
N_GROUPS = 8
TOPK_GROUPS = 4
EXP_D = 256
ROUTED_SCALE = 2.5

TM = 256
SUB = 8
LANE = 128
EXP_TM = 256
VMEM_BIG = 56 * 1024 * 1024


def _cparams(n_axes, vmem=None):
    return pltpu.CompilerParams(dimension_semantics=("arbitrary",) * n_axes, vmem_limit_bytes=vmem)


def _dot(a, b):
    return jnp.dot(a, b, preferred_element_type=F32)


def _dot_nt(a, b):
    return lax.dot_general(a, b, (((1,), (1,)), ((), ())), preferred_element_type=F32)


def _dot_tn(a, b):
    return lax.dot_general(a, b, (((0,), (0,)), ((), ())), preferred_element_type=F32)


def _layernorm(z, g, b):
    mu = jnp.mean(z, axis=-1, keepdims=True)
    zc = z - mu
    var = jnp.mean(zc * zc, axis=-1, keepdims=True)
    return zc * lax.rsqrt(var + LN_EPS) * g + b


def _rmsnorm(x, g):
    return x * lax.rsqrt(jnp.mean(x * x, axis=-1, keepdims=True) + RMS_EPS) * g


def _rope(x, c, sp, sm, shift):
    w = x.shape[-1]
    return x * c + pltpu.roll(x, shift, 1) * sp + pltpu.roll(x, w - shift, 1) * sm


def _mod_kernel(c_ref, w_ref, b_ref, o_ref):
    s = jax.nn.silu(c_ref[...]).astype(BF16)
    o_ref[0] = _dot(s, w_ref[0].astype(BF16)) + b_ref[0]


def _modulation(cond8, w_mod, b_mod):
    nt = 4
    tn = 6 * D // nt
    return pl.pallas_call(
        _mod_kernel,
        grid=(DEPTH, nt),
        in_specs=[pl.BlockSpec((8, D), lambda l, j: (0, 0)),
                  pl.BlockSpec((1, D, tn), lambda l, j: (l, 0, j)),
                  pl.BlockSpec((1, 1, tn), lambda l, j: (l, 0, j))],
        out_specs=pl.BlockSpec((1, 8, tn), lambda l, j: (l, 0, j)),
        out_shape=jax.ShapeDtypeStruct((DEPTH, 8, 6 * D), F32),
        compiler_params=_cparams(2, 40 * 1024 * 1024),
        name="modulation",
    )(cond8, w_mod, b_mod.reshape(DEPTH, 1, 6 * D))


def _mod_row(i, per_batch):
    return 0 if per_batch is None else 1 + i // per_batch


def _modmm_kernel(x_ref, m_ref, w_ref, o_ref, *, shift_i, scale_i):
    h = x_ref[...] * (1.0 + m_ref[0, scale_i:scale_i + 1, :]) + m_ref[0, shift_i:shift_i + 1, :]
    o_ref[...] = _dot(h.astype(BF16), w_ref[...])


def _modmm(x, mod, w, per_batch):
    rows = x.shape[0]
    n = w.shape[1]
    return pl.pallas_call(
        functools.partial(_modmm_kernel, shift_i=0, scale_i=1),
        grid=(rows // TM,),
        in_specs=[pl.BlockSpec((TM, D), lambda i: (i, 0)),
                  pl.BlockSpec((1, 6, D), lambda i: (_mod_row(i, per_batch), 0, 0)),
                  pl.BlockSpec((D, n), lambda i: (0, 0))],
        out_specs=pl.BlockSpec((TM, n), lambda i: (i, 0)),
        out_shape=jax.ShapeDtypeStruct((rows, n), F32),
        compiler_params=_cparams(1, 40 * 1024 * 1024),
        name="modmm",
    )(x, mod, w)


def _mla_prep_kernel(p_ref, c_ref, sp_ref, sm_ref, qn_ref, kvn_ref, wuq_ref, wuk_ref, wuv_ref,
                     q_ref, k_ref, v_ref, ckv_ref, kpe_ref):
    p = p_ref[...]
    c, sp, sm = c_ref[...], sp_ref[...], sm_ref[...]
    qn = _rmsnorm(p[:, 0:MLA_QR], qn_ref[...])
    q = _dot(qn.astype(BF16), wuq_ref[...])
    ckv = _rmsnorm(p[:, MLA_QR:MLA_QR + MLA_KVR], kvn_ref[...])
    ckv_ref[...] = ckv
    kpe = _rope(p[:, MLA_QR + MLA_KVR:512], c, sp, sm, MLA_ROPE // 4)
    kpe_ref[...] = kpe
    ckv_b = ckv.astype(BF16)
    kn = _dot(ckv_b, wuk_ref[...])
    for h in range(MLA_HEADS):
        sl = slice(MLA_HP * h, MLA_HP * (h + 1))
        q_ref[:, sl] = _rope(q[:, sl], c, sp, sm, MLA_ROPE // 4).astype(BF16)
        k_ref[:, sl] = (kn[:, sl] + kpe).astype(BF16)
    v_ref[...] = _dot(ckv_b, wuv_ref[...]).astype(BF16)


def _mla_prep(proj, tabs, ew, per_batch):
    rows = proj.shape[0]
    c, sp, sm = tabs

    def tab_idx(i):
        return (0 if per_batch is None else 1 + i % per_batch, 0)

    tab_spec = pl.BlockSpec((TM, LANE), tab_idx)
    full = lambda a: pl.BlockSpec(a.shape, lambda i: (0,) * a.ndim)
    return pl.pallas_call(
        _mla_prep_kernel,
        grid=(rows // TM,),
        in_specs=[pl.BlockSpec((TM, 512), lambda i: (i, 0)), tab_spec, tab_spec, tab_spec,
                  full(ew["q_norm"]), full(ew["kv_norm"]), full(ew["w_uq"]), full(ew["w_uk"]), full(ew["w_uv"])],
        out_specs=[pl.BlockSpec((TM, MLA_HEADS * MLA_HP), lambda i: (i, 0)),
                   pl.BlockSpec((TM, MLA_HEADS * MLA_HP), lambda i: (i, 0)),
                   pl.BlockSpec((TM, MLA_HEADS * MLA_V), lambda i: (i, 0)),
                   pl.BlockSpec((TM, MLA_KVR), lambda i: (i, 0)),
                   pl.BlockSpec((TM, LANE), lambda i: (i, 0))],
        out_shape=[jax.ShapeDtypeStruct((rows, MLA_HEADS * MLA_HP), BF16),
                   jax.ShapeDtypeStruct((rows, MLA_HEADS * MLA_HP), BF16),
                   jax.ShapeDtypeStruct((rows, MLA_HEADS * MLA_V), BF16),
                   jax.ShapeDtypeStruct((rows, MLA_KVR), F32),
                   jax.ShapeDtypeStruct((rows, LANE), F32)],
        compiler_params=_cparams(1),
        name="mla_prep",
    )(proj, c, sp, sm, ew["q_norm"], ew["kv_norm"], ew["w_uq"], ew["w_uk"], ew["w_uv"])


def _mla_ctx_kernel(ckv_ref, kpe_ref, wuk_ref, wuv_ref, k_ref, v_ref):
    ckv_b = ckv_ref[...].astype(BF16)
    kn = _dot(ckv_b, wuk_ref[...])
    kpe = kpe_ref[...]
    for h in range(MLA_HEADS):
        sl = slice(MLA_HP * h, MLA_HP * (h + 1))
        k_ref[:, sl] = (kn[:, sl] + kpe).astype(BF16)
    v_ref[...] = _dot(ckv_b, wuv_ref[...]).astype(BF16)


def _mla_ctx(ckv, kpe_pad, ew):
    rows = ckv.shape[0]
    full = lambda a: pl.BlockSpec(a.shape, lambda i: (0,) * a.ndim)
    return pl.pallas_call(
        _mla_ctx_kernel,
        grid=(rows // TM,),
        in_specs=[pl.BlockSpec((TM, MLA_KVR), lambda i: (i, 0)), pl.BlockSpec((TM, LANE), lambda i: (i, 0)),
                  full(ew["w_uk"]), full(ew["w_uv"])],
        out_specs=[pl.BlockSpec((TM, MLA_HEADS * MLA_HP), lambda i: (i, 0)),
                   pl.BlockSpec((TM, MLA_HEADS * MLA_V), lambda i: (i, 0))],
        out_shape=[jax.ShapeDtypeStruct((rows, MLA_HEADS * MLA_HP), BF16),
                   jax.ShapeDtypeStruct((rows, MLA_HEADS * MLA_V), BF16)],
        compiler_params=_cparams(1),
        name="mla_ctx",
    )(ckv, kpe_pad, ew["w_uk"], ew["w_uv"])


def _attn_kernel(*refs, n_heads, group, dqk, dv, scale, has_ctx, has_sink, window, tq, seq):
    refs = list(refs)
    sink_ref = refs.pop(0) if has_sink else None
    q_ref, k_ref, v_ref = refs[:3]
    kc_ref, vc_ref = (refs[3], refs[4]) if has_ctx else (None, None)
    o_ref = refs[-1]
    i = pl.program_id(1)
    if window:
        kw = tq + 2 * window
        start = pl.multiple_of(jnp.clip(i * tq - window, 0, seq - kw), LANE)
        qpos = i * tq + lax.broadcasted_iota(jnp.int32, (tq, kw), 0)
        kpos = start + lax.broadcasted_iota(jnp.int32, (tq, kw), 1)
        valid = jnp.abs(qpos - kpos) <= window
    for h in range(n_heads):
        hk = h // group
        q = q_ref[:, h * dqk:(h + 1) * dqk]
        if window:
            k = k_ref[pl.ds(start, kw), hk * dqk:(hk + 1) * dqk]
            v = v_ref[pl.ds(start, kw), hk * dv:(hk + 1) * dv]
            s = jnp.where(valid, _dot_nt(q, k) * scale, NEG_INF)
        else:
            k = k_ref[:, hk * dqk:(hk + 1) * dqk]
            v = v_ref[:, hk * dv:(hk + 1) * dv]
            s = _dot_nt(q, k) * scale
        m = jnp.max(s, axis=-1, keepdims=True)
        if has_ctx:
            sc = _dot_nt(q, kc_ref[:, hk * dqk:(hk + 1) * dqk]) * scale
            m = jnp.maximum(m, jnp.max(sc, axis=-1, keepdims=True))
        if has_sink:
            sk = sink_ref[h]
            m = jnp.maximum(m, sk)
        p = jnp.exp(s - m)
        l = jnp.sum(p, axis=-1, keepdims=True)
        o = _dot(p.astype(BF16), v)
        if has_ctx:
            pc = jnp.exp(sc - m)
            l = l + jnp.sum(pc, axis=-1, keepdims=True)
            o = o + _dot(pc.astype(BF16), vc_ref[:, hk * dv:(hk + 1) * dv])
        if has_sink:
            l = l + jnp.exp(sk - m)
        o_ref[:, h * dv:(h + 1) * dv] = (o / l).astype(o_ref.dtype)


def _attention(q, k, v, ctx, sink, *, n_batch, seq, n_heads, group, dqk, dv, scale, window):
    tq = TM
    nq = seq // tq
    n_kv = n_heads // group
    in_specs = []
    args = []
    if sink is not None:
        in_specs.append(pl.BlockSpec(memory_space=pltpu.SMEM))
        args.append(sink)
    in_specs += [pl.BlockSpec((tq, n_heads * dqk), lambda b, i: (b * nq + i, 0)),
                 pl.BlockSpec((seq, n_kv * dqk), lambda b, i: (b, 0)),
                 pl.BlockSpec((seq, n_kv * dv), lambda b, i: (b, 0))]
    args += [q, k, v]
    if ctx is not None:
        kc, vc = ctx
        sc = kc.shape[0] // n_batch
        in_specs += [pl.BlockSpec((sc, n_kv * dqk), lambda b, i: (b, 0)),
                     pl.BlockSpec((sc, n_kv * dv), lambda b, i: (b, 0))]
        args += [kc, vc]
    kern = functools.partial(_attn_kernel, n_heads=n_heads, group=group, dqk=dqk, dv=dv, scale=scale,
                             has_ctx=ctx is not None, has_sink=sink is not None, window=window, tq=tq, seq=seq)
    return pl.pallas_call(
        kern,
        grid=(n_batch, nq),
        in_specs=in_specs,
        out_specs=pl.BlockSpec((tq, n_heads * dv), lambda b, i: (b * nq + i, 0)),
        out_shape=jax.ShapeDtypeStruct((n_batch * seq, n_heads * dv), BF16),
        compiler_params=_cparams(2, 40 * 1024 * 1024),
        name="attention",
    )(*args)


def _ret_kernel(*refs, seq, has_init, out_state):
    refs = list(refs)
    df_ref, db_ref, rq_ref, rk_ref, rv_ref, rg_ref, gn_ref = refs[:7]
    pos = 7
    if has_init:
        s0f_ref, s0b_ref = refs[pos], refs[pos + 1]
        pos += 2
    o_ref = refs[pos]
    pos += 1
    if out_state:
        sf_ref, sb_ref = refs[pos], refs[pos + 1]
        pos += 2
    of_scr, ob_scr = refs[pos], refs[pos + 1]

    pair = pl.program_id(1)
    n_chunks = seq // RET_CHUNK
    idx_c = lax.broadcasted_iota(jnp.int32, (RET_CHUNK, 1), 0).astype(F32)
    diff = (lax.broadcasted_iota(jnp.int32, (RET_CHUNK, RET_CHUNK), 0)
            - lax.broadcasted_iota(jnp.int32, (RET_CHUNK, RET_CHUNK), 1)).astype(F32)
    for hh in range(2):
        h = 2 * pair + hh
        hs = slice(RET_HD * hh, RET_HD * (hh + 1))
        rows = [slice(RET_CHUNK * ci, RET_CHUNK * (ci + 1)) for ci in range(n_chunks)]
        qs = [rq_ref[r, hs].astype(BF16) for r in rows]
        ks = [rk_ref[r, hs] * (RET_HD ** -0.5) for r in rows]
        vs = [rv_ref[r, hs].astype(BF16) for r in rows]
        qk = [_dot_nt(q, k.astype(BF16)) for q, k in zip(qs, ks)]
        for fwd in (True, False):
            d = jnp.full((1, 1), (df_ref if fwd else db_ref)[h], F32)
            lg = jnp.minimum(d, 0.0) - jnp.log1p(jnp.exp(-jnp.abs(d)))
            dd = diff if fwd else -diff
            mask = jnp.where(dd >= 0, jnp.exp(lg * jnp.maximum(dd, 0.0)), 0.0)
            if fwd:
                q_dec = jnp.exp(lg * (idx_c + 1.0))
                k_dec = jnp.exp(lg * (RET_CHUNK - 1.0 - idx_c))
            else:
                q_dec = jnp.exp(lg * (RET_CHUNK - idx_c))
                k_dec = jnp.exp(lg * idx_c)
            c_dec = jnp.exp(lg * RET_CHUNK)
            scr = of_scr if fwd else ob_scr
            if has_init:
                state0 = (s0f_ref if fwd else s0b_ref)[0, hh]
            else:
                state0 = jnp.zeros((RET_HD, RET_HD), F32)

            state = state0
            for ci in (range(n_chunks) if fwd else reversed(range(n_chunks))):
                o = _dot((qk[ci] * mask).astype(BF16), vs[ci]) + _dot(qs[ci], state.astype(BF16)) * q_dec
                scr[rows[ci], hs] = o
                state = state * c_dec + _dot_tn((ks[ci] * k_dec).astype(BF16), vs[ci])
            if out_state:
                (sf_ref if fwd else sb_ref)[0, hh] = state

    def head_norm(x):
        mu = jnp.mean(x, axis=-1, keepdims=True)
        xc = x - mu
        return xc * lax.rsqrt(jnp.mean(xc * xc, axis=-1, keepdims=True) + LN_EPS)

    for hh in range(2):
        hs = slice(RET_HD * hh, RET_HD * (hh + 1))
        o = head_norm(of_scr[:, hs]) + head_norm(ob_scr[:, hs])
        o_ref[:, hs] = (o * gn_ref[:, hs] * jax.nn.silu(rg_ref[:, hs])).astype(o_ref.dtype)


def _retention(proj, dec_f, dec_b, gn, init, *, n_batch, seq, out_state):
    pairs = RET_HEADS // 2
    col0 = 512 // LANE

    def col_spec(k):
        return pl.BlockSpec((seq, LANE), lambda b, p: (b, col0 + k * pairs + p))

    smem = pl.BlockSpec(memory_space=pltpu.SMEM)
    st_spec = pl.BlockSpec((1, 2, RET_HD, RET_HD), lambda b, p: (b, p, 0, 0))
    in_specs = [smem, smem, col_spec(0), col_spec(1), col_spec(2), col_spec(3),
                pl.BlockSpec((1, LANE), lambda b, p: (0, p))]
    args = [dec_f, dec_b, proj, proj, proj, proj, gn]
    if init is not None:
        in_specs += [st_spec, st_spec]
        args += list(init)
    out_specs = [pl.BlockSpec((seq, LANE), lambda b, p: (b, p))]
    out_shape = [jax.ShapeDtypeStruct((n_batch * seq, RET_W), BF16)]
    if out_state:
        out_specs += [st_spec, st_spec]
        out_shape += [jax.ShapeDtypeStruct((n_batch, RET_HEADS, RET_HD, RET_HD), F32)] * 2
    kern = functools.partial(_ret_kernel, seq=seq, has_init=init is not None, out_state=out_state)
    return pl.pallas_call(
        kern,
        grid=(n_batch, pairs),
        in_specs=in_specs,
        out_specs=out_specs,
        out_shape=out_shape,
        scratch_shapes=[pltpu.VMEM((seq, LANE), F32), pltpu.VMEM((seq, LANE), F32)],
        compiler_params=_cparams(2),
        name="retention",
    )(*args)


def _swa_prep_kernel(p_ref, c_ref, sp_ref, sm_ref, q_ref, k_ref, v_ref):
    c, sp, sm = c_ref[...], sp_ref[...], sm_ref[...]
    for j in range(SWA_Q // LANE):
        sl = slice(LANE * j, LANE * (j + 1))
        q_ref[:, sl] = _rope(p_ref[:, sl], c, sp, sm, SWA_HD // 4).astype(BF16)
    for j in range(SWA_KV // LANE):
        sl = slice(LANE * j, LANE * (j + 1))
        k_ref[:, sl] = _rope(p_ref[:, SWA_Q + LANE * j:SWA_Q + LANE * (j + 1)], c, sp, sm, SWA_HD // 4).astype(BF16)
    v_ref[...] = p_ref[:, SWA_Q + SWA_KV:].astype(BF16)


def _swa_prep(proj, tabs, per_batch):
    rows = proj.shape[0]
    c, sp, sm = tabs

    def tab_idx(i):
        return (0 if per_batch is None else 1 + i % per_batch, 0)

    tab_spec = pl.BlockSpec((TM, LANE), tab_idx)
    return pl.pallas_call(
        _swa_prep_kernel,
        grid=(rows // TM,),
        in_specs=[pl.BlockSpec((TM, SWA_Q + 2 * SWA_KV), lambda i: (i, 0)), tab_spec, tab_spec, tab_spec],
        out_specs=[pl.BlockSpec((TM, SWA_Q), lambda i: (i, 0)),
                   pl.BlockSpec((TM, SWA_KV), lambda i: (i, 0)),
                   pl.BlockSpec((TM, SWA_KV), lambda i: (i, 0))],
        out_shape=[jax.ShapeDtypeStruct((rows, SWA_Q), BF16),
                   jax.ShapeDtypeStruct((rows, SWA_KV), BF16),
                   jax.ShapeDtypeStruct((rows, SWA_KV), BF16)],
        compiler_params=_cparams(1),
        name="swa_prep",
    )(proj, c, sp, sm)


def _to_row_tiles(ref, x):
    rows = x.shape[0]
    for s in range(D // LANE):
        ref[pl.ds(s, rows, stride=SUB), :] = x[:, LANE * s:LANE * (s + 1)]


def _from_row_tiles(ref, rows):
    return jnp.concatenate([ref[pl.ds(s, rows, stride=SUB), :] for s in range(D // LANE)], axis=1)


def _post_kernel(*refs, n_parts):
    a_refs = refs[:n_parts]
    w_ref, x_ref, m_ref, g_ref, b_ref, rh_ref, rl_ref, x1_ref, h2_ref, lg_ref = refs[n_parts:]
    out = None
    off = 0
    for a_ref in a_refs:
        kk = a_ref.shape[1]
        part = _dot(a_ref[...], w_ref[off:off + kk, :])
        out = part if out is None else out + part
        off += kk
    z = ALPHA * x_ref[...] + m_ref[0, 2:3, :] * out
    x1 = _layernorm(z, g_ref[...], b_ref[...])
    x1_ref[...] = x1
    h2 = x1 * (1.0 + m_ref[0, 4:5, :]) + m_ref[0, 3:4, :]
    _to_row_tiles(h2_ref, h2)
    hi = h2.astype(BF16)
    lo = (h2 - hi.astype(F32)).astype(BF16)
    rh = rh_ref[...]
    lg_ref[...] = _dot(hi, rh) + _dot(lo, rh) + _dot(hi, rl_ref[...])


def _post_mixer(parts, w_out, x, mod, ln_g, ln_b, r_hi, r_lo, per_batch):
    rows = x.shape[0]
    full = lambda a: pl.BlockSpec(a.shape, lambda i: (0,) * a.ndim)
    in_specs = [pl.BlockSpec((TM, a.shape[1]), lambda i: (i, 0)) for a in parts]
    in_specs += [full(w_out), pl.BlockSpec((TM, D), lambda i: (i, 0)),
                 pl.BlockSpec((1, 6, D), lambda i: (_mod_row(i, per_batch), 0, 0)),
                 full(ln_g), full(ln_b), full(r_hi), full(r_lo)]
    return pl.pallas_call(
        functools.partial(_post_kernel, n_parts=len(parts)),
        grid=(rows // TM,),
        in_specs=in_specs,
        out_specs=[pl.BlockSpec((TM, D), lambda i: (i, 0)),
                   pl.BlockSpec((TM * SUB, LANE), lambda i: (i, 0)),
                   pl.BlockSpec((TM, LANE), lambda i: (i, 0))],
        out_shape=[jax.ShapeDtypeStruct((rows, D), F32),
                   jax.ShapeDtypeStruct((rows * SUB, LANE), F32),
                   jax.ShapeDtypeStruct((rows, LANE), F32)],
        compiler_params=_cparams(1),
        name="post_mixer",
    )(*parts, w_out, x, mod, ln_g, ln_b, r_hi, r_lo)


def _route_kernel(lg_ref, bias_ref, idx_ref, rank_ref, w_ref, cnt_ref, carry_ref):
    i = pl.program_id(0)

    @pl.when(i == 0)
    def _():
        carry_ref[...] = jnp.zeros_like(carry_ref)

    t = lg_ref.shape[0]
    gsz = N_EXP // N_GROUPS
    scores = jax.nn.sigmoid(lg_ref[...].T[:N_EXP])
    sel = scores + bias_ref[...]
    g3 = sel.reshape(N_GROUPS, gsz, t)
    sub_iota = lax.broadcasted_iota(jnp.int32, g3.shape, 1)
    m1 = jnp.max(g3, axis=1)
    first = jnp.min(jnp.where(g3 == m1[:, None, :], sub_iota, gsz), axis=1)
    m2 = jnp.max(jnp.where(sub_iota == first[:, None, :], -jnp.inf, g3), axis=1)
    grp = m1 + m2
    g_iota = lax.broadcasted_iota(jnp.int32, grp.shape, 0)
    gmask = jnp.zeros(grp.shape, jnp.bool_)
    for _ in range(TOPK_GROUPS):
        gm = jnp.max(grp, axis=0, keepdims=True)
        gi = jnp.min(jnp.where(grp == gm, g_iota, N_GROUPS), axis=0, keepdims=True)
        hit = g_iota == gi
        gmask = jnp.logical_or(gmask, hit)
        grp = jnp.where(hit, -jnp.inf, grp)
    emask = jnp.broadcast_to(gmask[:, None, :], g3.shape).reshape(N_EXP, t)
    cur = jnp.where(emask, sel, NEG_INF)
    e_iota = lax.broadcasted_iota(jnp.int32, cur.shape, 0)
    hits = []
    member = jnp.zeros(cur.shape, F32)
    for _ in range(TOP_K):
        cm = jnp.max(cur, axis=0, keepdims=True)
        ci = jnp.min(jnp.where(cur == cm, e_iota, N_EXP), axis=0, keepdims=True)
        hit = e_iota == ci
        hits.append((hit, ci))
        member = member + hit.astype(F32)
        cur = jnp.where(hit, -jnp.inf, cur)
    tri = (lax.broadcasted_iota(jnp.int32, (t, t), 0) < lax.broadcasted_iota(jnp.int32, (t, t), 1)).astype(BF16)
    before = _dot(member.astype(BF16), tri) + carry_ref[:, 0:1]
    ws = [jnp.sum(jnp.where(hit, scores, 0.0), axis=0, keepdims=True) for hit, _ in hits]
    wsum = ws[0]
    for w in ws[1:]:
        wsum = wsum + w
    for k, (hit, ci) in enumerate(hits):
        idx_ref[k:k + 1, :] = ci
        rank_ref[k:k + 1, :] = jnp.sum(jnp.where(hit, before, 0.0), axis=0, keepdims=True).astype(jnp.int32)
        w_ref[k:k + 1, :] = ws[k] / wsum * ROUTED_SCALE
    total = carry_ref[...] + jnp.sum(member, axis=1, keepdims=True)
    carry_ref[...] = total
    cnt_ref[...] = total.astype(jnp.int32)


def _route(logits, bias_col):
    rows = logits.shape[0]
    row_spec = pl.BlockSpec((TOP_K, TM), lambda i: (0, i))
    return pl.pallas_call(
        _route_kernel,
        grid=(rows // TM,),
        in_specs=[pl.BlockSpec((TM, LANE), lambda i: (i, 0)), pl.BlockSpec((N_EXP, 1), lambda i: (0, 0))],
        out_specs=[row_spec, row_spec, row_spec, pl.BlockSpec((N_EXP, LANE), lambda i: (0, 0))],
        out_shape=[jax.ShapeDtypeStruct((TOP_K, rows), jnp.int32),
                   jax.ShapeDtypeStruct((TOP_K, rows), jnp.int32),
                   jax.ShapeDtypeStruct((TOP_K, rows), F32),
                   jax.ShapeDtypeStruct((N_EXP, LANE), jnp.int32)],
        scratch_shapes=[pltpu.VMEM((N_EXP, LANE), F32)],
        compiler_params=_cparams(1),
        name="route",
    )(logits, bias_col)


EXP_LEAD = 2 * EXP_TM


def _experts_kernel(exp_ref, used_ref, dest_ref, seg_ref, cnt_ref, w_ref,
                    x_ref, wg_ref, wu_ref, wd_ref, y_ref,
                    xbuf0, xbuf1, obuf0, obuf1, wg_b, wu_b, wd_b, asg, *, n_tok):
    s = pl.program_id(0)
    used = used_ref[0]
    n_assign = dest_ref.shape[0]
    pad_id = n_assign

    @pl.when(s == 0)
    def _():
        y_ref[...] = jnp.zeros_like(y_ref)
        for buf in (xbuf0, xbuf1, obuf0, obuf1):
            buf[...] = jnp.zeros_like(buf)

        def lead(j, c):
            for u in range(SUB):
                asg[j * SUB + u] = pad_id
            return c
        lax.fori_loop(0, EXP_LEAD // SUB, lead, 0)

        def fill(e, c):
            last = seg_ref[e + 1] - EXP_TM

            @pl.when(cnt_ref[e] > 0)
            def _():
                def eight(j, c2):
                    for u in range(SUB):
                        asg[last + j * SUB + u] = pad_id
                    return c2
                lax.fori_loop(0, EXP_TM // SUB, eight, 0)
            return c
        lax.fori_loop(0, N_EXP, fill, 0)

        def invert(j, c):
            for u in range(16):
                a = j * 16 + u
                asg[dest_ref[a]] = a
            return c
        lax.fori_loop(0, n_assign // 16, invert, 0)

    mm_tile = jnp.clip(s - 1, 0, used - 1)

    @pl.when(jnp.logical_or(s == 0, exp_ref[mm_tile] != exp_ref[jnp.maximum(mm_tile - 1, 0)]))
    def _():
        wg_b[...] = wg_ref[...].astype(BF16)
        wu_b[...] = wu_ref[...].astype(BF16)
        wd_b[...] = wd_ref[...].astype(BF16)

    def stages(x_gather, x_mm, o_mm, o_scatter):
        g_slots = asg.at[pl.ds(EXP_LEAD + jnp.minimum(s, used - 1) * EXP_TM, EXP_TM)]
        for r in range(EXP_TM):
            row0 = g_slots[r] & ((n_tok - 1) * SUB)
            x_gather[r * SUB:(r + 1) * SUB, :] = x_ref[pl.ds(pl.multiple_of(row0, SUB), SUB), :]

        x = _from_row_tiles(x_mm, EXP_TM).astype(BF16)
        h = (jax.nn.silu(_dot(x, wg_b[...])) * _dot(x, wu_b[...])).astype(BF16)
        _to_row_tiles(o_mm, _dot(h, wd_b[...]))

        s_slots = asg.at[pl.ds(s * EXP_TM, EXP_TM)]
        for j in range(EXP_TM // SUB):
            vals = []
            for u in range(SUB):
                r = j * SUB + u
                a = s_slots[r]
                rows = pl.ds(pl.multiple_of(a & -SUB, SUB), SUB)
                vals.append((rows, y_ref[rows, :] + w_ref[a] * o_scatter[r * SUB:(r + 1) * SUB, :]))
            for rows, v in vals:
                y_ref[rows, :] = v

    @pl.when(s < used + 2)
    def _():
        @pl.when(s % 2 == 0)
        def _():
            stages(xbuf0, xbuf1, obuf1, obuf0)

        @pl.when(s % 2 == 1)
        def _():
            stages(xbuf1, xbuf0, obuf0, obuf1)


def _experts(tile_exp, used, dest, seg, cnt, w_flat, h2_tiles, wg, wu, wd, layer):
    n_tok = h2_tiles.shape[0] // SUB
    assert TOP_K == SUB and dest.shape[0] == n_tok * TOP_K and w_flat.shape[0] == dest.shape[0] + SUB
    n_tiles = tile_exp.shape[0]
    wmap = lambda s, exp, used, *_: (layer, exp[jnp.clip(s - 1, 0, used[0] - 1)], 0, 0)
    buf = pltpu.VMEM((EXP_TM * SUB, LANE), F32)
    gs = pltpu.PrefetchScalarGridSpec(
        num_scalar_prefetch=6,
        grid=(n_tiles + 2,),
        in_specs=[pl.BlockSpec(memory_space=pltpu.VMEM),
                  pl.BlockSpec((None, None, D, EXP_D), wmap),
                  pl.BlockSpec((None, None, D, EXP_D), wmap),
                  pl.BlockSpec((None, None, EXP_D, D), wmap)],
        out_specs=pl.BlockSpec(memory_space=pltpu.VMEM),
        scratch_shapes=[buf, buf, buf, buf,
                        pltpu.VMEM((D, EXP_D), BF16), pltpu.VMEM((D, EXP_D), BF16), pltpu.VMEM((EXP_D, D), BF16),
                        pltpu.SMEM((EXP_LEAD + n_tiles * EXP_TM,), jnp.int32)],
    )
    return pl.pallas_call(
        functools.partial(_experts_kernel, n_tok=n_tok),
        grid_spec=gs,
        out_shape=jax.ShapeDtypeStruct(((n_tok + 1) * SUB, LANE), F32),
        compiler_params=_cparams(1, VMEM_BIG),
        name="experts",
    )(tile_exp, used, dest, seg, cnt, w_flat, h2_tiles, wg, wu, wd)


def _dispatch_plan(idx, rank, w, cnt):
    n_tok = idx.shape[1]
    n_tiles = n_tok * TOP_K // EXP_TM + N_EXP
    tiles_e = (cnt + EXP_TM - 1) // EXP_TM
    tile_end = jnp.cumsum(tiles_e)
    seg = (EXP_LEAD + jnp.concatenate([jnp.zeros((1,), jnp.int32), tile_end * EXP_TM])).astype(jnp.int32)
    e_ids = jnp.arange(N_EXP, dtype=jnp.int32)
    dest = jnp.sum(jnp.where(idx[:, :, None] == e_ids, seg[:N_EXP], 0), axis=-1) + rank
    ii = jnp.arange(n_tiles, dtype=jnp.int32)
    used = tile_end[-1:].astype(jnp.int32)
    tile_exp = jnp.sum(tile_end[None, :] <= jnp.minimum(ii, used - 1)[:, None], axis=1).astype(jnp.int32)
    tile_exp = jnp.minimum(tile_exp, N_EXP - 1)
    w_flat = jnp.concatenate([w.T.reshape(-1), jnp.zeros((SUB,), F32)])
    return tile_exp, used, dest.T.reshape(-1), seg, w_flat


def _final_kernel(y_ref, x1_ref, m_ref, sg_ref, su_ref, sd_ref, g_ref, b_ref, o_ref):
    x1 = x1_ref[...]
    routed = _from_row_tiles(y_ref, TM)
    h2 = (x1 * (1.0 + m_ref[0, 4:5, :]) + m_ref[0, 3:4, :]).astype(BF16)
    act = (jax.nn.silu(_dot(h2, sg_ref[...])) * _dot(h2, su_ref[...])).astype(BF16)
    shared = _dot(act, sd_ref[...])
    z = ALPHA * x1 + m_ref[0, 5:6, :] * (routed + shared)
    o_ref[...] = _layernorm(z, g_ref[...], b_ref[...])


def _final(y_tiles, x1, mod, sg, su, sd, ln_g, ln_b, per_batch):
    rows = x1.shape[0]
    full = lambda a: pl.BlockSpec(a.shape, lambda i: (0,) * a.ndim)
    return pl.pallas_call(
        _final_kernel,
        grid=(rows // TM,),
        in_specs=[pl.BlockSpec((TM * SUB, LANE), lambda i: (i, 0)),
                  pl.BlockSpec((TM, D), lambda i: (i, 0)),
                  pl.BlockSpec((1, 6, D), lambda i: (_mod_row(i, per_batch), 0, 0)),
                  full(sg), full(su), full(sd), full(ln_g), full(ln_b)],
        out_specs=pl.BlockSpec((TM, D), lambda i: (i, 0)),
        out_shape=jax.ShapeDtypeStruct((rows, D), F32),
        compiler_params=_cparams(1),
        name="final",
    )(y_tiles, x1, mod, sg, su, sd, ln_g, ln_b)


def _rope_tables(rot_dim, lane_lo, n_rot, n_rows):
    nf = rot_dim // 4
    t = jnp.arange(n_rows)
    r = (t // GRID_W).astype(F32)
    col = (t % GRID_W).astype(F32)
    freqs = ROPE_BASE ** (-jnp.arange(nf, dtype=F32) / nf)
    lane = jnp.arange(LANE)
    j = (lane - lane_lo) % rot_dim
    in_rot = jnp.logical_and(lane >= lane_lo, lane < lane_lo + n_rot * rot_dim)
    half = j // (2 * nf)
    second = (j % (2 * nf)) >= nf
    f = freqs[j % nf]
    ang = jnp.where(half[None, :] == 0, r[:, None], col[:, None]) * f[None, :]
    cos = jnp.where(in_rot[None, :], jnp.cos(ang), 1.0)
    sin = jnp.where(in_rot[None, :], jnp.sin(ang), 0.0)
    sp = jnp.where(second[None, :], sin, 0.0)
    sm = jnp.where(second[None, :], 0.0, -sin)
    ident = lambda v: jnp.full((TM, LANE), v, F32)
    return (jnp.concatenate([ident(1.0), cos], axis=0), jnp.concatenate([ident(0.0), sp], axis=0),
            jnp.concatenate([ident(0.0), sm], axis=0))


def _even_weights(w_in, q_norm, w_uq, kv_norm, w_uk, w_uv, w_out):
    z = lambda n: jnp.zeros((D, n), F32)
    mla_in = MLA_QR + MLA_KVR + MLA_ROPE
    w_in_p = jnp.concatenate([w_in[:, :MLA_QR + MLA_KVR], z(MLA_NOPE), w_in[:, MLA_QR + MLA_KVR:mla_in],
                              z(MLA_HP - MLA_NOPE - MLA_ROPE), w_in[:, mla_in:]], axis=1)
    uq = w_uq.reshape(MLA_QR, MLA_HEADS, MLA_NOPE + MLA_ROPE)
    uq = jnp.pad(uq, ((0, 0), (0, 0), (0, MLA_HP - MLA_NOPE - MLA_ROPE))).reshape(MLA_QR, MLA_HEADS * MLA_HP)
    uk = w_uk.reshape(MLA_KVR, MLA_HEADS, MLA_NOPE)
    uk = jnp.pad(uk, ((0, 0), (0, 0), (0, MLA_HP - MLA_NOPE))).reshape(MLA_KVR, MLA_HEADS * MLA_HP)
    return {"w_in": w_in_p.astype(BF16), "q_norm": q_norm.reshape(1, -1), "kv_norm": kv_norm.reshape(1, -1),
            "w_uq": uq.astype(BF16), "w_uk": uk.astype(BF16), "w_uv": w_uv.astype(BF16), "w_out": w_out.astype(BF16)}


def kernel(x_prompt, x_sample, cache_mla_ckv, cache_mla_kpe, state_ret_fwd, state_ret_bwd, cache_swa_k, cache_swa_v, c, c_ctx, w_mod, b_mod, ln1_g, ln1_b, ln2_g, ln2_b, mla_ret_w_in, mla_q_norm, mla_w_uq, mla_kv_norm, mla_w_uk, mla_w_uv, ret_decay_fwd, ret_decay_bwd, ret_gn_g, even_w_out, swa_w_in, swa_sink, swa_w_out, moe_router, moe_router_bias, moe_w_gate, moe_w_up, moe_w_down, shared_w_gate, shared_w_up, shared_w_down):
    n_p, seq_p, _ = x_prompt.shape
    n_s, seq_s, _ = x_sample.shape
    past = cache_mla_ckv.shape[2]
    groups = [
        dict(x=x_prompt.reshape(n_p * seq_p, D), nb=n_p, seq=seq_p, per_batch=None),
        dict(x=x_sample.reshape(n_s * seq_s, D), nb=n_s, seq=seq_s, per_batch=seq_s // TM),
    ]
    cond8 = jnp.zeros((8, D), F32).at[0].set(c_ctx).at[1:1 + n_s].set(c)
    mods = _modulation(cond8, w_mod, b_mod).reshape(DEPTH, 8, 6, D)
    tabs_mla = _rope_tables(MLA_ROPE, MLA_NOPE, 1, seq_s)
    tabs_swa = _rope_tables(SWA_HD, 0, LANE // SWA_HD, seq_s)

    outs = {k: [] for k in ("ckv", "kpe", "rf", "rb", "sk", "sv")}
    for l in range(DEPTH):
        mod = mods[l]
        r_pad = jnp.pad(moe_router[l], ((0, 0), (0, LANE - N_EXP)))
        r_hi = r_pad.astype(BF16)
        r_lo = (r_pad - r_hi.astype(F32)).astype(BF16)
        bias_col = moe_router_bias[l].reshape(N_EXP, 1)
        sg, su, sd = (shared_w_gate[l].astype(BF16), shared_w_up[l].astype(BF16), shared_w_down[l].astype(BF16))
        if l % 2 == 0:
            e = l // 2
            ew = _even_weights(mla_ret_w_in[e], mla_q_norm[e], mla_w_uq[e], mla_kv_norm[e], mla_w_uk[e],
                               mla_w_uv[e], even_w_out[e])
            kpe_ctx = jnp.pad(cache_mla_kpe[:, e].reshape(n_s * past, MLA_ROPE),
                              ((0, 0), (MLA_NOPE, MLA_HP - MLA_NOPE - MLA_ROPE)))
            ctx_kv = _mla_ctx(cache_mla_ckv[:, e].reshape(n_s * past, MLA_KVR), kpe_ctx, ew)
            gn = ret_gn_g[e].reshape(1, RET_W)
        else:
            o = l // 2
            w_in_o = swa_w_in[o].astype(BF16)
            w_out_o = swa_w_out[o].astype(BF16)
            ctx_swa = (cache_swa_k[:, o].reshape(n_s * past, SWA_KV).astype(BF16),
                       cache_swa_v[:, o].reshape(n_s * past, SWA_KV).astype(BF16))
        for gi, g in enumerate(groups):
            x, nb, seq, per_batch = g["x"], g["nb"], g["seq"], g["per_batch"]
            is_sample = gi == 1
            if l % 2 == 0:
                proj = _modmm(x, mod, ew["w_in"], per_batch)
                q, k, v, ckv, kpe = _mla_prep(proj, tabs_mla, ew, per_batch)
                o_mla = _attention(q, k, v, ctx_kv if is_sample else None, None, n_batch=nb, seq=seq,
                                   n_heads=MLA_HEADS, group=1, dqk=MLA_HP, dv=MLA_V,
                                   scale=(MLA_NOPE + MLA_ROPE) ** -0.5, window=0)
                init = (state_ret_fwd[:, e], state_ret_bwd[:, e]) if is_sample else None
                ret = _retention(proj, ret_decay_fwd[e], ret_decay_bwd[e], gn, init, n_batch=nb, seq=seq,
                                 out_state=not is_sample)
                if is_sample:
                    o_ret = ret[0]
                else:
                    o_ret, s_f, s_b = ret
                    outs["ckv"].append(ckv.reshape(nb, seq, MLA_KVR))
                    outs["kpe"].append(kpe[:, MLA_NOPE:MLA_NOPE + MLA_ROPE].reshape(nb, seq, MLA_ROPE))
                    outs["rf"].append(s_f)
                    outs["rb"].append(s_b)
                parts, w_out = [o_mla, o_ret], ew["w_out"]
            else:
                proj = _modmm(x, mod, w_in_o, per_batch)
                q, k, v = _swa_prep(proj, tabs_swa, per_batch)
                o_swa = _attention(q, k, v, ctx_swa if is_sample else None, swa_sink[o], n_batch=nb, seq=seq,
                                   n_heads=SWA_HEADS, group=SWA_HEADS // SWA_KVH, dqk=SWA_HD, dv=SWA_HD,
                                   scale=SWA_HD ** -0.5, window=WINDOW if is_sample else 0)
                if not is_sample:
                    outs["sk"].append(proj[:, SWA_Q:SWA_Q + SWA_KV].reshape(nb, seq, SWA_KVH, SWA_HD))
                    outs["sv"].append(proj[:, SWA_Q + SWA_KV:].reshape(nb, seq, SWA_KVH, SWA_HD))
                parts, w_out = [o_swa], w_out_o
            x1, h2_tiles, logits = _post_mixer(parts, w_out, x, mod, ln1_g[l].reshape(1, D), ln1_b[l].reshape(1, D),
                                               r_hi, r_lo, per_batch)
            idx, rank, w, cnt = _route(logits, bias_col)
            tile_exp, used, dest, seg, w_flat = _dispatch_plan(idx, rank, w, cnt[:, 0])
            y_tiles = _experts(tile_exp, used, dest, seg, cnt[:, 0], w_flat, h2_tiles,
                               moe_w_gate, moe_w_up, moe_w_down, l)
            g["x"] = _final(y_tiles, x1, mod, sg, su, sd,
                            ln2_g[l].reshape(1, D), ln2_b[l].reshape(1, D), per_batch)
    y_prompt = groups[0]["x"].reshape(n_p, seq_p, D)
    y_sample = groups[1]["x"].reshape(n_s, seq_s, D)
    return (y_prompt, y_sample, jnp.stack(outs["ckv"], axis=1), jnp.stack(outs["kpe"], axis=1),
            jnp.stack(outs["rf"], axis=1), jnp.stack(outs["rb"], axis=1),
            jnp.stack(outs["sk"], axis=1), jnp.stack(outs["sv"], axis=1))
```

```python
import functools

import jax
import jax.numpy as jnp
from jax import lax
from jax.experimental import pallas as pl
from jax.experimental.pallas import tpu as pltpu

F32 = jnp.float32
BF16 = jnp.bfloat16

D = 1024
DEPTH = 4
GRID_W = 64
ALPHA = (2.0 * DEPTH) ** 0.25
LN_EPS = 1e-5
RMS_EPS = 1e-6
ROPE_BASE = 10000.0
NEG_INF = -1e30
MLA_HEADS = 8
MLA_NOPE = 64
MLA_ROPE = 32
MLA_V = 64
MLA_QR = 256
MLA_KVR = 128
MLA_HP = 128
RET_HEADS = 8
RET_HD = 64
RET_CHUNK = 256
RET_W = RET_HEADS * RET_HD
EVEN_P = 512 + 4 * RET_W
SWA_HEADS = 16
SWA_KVH = 4
SWA_HD = 64
WINDOW = 128
SWA_Q = SWA_HEADS * SWA_HD
SWA_KV = SWA_KVH * SWA_HD
N_EXP = 64
TOP_K = 8
N_GROUPS = 8
TOPK_GROUPS = 4
EXP_D = 256
ROUTED_SCALE = 2.5

TM = 256
SUB = 8
LANE = 128
EXP_TM = 256
VMEM_BIG = 56 * 1024 * 1024


def _cparams(n_axes, vmem=None):
    return pltpu.CompilerParams(dimension_semantics=("arbitrary",) * n_axes, vmem_limit_bytes=vmem)


def _dot(a, b):
    return jnp.dot(a, b, preferred_element_type=F32)


def _dot_nt(a, b):
    return lax.dot_general(a, b, (((1,), (1,)), ((), ())), preferred_element_type=F32)


def _dot_tn(a, b):
    return lax.dot_general(a, b, (((0,), (0,)), ((), ())), preferred_element_type=F32)


def _layernorm(z, g, b):
    mu = jnp.mean(z, axis=-1, keepdims=True)
    zc = z - mu
    var = jnp.mean(zc * zc, axis=-1, keepdims=True)
    return zc * lax.rsqrt(var + LN_EPS) * g + b


def _rmsnorm(x, g):
    return x * lax.rsqrt(jnp.mean(x * x, axis=-1, keepdims=True) + RMS_EPS) * g


def _rope(x, c, sp, sm, shift):
    w = x.shape[-1]
    return x * c + pltpu.roll(x, shift, 1) * sp + pltpu.roll(x, w - shift, 1) * sm


def _mod_kernel(c_ref, w_ref, b_ref, o_ref):
    s = jax.nn.silu(c_ref[...]).astype(BF16)
    o_ref[0] = _dot(s, w_ref[0].astype(BF16)) + b_ref[0]


def _modulation(cond8, w_mod, b_mod):
    nt = 4
    tn = 6 * D // nt
    return pl.pallas_call(
        _mod_kernel,
        grid=(DEPTH, nt),
        in_specs=[pl.BlockSpec((8, D), lambda l, j: (0, 0)),
                  pl.BlockSpec((1, D, tn), lambda l, j: (l, 0, j)),
                  pl.BlockSpec((1, 1, tn), lambda l, j: (l, 0, j))],
        out_specs=pl.BlockSpec((1, 8, tn), lambda l, j: (l, 0, j)),
        out_shape=jax.ShapeDtypeStruct((DEPTH, 8, 6 * D), F32),
        compiler_params=_cparams(2, 40 * 1024 * 1024),
        name="modulation",
    )(cond8, w_mod, b_mod.reshape(DEPTH, 1, 6 * D))


def _mod_row(i, per_batch):
    return 0 if per_batch is None else 1 + i // per_batch


def _modmm_kernel(x_ref, m_ref, w_ref, o_ref, *, shift_i, scale_i):
    h = x_ref[...] * (1.0 + m_ref[0, scale_i:scale_i + 1, :]) + m_ref[0, shift_i:shift_i + 1, :]
    o_ref[...] = _dot(h.astype(BF16), w_ref[...])


def _modmm(x, mod, w, per_batch):
    rows = x.shape[0]
    n = w.shape[1]
    return pl.pallas_call(
        functools.partial(_modmm_kernel, shift_i=0, scale_i=1),
        grid=(rows // TM,),
        in_specs=[pl.BlockSpec((TM, D), lambda i: (i, 0)),
                  pl.BlockSpec((1, 6, D), lambda i: (_mod_row(i, per_batch), 0, 0)),
                  pl.BlockSpec((D, n), lambda i: (0, 0))],
        out_specs=pl.BlockSpec((TM, n), lambda i: (i, 0)),
        out_shape=jax.ShapeDtypeStruct((rows, n), F32),
        compiler_params=_cparams(1, 40 * 1024 * 1024),
        name="modmm",
    )(x, mod, w)


def _mla_prep_kernel(p_ref, c_ref, sp_ref, sm_ref, qn_ref, kvn_ref, wuq_ref, wuk_ref, wuv_ref,
                     q_ref, k_ref, v_ref, ckv_ref, kpe_ref):
    p = p_ref[...]
    c, sp, sm = c_ref[...], sp_ref[...], sm_ref[...]
    qn = _rmsnorm(p[:, 0:MLA_QR], qn_ref[...])
    q = _dot(qn.astype(BF16), wuq_ref[...])
    ckv = _rmsnorm(p[:, MLA_QR:MLA_QR + MLA_KVR], kvn_ref[...])
    ckv_ref[...] = ckv
    kpe = _rope(p[:, MLA_QR + MLA_KVR:512], c, sp, sm, MLA_ROPE // 4)
    kpe_ref[...] = kpe
    ckv_b = ckv.astype(BF16)
    kn = _dot(ckv_b, wuk_ref[...])
    for h in range(MLA_HEADS):
        sl = slice(MLA_HP * h, MLA_HP * (h + 1))
        q_ref[:, sl] = _rope(q[:, sl], c, sp, sm, MLA_ROPE // 4).astype(BF16)
        k_ref[:, sl] = (kn[:, sl] + kpe).astype(BF16)
    v_ref[...] = _dot(ckv_b, wuv_ref[...]).astype(BF16)


def _mla_prep(proj, tabs, ew, per_batch):
    rows = proj.shape[0]
    c, sp, sm = tabs

    def tab_idx(i):
        return (0 if per_batch is None else 1 + i % per_batch, 0)

    tab_spec = pl.BlockSpec((TM, LANE), tab_idx)
    full = lambda a: pl.BlockSpec(a.shape, lambda i: (0,) * a.ndim)
    return pl.pallas_call(
        _mla_prep_kernel,
        grid=(rows // TM,),
        in_specs=[pl.BlockSpec((TM, 512), lambda i: (i, 0)), tab_spec, tab_spec, tab_spec,
                  full(ew["q_norm"]), full(ew["kv_norm"]), full(ew["w_uq"]), full(ew["w_uk"]), full(ew["w_uv"])],
        out_specs=[pl.BlockSpec((TM, MLA_HEADS * MLA_HP), lambda i: (i, 0)),
                   pl.BlockSpec((TM, MLA_HEADS * MLA_HP), lambda i: (i, 0)),
                   pl.BlockSpec((TM, MLA_HEADS * MLA_V), lambda i: (i, 0)),
                   pl.BlockSpec((TM, MLA_KVR), lambda i: (i, 0)),
                   pl.BlockSpec((TM, LANE), lambda i: (i, 0))],
        out_shape=[jax.ShapeDtypeStruct((rows, MLA_HEADS * MLA_HP), BF16),
                   jax.ShapeDtypeStruct((rows, MLA_HEADS * MLA_HP), BF16),
                   jax.ShapeDtypeStruct((rows, MLA_HEADS * MLA_V), BF16),
                   jax.ShapeDtypeStruct((rows, MLA_KVR), F32),
                   jax.ShapeDtypeStruct((rows, LANE), F32)],
        compiler_params=_cparams(1),
        name="mla_prep",
    )(proj, c, sp, sm, ew["q_norm"], ew["kv_norm"], ew["w_uq"], ew["w_uk"], ew["w_uv"])


def _mla_ctx_kernel(ckv_ref, kpe_ref, wuk_ref, wuv_ref, k_ref, v_ref):
    ckv_b = ckv_ref[...].astype(BF16)
    kn = _dot(ckv_b, wuk_ref[...])
    kpe = kpe_ref[...]
    for h in range(MLA_HEADS):
        sl = slice(MLA_HP * h, MLA_HP * (h + 1))
        k_ref[:, sl] = (kn[:, sl] + kpe).astype(BF16)
    v_ref[...] = _dot(ckv_b, wuv_ref[...]).astype(BF16)


def _mla_ctx(ckv, kpe_pad, ew):
    rows = ckv.shape[0]
    full = lambda a: pl.BlockSpec(a.shape, lambda i: (0,) * a.ndim)
    return pl.pallas_call(
        _mla_ctx_kernel,
        grid=(rows // TM,),
        in_specs=[pl.BlockSpec((TM, MLA_KVR), lambda i: (i, 0)), pl.BlockSpec((TM, LANE), lambda i: (i, 0)),
                  full(ew["w_uk"]), full(ew["w_uv"])],
        out_specs=[pl.BlockSpec((TM, MLA_HEADS * MLA_HP), lambda i: (i, 0)),
                   pl.BlockSpec((TM, MLA_HEADS * MLA_V), lambda i: (i, 0))],
        out_shape=[jax.ShapeDtypeStruct((rows, MLA_HEADS * MLA_HP), BF16),
                   jax.ShapeDtypeStruct((rows, MLA_HEADS * MLA_V), BF16)],
        compiler_params=_cparams(1),
        name="mla_ctx",
    )(ckv, kpe_pad, ew["w_uk"], ew["w_uv"])


def _attn_kernel(*refs, n_heads, group, dqk, dv, scale, has_ctx, has_sink, window, tq, seq):
    refs = list(refs)
    sink_ref = refs.pop(0) if has_sink else None
    q_ref, k_ref, v_ref = refs[:3]
    kc_ref, vc_ref = (refs[3], refs[4]) if has_ctx else (None, None)
    o_ref = refs[-1]
    i = pl.program_id(1)
    if window:
        kw = tq + 2 * window
        start = pl.multiple_of(jnp.clip(i * tq - window, 0, seq - kw), LANE)
        qpos = i * tq + lax.broadcasted_iota(jnp.int32, (tq, kw), 0)
        kpos = start + lax.broadcasted_iota(jnp.int32, (tq, kw), 1)
        valid = jnp.abs(qpos - kpos) <= window
    for h in range(n_heads):
        hk = h // group
        q = q_ref[:, h * dqk:(h + 1) * dqk]
        if window:
            k = k_ref[pl.ds(start, kw), hk * dqk:(hk + 1) * dqk]
            v = v_ref[pl.ds(start, kw), hk * dv:(hk + 1) * dv]
            s = jnp.where(valid, _dot_nt(q, k) * scale, NEG_INF)
        else:
            k = k_ref[:, hk * dqk:(hk + 1) * dqk]
            v = v_ref[:, hk * dv:(hk + 1) * dv]
            s = _dot_nt(q, k) * scale
        m = jnp.max(s, axis=-1, keepdims=True)
        if has_ctx:
            sc = _dot_nt(q, kc_ref[:, hk * dqk:(hk + 1) * dqk]) * scale
            m = jnp.maximum(m, jnp.max(sc, axis=-1, keepdims=True))
        if has_sink:
            sk = sink_ref[h]
            m = jnp.maximum(m, sk)
        p = jnp.exp(s - m)
        l = jnp.sum(p, axis=-1, keepdims=True)
        o = _dot(p.astype(BF16), v)
        if has_ctx:
            pc = jnp.exp(sc - m)
            l = l + jnp.sum(pc, axis=-1, keepdims=True)
            o = o + _dot(pc.astype(BF16), vc_ref[:, hk * dv:(hk + 1) * dv])
        if has_sink:
            l = l + jnp.exp(sk - m)
        o_ref[:, h * dv:(h + 1) * dv] = (o / l).astype(o_ref.dtype)


def _attention(q, k, v, ctx, sink, *, n_batch, seq, n_heads, group, dqk, dv, scale, window):
    tq = TM
    nq = seq // tq
    n_kv = n_heads // group
    in_specs = []
    args = []
    if sink is not None:
        in_specs.append(pl.BlockSpec(memory_space=pltpu.SMEM))
        args.append(sink)
    in_specs += [pl.BlockSpec((tq, n_heads * dqk), lambda b, i: (b * nq + i, 0)),
                 pl.BlockSpec((seq, n_kv * dqk), lambda b, i: (b, 0)),
                 pl.BlockSpec((seq, n_kv * dv), lambda b, i: (b, 0))]
    args += [q, k, v]
    if ctx is not None:
        kc, vc = ctx
        sc = kc.shape[0] // n_batch
        in_specs += [pl.BlockSpec((sc, n_kv * dqk), lambda b, i: (b, 0)),
                     pl.BlockSpec((sc, n_kv * dv), lambda b, i: (b, 0))]
        args += [kc, vc]
    kern = functools.partial(_attn_kernel, n_heads=n_heads, group=group, dqk=dqk, dv=dv, scale=scale,
                             has_ctx=ctx is not None, has_sink=sink is not None, window=window, tq=tq, seq=seq)
    return pl.pallas_call(
        kern,
        grid=(n_batch, nq),
        in_specs=in_specs,
        out_specs=pl.BlockSpec((tq, n_heads * dv), lambda b, i: (b * nq + i, 0)),
        out_shape=jax.ShapeDtypeStruct((n_batch * seq, n_heads * dv), BF16),
        compiler_params=_cparams(2, 40 * 1024 * 1024),
        name="attention",
    )(*args)


def _ret_kernel(*refs, seq, has_init, out_state):
    refs = list(refs)
    df_ref, db_ref, rq_ref, rk_ref, rv_ref, rg_ref, gn_ref = refs[:7]
    pos = 7
    if has_init:
        s0f_ref, s0b_ref = refs[pos], refs[pos + 1]
        pos += 2
    o_ref = refs[pos]
    pos += 1
    if out_state:
        sf_ref, sb_ref = refs[pos], refs[pos + 1]
        pos += 2
    of_scr, ob_scr = refs[pos], refs[pos + 1]

    pair = pl.program_id(1)
    n_chunks = seq // RET_CHUNK
    idx_c = lax.broadcasted_iota(jnp.int32, (RET_CHUNK, 1), 0).astype(F32)
    diff = (lax.broadcasted_iota(jnp.int32, (RET_CHUNK, RET_CHUNK), 0)
            - lax.broadcasted_iota(jnp.int32, (RET_CHUNK, RET_CHUNK), 1)).astype(F32)
    for hh in range(2):
        h = 2 * pair + hh
        hs = slice(RET_HD * hh, RET_HD * (hh + 1))
        rows = [slice(RET_CHUNK * ci, RET_CHUNK * (ci + 1)) for ci in range(n_chunks)]
        qs = [rq_ref[r, hs].astype(BF16) for r in rows]
        ks = [rk_ref[r, hs] * (RET_HD ** -0.5) for r in rows]
        vs = [rv_ref[r, hs].astype(BF16) for r in rows]
        qk = [_dot_nt(q, k.astype(BF16)) for q, k in zip(qs, ks)]
        for fwd in (True, False):
            d = jnp.full((1, 1), (df_ref if fwd else db_ref)[h], F32)
            lg = jnp.minimum(d, 0.0) - jnp.log1p(jnp.exp(-jnp.abs(d)))
            dd = diff if fwd else -diff
            mask = jnp.where(dd >= 0, jnp.exp(lg * jnp.maximum(dd, 0.0)), 0.0)
            if fwd:
                q_dec = jnp.exp(lg * (idx_c + 1.0))
                k_dec = jnp.exp(lg * (RET_CHUNK - 1.0 - idx_c))
            else:
                q_dec = jnp.exp(lg * (RET_CHUNK - idx_c))
                k_dec = jnp.exp(lg * idx_c)
            c_dec = jnp.exp(lg * RET_CHUNK)
            scr = of_scr if fwd else ob_scr
            if has_init:
                state0 = (s0f_ref if fwd else s0b_ref)[0, hh]
            else:
                state0 = jnp.zeros((RET_HD, RET_HD), F32)

            state = state0
            for ci in (range(n_chunks) if fwd else reversed(range(n_chunks))):
                o = _dot((qk[ci] * mask).astype(BF16), vs[ci]) + _dot(qs[ci], state.astype(BF16)) * q_dec
                scr[rows[ci], hs] = o
                state = state * c_dec + _dot_tn((ks[ci] * k_dec).astype(BF16), vs[ci])
            if out_state:
                (sf_ref if fwd else sb_ref)[0, hh] = state

    def head_norm(x):
        mu = jnp.mean(x, axis=-1, keepdims=True)
        xc = x - mu
        return xc * lax.rsqrt(jnp.mean(xc * xc, axis=-1, keepdims=True) + LN_EPS)

    for hh in range(2):
        hs = slice(RET_HD * hh, RET_HD * (hh + 1))
        o = head_norm(of_scr[:, hs]) + head_norm(ob_scr[:, hs])
        o_ref[:, hs] = (o * gn_ref[:, hs] * jax.nn.silu(rg_ref[:, hs])).astype(o_ref.dtype)


def _retention(proj, dec_f, dec_b, gn, init, *, n_batch, seq, out_state):
    pairs = RET_HEADS // 2
    col0 = 512 // LANE

    def col_spec(k):
        return pl.BlockSpec((seq, LANE), lambda b, p: (b, col0 + k * pairs + p))

    smem = pl.BlockSpec(memory_space=pltpu.SMEM)
    st_spec = pl.BlockSpec((1, 2, RET_HD, RET_HD), lambda b, p: (b, p, 0, 0))
    in_specs = [smem, smem, col_spec(0), col_spec(1), col_spec(2), col_spec(3),
                pl.BlockSpec((1, LANE), lambda b, p: (0, p))]
    args = [dec_f, dec_b, proj, proj, proj, proj, gn]
    if init is not None:
        in_specs += [st_spec, st_spec]
        args += list(init)
    out_specs = [pl.BlockSpec((seq, LANE), lambda b, p: (b, p))]
    out_shape = [jax.ShapeDtypeStruct((n_batch * seq, RET_W), BF16)]
    if out_state:
        out_specs += [st_spec, st_spec]
        out_shape += [jax.ShapeDtypeStruct((n_batch, RET_HEADS, RET_HD, RET_HD), F32)] * 2
    kern = functools.partial(_ret_kernel, seq=seq, has_init=init is not None, out_state=out_state)
    return pl.pallas_call(
        kern,
        grid=(n_batch, pairs),
        in_specs=in_specs,
        out_specs=out_specs,
        out_shape=out_shape,
        scratch_shapes=[pltpu.VMEM((seq, LANE), F32), pltpu.VMEM((seq, LANE), F32)],
        compiler_params=_cparams(2),
        name="retention",
    )(*args)


def _swa_prep_kernel(p_ref, c_ref, sp_ref, sm_ref, q_ref, k_ref, v_ref):
    c, sp, sm = c_ref[...], sp_ref[...], sm_ref[...]
    for j in range(SWA_Q // LANE):
        sl = slice(LANE * j, LANE * (j + 1))
        q_ref[:, sl] = _rope(p_ref[:, sl], c, sp, sm, SWA_HD // 4).astype(BF16)
    for j in range(SWA_KV // LANE):
        sl = slice(LANE * j, LANE * (j + 1))
        k_ref[:, sl] = _rope(p_ref[:, SWA_Q + LANE * j:SWA_Q + LANE * (j + 1)], c, sp, sm, SWA_HD // 4).astype(BF16)
    v_ref[...] = p_ref[:, SWA_Q + SWA_KV:].astype(BF16)


def _swa_prep(proj, tabs, per_batch):
    rows = proj.shape[0]
    c, sp, sm = tabs

    def tab_idx(i):
        return (0 if per_batch is None else 1 + i % per_batch, 0)

    tab_spec = pl.BlockSpec((TM, LANE), tab_idx)
    return pl.pallas_call(
        _swa_prep_kernel,
        grid=(rows // TM,),
        in_specs=[pl.BlockSpec((TM, SWA_Q + 2 * SWA_KV), lambda i: (i, 0)), tab_spec, tab_spec, tab_spec],
        out_specs=[pl.BlockSpec((TM, SWA_Q), lambda i: (i, 0)),
                   pl.BlockSpec((TM, SWA_KV), lambda i: (i, 0)),
                   pl.BlockSpec((TM, SWA_KV), lambda i: (i, 0))],
        out_shape=[jax.ShapeDtypeStruct((rows, SWA_Q), BF16),
                   jax.ShapeDtypeStruct((rows, SWA_KV), BF16),
                   jax.ShapeDtypeStruct((rows, SWA_KV), BF16)],
        compiler_params=_cparams(1),
        name="swa_prep",
    )(proj, c, sp, sm)


def _to_row_tiles(ref, x):
    rows = x.shape[0]
    for s in range(D // LANE):
        ref[pl.ds(s, rows, stride=SUB), :] = x[:, LANE * s:LANE * (s + 1)]


def _from_row_tiles(ref, rows):
    return jnp.concatenate([ref[pl.ds(s, rows, stride=SUB), :] for s in range(D // LANE)], axis=1)


def _post_kernel(*refs, n_parts):
    a_refs = refs[:n_parts]
    w_ref, x_ref, m_ref, g_ref, b_ref, rh_ref, rl_ref, x1_ref, h2_ref, lg_ref = refs[n_parts:]
    out = None
    off = 0
    for a_ref in a_refs:
        kk = a_ref.shape[1]
        part = _dot(a_ref[...], w_ref[off:off + kk, :])
        out = part if out is None else out + part
        off += kk
    z = ALPHA * x_ref[...] + m_ref[0, 2:3, :] * out
    x1 = _layernorm(z, g_ref[...], b_ref[...])
    x1_ref[...] = x1
    h2 = x1 * (1.0 + m_ref[0, 4:5, :]) + m_ref[0, 3:4, :]
    _to_row_tiles(h2_ref, h2)
    hi = h2.astype(BF16)
    lo = (h2 - hi.astype(F32)).astype(BF16)
    rh = rh_ref[...]
    lg_ref[...] = _dot(hi, rh) + _dot(lo, rh) + _dot(hi, rl_ref[...])


def _post_mixer(parts, w_out, x, mod, ln_g, ln_b, r_hi, r_lo, per_batch):
    rows = x.shape[0]
    full = lambda a: pl.BlockSpec(a.shape, lambda i: (0,) * a.ndim)
    in_specs = [pl.BlockSpec((TM, a.shape[1]), lambda i: (i, 0)) for a in parts]
    in_specs += [full(w_out), pl.BlockSpec((TM, D), lambda i: (i, 0)),
                 pl.BlockSpec((1, 6, D), lambda i: (_mod_row(i, per_batch), 0, 0)),
                 full(ln_g), full(ln_b), full(r_hi), full(r_lo)]
    return pl.pallas_call(
        functools.partial(_post_kernel, n_parts=len(parts)),
        grid=(rows // TM,),
        in_specs=in_specs,
        out_specs=[pl.BlockSpec((TM, D), lambda i: (i, 0)),
                   pl.BlockSpec((TM * SUB, LANE), lambda i: (i, 0)),
                   pl.BlockSpec((TM, LANE), lambda i: (i, 0))],
        out_shape=[jax.ShapeDtypeStruct((rows, D), F32),
                   jax.ShapeDtypeStruct((rows * SUB, LANE), F32),
                   jax.ShapeDtypeStruct((rows, LANE), F32)],
        compiler_params=_cparams(1),
        name="post_mixer",
    )(*parts, w_out, x, mod, ln_g, ln_b, r_hi, r_lo)


def _route_kernel(lg_ref, bias_ref, idx_ref, rank_ref, w_ref, cnt_ref, carry_ref):
    i = pl.program_id(0)

    @pl.when(i == 0)
    def _():
        carry_ref[...] = jnp.zeros_like(carry_ref)

    t = lg_ref.shape[0]
    gsz = N_EXP // N_GROUPS
    scores = jax.nn.sigmoid(lg_ref[...].T[:N_EXP])
    sel = scores + bias_ref[...]
    g3 = sel.reshape(N_GROUPS, gsz, t)
    sub_iota = lax.broadcasted_iota(jnp.int32, g3.shape, 1)
    m1 = jnp.max(g3, axis=1)
    first = jnp.min(jnp.where(g3 == m1[:, None, :], sub_iota, gsz), axis=1)
    m2 = jnp.max(jnp.where(sub_iota == first[:, None, :], -jnp.inf, g3), axis=1)
    grp = m1 + m2
    g_iota = lax.broadcasted_iota(jnp.int32, grp.shape, 0)
    gmask = jnp.zeros(grp.shape, jnp.bool_)
    for _ in range(TOPK_GROUPS):
        gm = jnp.max(grp, axis=0, keepdims=True)
        gi = jnp.min(jnp.where(grp == gm, g_iota, N_GROUPS), axis=0, keepdims=True)
        hit = g_iota == gi
        gmask = jnp.logical_or(gmask, hit)
        grp = jnp.where(hit, -jnp.inf, grp)
    emask = jnp.broadcast_to(gmask[:, None, :], g3.shape).reshape(N_EXP, t)
    cur = jnp.where(emask, sel, NEG_INF)
    e_iota = lax.broadcasted_iota(jnp.int32, cur.shape, 0)
    hits = []
    member = jnp.zeros(cur.shape, F32)
    for _ in range(TOP_K):
        cm = jnp.max(cur, axis=0, keepdims=True)
        ci = jnp.min(jnp.where(cur == cm, e_iota, N_EXP), axis=0, keepdims=True)
        hit = e_iota == ci
        hits.append((hit, ci))
        member = member + hit.astype(F32)
        cur = jnp.where(hit, -jnp.inf, cur)
    tri = (lax.broadcasted_iota(jnp.int32, (t, t), 0) < lax.broadcasted_iota(jnp.int32, (t, t), 1)).astype(BF16)
    before = _dot(member.astype(BF16), tri) + carry_ref[:, 0:1]
    ws = [jnp.sum(jnp.where(hit, scores, 0.0), axis=0, keepdims=True) for hit, _ in hits]
    wsum = ws[0]
    for w in ws[1:]:
        wsum = wsum + w
    for k, (hit, ci) in enumerate(hits):
        idx_ref[k:k + 1, :] = ci
        rank_ref[k:k + 1, :] = jnp.sum(jnp.where(hit, before, 0.0), axis=0, keepdims=True).astype(jnp.int32)
        w_ref[k:k + 1, :] = ws[k] / wsum * ROUTED_SCALE
    total = carry_ref[...] + jnp.sum(member, axis=1, keepdims=True)
    carry_ref[...] = total
    cnt_ref[...] = total.astype(jnp.int32)


def _route(logits, bias_col):
    rows = logits.shape[0]
    row_spec = pl.BlockSpec((TOP_K, TM), lambda i: (0, i))
    return pl.pallas_call(
        _route_kernel,
        grid=(rows // TM,),
        in_specs=[pl.BlockSpec((TM, LANE), lambda i: (i, 0)), pl.BlockSpec((N_EXP, 1), lambda i: (0, 0))],
        out_specs=[row_spec, row_spec, row_spec, pl.BlockSpec((N_EXP, LANE), lambda i: (0, 0))],
        out_shape=[jax.ShapeDtypeStruct((TOP_K, rows), jnp.int32),
                   jax.ShapeDtypeStruct((TOP_K, rows), jnp.int32),
                   jax.ShapeDtypeStruct((TOP_K, rows), F32),
                   jax.ShapeDtypeStruct((N_EXP, LANE), jnp.int32)],
        scratch_shapes=[pltpu.VMEM((N_EXP, LANE), F32)],
        compiler_params=_cparams(1),
        name="route",
    )(logits, bias_col)


EXP_LAG = 2
EXP_LEAD = (EXP_LAG + 1) * EXP_TM


def _experts_kernel(exp_ref, used_ref, dest_ref, seg_ref, cnt_ref, w_ref,
                    x_ref, wg_ref, wu_ref, wd_ref, y_ref,
                    xbuf0, xbuf1, obuf0, obuf1, wg_b0, wg_b1, wu_b0, wu_b1, asg, *, n_tok):
    s = pl.program_id(0)
    used = used_ref[0]
    n_assign = dest_ref.shape[0]
    pad_id = n_assign

    @pl.when(s == 0)
    def _():
        y_ref[...] = jnp.zeros_like(y_ref)
        for buf in (xbuf0, xbuf1, obuf0, obuf1, wg_b0, wg_b1, wu_b0, wu_b1):
            buf[...] = jnp.zeros_like(buf)

        def lead(j, c):
            for u in range(SUB):
                asg[j * SUB + u] = pad_id
            return c
        lax.fori_loop(0, EXP_LEAD // SUB, lead, 0)

        def fill(e, c):
            @pl.when(cnt_ref[e] > 0)
            def _():
                last = asg.at[pl.ds(pl.multiple_of(seg_ref[e + 1] - EXP_TM, EXP_TM), EXP_TM)]
                for u in range(EXP_TM):
                    last[u] = pad_id
            return c
        lax.fori_loop(0, N_EXP, fill, 0)

        def invert(j, c):
            for u in range(16):
                a = j * 16 + u
                asg[dest_ref[a]] = a
            return c
        lax.fori_loop(0, n_assign // 16, invert, 0)

    def stages(x_gather, x_mm, o_mm, o_scatter, wg_mm, wu_mm, wg_next, wu_next):
        g_slots = asg.at[pl.ds(jnp.minimum(s + 2, used + EXP_LAG) * EXP_TM, EXP_TM)]
        for r in range(EXP_TM):
            row0 = g_slots[r] & ((n_tok - 1) * SUB)
            x_gather[r * SUB:(r + 1) * SUB, :] = x_ref[pl.ds(pl.multiple_of(row0, SUB), SUB), :]

        x = _from_row_tiles(x_mm, EXP_TM).astype(BF16)
        h = (jax.nn.silu(_dot(x, wg_mm[...])) * _dot(x, wu_mm[...])).astype(BF16)
        _to_row_tiles(o_mm, _dot(h, wd_ref[...].astype(BF16)))
        wg_next[...] = wg_ref[...].astype(BF16)
        wu_next[...] = wu_ref[...].astype(BF16)

        s_slots = asg.at[pl.ds(s * EXP_TM, EXP_TM)]
        for j in range(EXP_TM // SUB):
            vals = []
            for u in range(SUB):
                r = j * SUB + u
                a = s_slots[r]
                rows = pl.ds(pl.multiple_of(a & -SUB, SUB), SUB)
                vals.append((rows, y_ref[rows, :] + w_ref[a] * o_scatter[r * SUB:(r + 1) * SUB, :]))
            for rows, v in vals:
                y_ref[rows, :] = v

    @pl.when(s < used + EXP_LAG + 1)
    def _():
        @pl.when(s % 2 == 0)
        def _():
            stages(xbuf0, xbuf1, obuf1, obuf0, wg_b0, wu_b0, wg_b1, wu_b1)

        @pl.when(s % 2 == 1)
        def _():
            stages(xbuf1, xbuf0, obuf0, obuf1, wg_b1, wu_b1, wg_b0, wu_b0)


def _experts(tile_exp, used, dest, seg, cnt, w_flat, h2_tiles, wg, wu, wd, layer):
    n_tok = h2_tiles.shape[0] // SUB
    assert TOP_K == SUB and dest.shape[0] == n_tok * TOP_K and w_flat.shape[0] == dest.shape[0] + SUB
    n_tiles = tile_exp.shape[0]

    def wmap(ahead):
        return lambda s, exp, used, *_: (layer, exp[jnp.clip(s - EXP_LAG + ahead, 0, used[0] - 1)], 0, 0)

    buf = pltpu.VMEM((EXP_TM * SUB, LANE), F32)
    gs = pltpu.PrefetchScalarGridSpec(
        num_scalar_prefetch=6,
        grid=(n_tiles + EXP_LAG + 1,),
        in_specs=[pl.BlockSpec(memory_space=pltpu.VMEM),
                  pl.BlockSpec((None, None, D, EXP_D), wmap(1)),
                  pl.BlockSpec((None, None, D, EXP_D), wmap(1)),
                  pl.BlockSpec((None, None, EXP_D, D), wmap(0))],
        out_specs=pl.BlockSpec(memory_space=pltpu.VMEM),
        scratch_shapes=[buf, buf, buf, buf] + [pltpu.VMEM((D, EXP_D), BF16)] * 4 + [
                        pltpu.SMEM((EXP_LEAD + n_tiles * EXP_TM,), jnp.int32)],
    )
    return pl.pallas_call(
        functools.partial(_experts_kernel, n_tok=n_tok),
        grid_spec=gs,
        out_shape=jax.ShapeDtypeStruct(((n_tok + 1) * SUB, LANE), F32),
        compiler_params=_cparams(1, VMEM_BIG),
        name="experts",
    )(tile_exp, used, dest, seg, cnt, w_flat, h2_tiles, wg, wu, wd)


def _dispatch_plan(idx, rank, w, cnt):
    n_tok = idx.shape[1]
    n_tiles = n_tok * TOP_K // EXP_TM + N_EXP
    tiles_e = (cnt + EXP_TM - 1) // EXP_TM
    tile_end = jnp.cumsum(tiles_e)
    seg = (EXP_LEAD + jnp.concatenate([jnp.zeros((1,), jnp.int32), tile_end * EXP_TM])).astype(jnp.int32)
    e_ids = jnp.arange(N_EXP, dtype=jnp.int32)
    dest = jnp.sum(jnp.where(idx[:, :, None] == e_ids, seg[:N_EXP], 0), axis=-1) + rank
    ii = jnp.arange(n_tiles, dtype=jnp.int32)
    used = tile_end[-1]
    tile_exp = jnp.sum(tile_end[None, :] <= jnp.minimum(ii, used - 1)[:, None], axis=1).astype(jnp.int32)
    tile_exp = jnp.minimum(tile_exp, N_EXP - 1)
    w_flat = jnp.concatenate([w.T.reshape(-1), jnp.zeros((SUB,), F32)])
    return tile_exp, used.reshape(1).astype(jnp.int32), dest.T.reshape(-1), seg, w_flat


def _final_kernel(y_ref, x1_ref, m_ref, sg_ref, su_ref, sd_ref, g_ref, b_ref, o_ref):
    x1 = x1_ref[...]
    routed = _from_row_tiles(y_ref, TM)
    h2 = (x1 * (1.0 + m_ref[0, 4:5, :]) + m_ref[0, 3:4, :]).astype(BF16)
    act = (jax.nn.silu(_dot(h2, sg_ref[...])) * _dot(h2, su_ref[...])).astype(BF16)
    shared = _dot(act, sd_ref[...])
    z = ALPHA * x1 + m_ref[0, 5:6, :] * (routed + shared)
    o_ref[...] = _layernorm(z, g_ref[...], b_ref[...])


def _final(y_tiles, x1, mod, sg, su, sd, ln_g, ln_b, per_batch):
    rows = x1.shape[0]
    full = lambda a: pl.BlockSpec(a.shape, lambda i: (0,) * a.ndim)
    return pl.pallas_call(
        _final_kernel,
        grid=(rows // TM,),
        in_specs=[pl.BlockSpec((TM * SUB, LANE), lambda i: (i, 0)),
                  pl.BlockSpec((TM, D), lambda i: (i, 0)),
                  pl.BlockSpec((1, 6, D), lambda i: (_mod_row(i, per_batch), 0, 0)),
                  full(sg), full(su), full(sd), full(ln_g), full(ln_b)],
        out_specs=pl.BlockSpec((TM, D), lambda i: (i, 0)),
        out_shape=jax.ShapeDtypeStruct((rows, D), F32),
        compiler_params=_cparams(1),
        name="final",
    )(y_tiles, x1, mod, sg, su, sd, ln_g, ln_b)


def _rope_tables(rot_dim, lane_lo, n_rot, n_rows):
    nf = rot_dim // 4
    t = jnp.arange(n_rows)
    r = (t // GRID_W).astype(F32)
    col = (t % GRID_W).astype(F32)
    freqs = ROPE_BASE ** (-jnp.arange(nf, dtype=F32) / nf)
    lane = jnp.arange(LANE)
    j = (lane - lane_lo) % rot_dim
    in_rot = jnp.logical_and(lane >= lane_lo, lane < lane_lo + n_rot * rot_dim)
    half = j // (2 * nf)
    second = (j % (2 * nf)) >= nf
    f = freqs[j % nf]
    ang = jnp.where(half[None, :] == 0, r[:, None], col[:, None]) * f[None, :]
    cos = jnp.where(in_rot[None, :], jnp.cos(ang), 1.0)
    sin = jnp.where(in_rot[None, :], jnp.sin(ang), 0.0)
    sp = jnp.where(second[None, :], sin, 0.0)
    sm = jnp.where(second[None, :], 0.0, -sin)
    ident = lambda v: jnp.full((TM, LANE), v, F32)
    return (jnp.concatenate([ident(1.0), cos], axis=0), jnp.concatenate([ident(0.0), sp], axis=0),
            jnp.concatenate([ident(0.0), sm], axis=0))


def _even_weights(w_in, q_norm, w_uq, kv_norm, w_uk, w_uv, w_out):
    z = lambda n: jnp.zeros((D, n), F32)
    mla_in = MLA_QR + MLA_KVR + MLA_ROPE
    w_in_p = jnp.concatenate([w_in[:, :MLA_QR + MLA_KVR], z(MLA_NOPE), w_in[:, MLA_QR + MLA_KVR:mla_in],
                              z(MLA_HP - MLA_NOPE - MLA_ROPE), w_in[:, mla_in:]], axis=1)
    uq = w_uq.reshape(MLA_QR, MLA_HEADS, MLA_NOPE + MLA_ROPE)
    uq = jnp.pad(uq, ((0, 0), (0, 0), (0, MLA_HP - MLA_NOPE - MLA_ROPE))).reshape(MLA_QR, MLA_HEADS * MLA_HP)
    uk = w_uk.reshape(MLA_KVR, MLA_HEADS, MLA_NOPE)
    uk = jnp.pad(uk, ((0, 0), (0, 0), (0, MLA_HP - MLA_NOPE))).reshape(MLA_KVR, MLA_HEADS * MLA_HP)
    return {"w_in": w_in_p.astype(BF16), "q_norm": q_norm.reshape(1, -1), "kv_norm": kv_norm.reshape(1, -1),
            "w_uq": uq.astype(BF16), "w_uk": uk.astype(BF16), "w_uv": w_uv.astype(BF16), "w_out": w_out.astype(BF16)}


def kernel(x_prompt, x_sample, cache_mla_ckv, cache_mla_kpe, state_ret_fwd, state_ret_bwd, cache_swa_k, cache_swa_v, c, c_ctx, w_mod, b_mod, ln1_g, ln1_b, ln2_g, ln2_b, mla_ret_w_in, mla_q_norm, mla_w_uq, mla_kv_norm, mla_w_uk, mla_w_uv, ret_decay_fwd, ret_decay_bwd, ret_gn_g, even_w_out, swa_w_in, swa_sink, swa_w_out, moe_router, moe_router_bias, moe_w_gate, moe_w_up, moe_w_down, shared_w_gate, shared_w_up, shared_w_down):
    n_p, seq_p, _ = x_prompt.shape
    n_s, seq_s, _ = x_sample.shape
    past = cache_mla_ckv.shape[2]
    groups = [
        dict(x=x_prompt.reshape(n_p * seq_p, D), nb=n_p, seq=seq_p, per_batch=None),
        dict(x=x_sample.reshape(n_s * seq_s, D), nb=n_s, seq=seq_s, per_batch=seq_s // TM),
    ]
    cond8 = jnp.zeros((8, D), F32).at[0].set(c_ctx).at[1:1 + n_s].set(c)
    mods = _modulation(cond8, w_mod, b_mod).reshape(DEPTH, 8, 6, D)
    tabs_mla = _rope_tables(MLA_ROPE, MLA_NOPE, 1, seq_s)
    tabs_swa = _rope_tables(SWA_HD, 0, LANE // SWA_HD, seq_s)

    outs = {k: [] for k in ("ckv", "kpe", "rf", "rb", "sk", "sv")}
    for l in range(DEPTH):
        mod = mods[l]
        r_pad = jnp.pad(moe_router[l], ((0, 0), (0, LANE - N_EXP)))
        r_hi = r_pad.astype(BF16)
        r_lo = (r_pad - r_hi.astype(F32)).astype(BF16)
        bias_col = moe_router_bias[l].reshape(N_EXP, 1)
        sg, su, sd = (shared_w_gate[l].astype(BF16), shared_w_up[l].astype(BF16), shared_w_down[l].astype(BF16))
        if l % 2 == 0:
            e = l // 2
            ew = _even_weights(mla_ret_w_in[e], mla_q_norm[e], mla_w_uq[e], mla_kv_norm[e], mla_w_uk[e],
                               mla_w_uv[e], even_w_out[e])
            kpe_ctx = jnp.pad(cache_mla_kpe[:, e].reshape(n_s * past, MLA_ROPE),
                              ((0, 0), (MLA_NOPE, MLA_HP - MLA_NOPE - MLA_ROPE)))
            ctx_kv = _mla_ctx(cache_mla_ckv[:, e].reshape(n_s * past, MLA_KVR), kpe_ctx, ew)
            gn = ret_gn_g[e].reshape(1, RET_W)
        else:
            o = l // 2
            w_in_o = swa_w_in[o].astype(BF16)
            w_out_o = swa_w_out[o].astype(BF16)
            ctx_swa = (cache_swa_k[:, o].reshape(n_s * past, SWA_KV).astype(BF16),
                       cache_swa_v[:, o].reshape(n_s * past, SWA_KV).astype(BF16))
        for gi, g in enumerate(groups):
            x, nb, seq, per_batch = g["x"], g["nb"], g["seq"], g["per_batch"]
            is_sample = gi == 1
            if l % 2 == 0:
                proj = _modmm(x, mod, ew["w_in"], per_batch)
                q, k, v, ckv, kpe = _mla_prep(proj, tabs_mla, ew, per_batch)
                o_mla = _attention(q, k, v, ctx_kv if is_sample else None, None, n_batch=nb, seq=seq,
                                   n_heads=MLA_HEADS, group=1, dqk=MLA_HP, dv=MLA_V,
                                   scale=(MLA_NOPE + MLA_ROPE) ** -0.5, window=0)
                init = (state_ret_fwd[:, e], state_ret_bwd[:, e]) if is_sample else None
                ret = _retention(proj, ret_decay_fwd[e], ret_decay_bwd[e], gn, init, n_batch=nb, seq=seq,
                                 out_state=not is_sample)
                if is_sample:
                    o_ret = ret[0]
                else:
                    o_ret, s_f, s_b = ret
                    outs["ckv"].append(ckv.reshape(nb, seq, MLA_KVR))
                    outs["kpe"].append(kpe[:, MLA_NOPE:MLA_NOPE + MLA_ROPE].reshape(nb, seq, MLA_ROPE))
                    outs["rf"].append(s_f)
                    outs["rb"].append(s_b)
                parts, w_out = [o_mla, o_ret], ew["w_out"]
            else:
                proj = _modmm(x, mod, w_in_o, per_batch)
                q, k, v = _swa_prep(proj, tabs_swa, per_batch)
                o_swa = _attention(q, k, v, ctx_swa if is_sample else None, swa_sink[o], n_batch=nb, seq=seq,
                                   n_heads=SWA_HEADS, group=SWA_HEADS // SWA_KVH, dqk=SWA_HD, dv=SWA_HD,
                                   scale=SWA_HD ** -0.5, window=WINDOW if is_sample else 0)
                if not is_sample:
                    outs["sk"].append(proj[:, SWA_Q:SWA_Q + SWA_KV].reshape(nb, seq, SWA_KVH, SWA_HD))
                    outs["sv"].append(proj[:, SWA_Q + SWA_KV:].reshape(nb, seq, SWA_KVH, SWA_HD))
                parts, w_out = [o_swa], w_out_o
            x1, h2_tiles, logits = _post_mixer(parts, w_out, x, mod, ln1_g[l].reshape(1, D), ln1_b[l].reshape(1, D),
                                               r_hi, r_lo, per_batch)
            idx, rank, w, cnt = _route(logits, bias_col)
            tile_exp, used, dest, seg, w_flat = _dispatch_plan(idx, rank, w, cnt[:, 0])
            y_tiles = _experts(tile_exp, used, dest, seg, cnt[:, 0], w_flat, h2_tiles,
                               moe_w_gate, moe_w_up, moe_w_down, l)
            g["x"] = _final(y_tiles, x1, mod, sg, su, sd,
                            ln2_g[l].reshape(1, D), ln2_b[l].reshape(1, D), per_batch)
    y_prompt = groups[0]["x"].reshape(n_p, seq_p, D)
    y_sample = groups[1]["x"].reshape(n_s, seq_s, D)
    return (y_prompt, y_sample, jnp.stack(outs["ckv"], axis=1), jnp.stack(outs["kpe"], axis=1),
            jnp.stack(outs["rf"], axis=1), jnp.stack(outs["rb"], axis=1),
            jnp.stack(outs["sk"], axis=1), jnp.stack(outs["sv"], axis=1))
```

```python
import functools

import jax
import jax.numpy as jnp
from jax import lax
from jax.experimental import pallas as pl
from jax.experimental.pallas import tpu as pltpu

F32 = jnp.float32
BF16 = jnp.bfloat16

D = 1024
DEPTH = 4
GRID_W = 64
ALPHA = (2.0 * DEPTH) ** 0.25
LN_EPS = 1e-5
RMS_EPS = 1e-6
ROPE_BASE = 10000.0
NEG_INF = -1e30
MLA_HEADS = 8
MLA_NOPE = 64
MLA_ROPE = 32
MLA_V = 64
MLA_QR = 256
MLA_KVR = 128
MLA_HP = 128
RET_HEADS = 8
RET_HD = 64
RET_CHUNK = 256
RET_W = RET_HEADS * RET_HD
RET_HPS = 8
EVEN_P = 512 + 4 * RET_W
SWA_HEADS = 16
SWA_KVH = 4
SWA_HD = 64
WINDOW = 128
SWA_Q = SWA_HEADS * SWA_HD
SWA_KV = SWA_KVH * SWA_HD
SWA_SCALE = SWA_HD ** -0.5
assert SWA_SCALE == 0.125
N_EXP = 64
TOP_K = 8
N_GROUPS = 8
TOPK_GROUPS = 4
EXP_D = 256
ROUTED_SCALE = 2.5

TM = 256
SUB = 8
LANE = 128
EXP_TM = 256
VMEM_BIG = 56 * 1024 * 1024


def _cparams(n_axes, vmem=None):
    return pltpu.CompilerParams(dimension_semantics=("arbitrary",) * n_axes, vmem_limit_bytes=vmem)


def _dot(a, b):
    return jnp.dot(a, b, preferred_element_type=F32)


def _dot_nt(a, b):
    return lax.dot_general(a, b, (((1,), (1,)), ((), ())), preferred_element_type=F32)


def _dot_tn(a, b):
    return lax.dot_general(a, b, (((0,), (0,)), ((), ())), preferred_element_type=F32)


def _layernorm(z, g, b):
    mu = jnp.mean(z, axis=-1, keepdims=True)
    zc = z - mu
    var = jnp.mean(zc * zc, axis=-1, keepdims=True)
    return zc * lax.rsqrt(var + LN_EPS) * g + b


def _rmsnorm(x, g):
    return x * lax.rsqrt(jnp.mean(x * x, axis=-1, keepdims=True) + RMS_EPS) * g


def _rope(x, c, sp, sm, shift):
    w = x.shape[-1]
    return x * c + pltpu.roll(x, shift, 1) * sp + pltpu.roll(x, w - shift, 1) * sm


def _mod_kernel(c_ref, w_ref, b_ref, o_ref):
    s = jax.nn.silu(c_ref[...]).astype(BF16)
    o_ref[0] = _dot(s, w_ref[0].astype(BF16)) + b_ref[0]


def _modulation(cond8, w_mod, b_mod):
    nt = 4
    tn = 6 * D // nt
    return pl.pallas_call(
        _mod_kernel,
        grid=(DEPTH, nt),
        in_specs=[pl.BlockSpec((8, D), lambda l, j: (0, 0)),
                  pl.BlockSpec((1, D, tn), lambda l, j: (l, 0, j)),
                  pl.BlockSpec((1, 1, tn), lambda l, j: (l, 0, j))],
        out_specs=pl.BlockSpec((1, 8, tn), lambda l, j: (l, 0, j)),
        out_shape=jax.ShapeDtypeStruct((DEPTH, 8, 6 * D), F32),
        compiler_params=_cparams(2, 40 * 1024 * 1024),
        name="modulation",
    )(cond8, w_mod, b_mod.reshape(DEPTH, 1, 6 * D))


def _mod_row(i, per_batch):
    return 0 if per_batch is None else 1 + i // per_batch


def _modmm_kernel(x_ref, m_ref, w_ref, o_ref, *, shift_i, scale_i):
    h = x_ref[...] * (1.0 + m_ref[0, scale_i:scale_i + 1, :]) + m_ref[0, shift_i:shift_i + 1, :]
    o_ref[...] = _dot(h.astype(BF16), w_ref[...])


def _modmm(x, mod, w, per_batch):
    rows = x.shape[0]
    n = w.shape[1]
    return pl.pallas_call(
        functools.partial(_modmm_kernel, shift_i=0, scale_i=1),
        grid=(rows // TM,),
        in_specs=[pl.BlockSpec((TM, D), lambda i: (i, 0)),
                  pl.BlockSpec((1, 6, D), lambda i: (_mod_row(i, per_batch), 0, 0)),
                  pl.BlockSpec((D, n), lambda i: (0, 0))],
        out_specs=pl.BlockSpec((TM, n), lambda i: (i, 0)),
        out_shape=jax.ShapeDtypeStruct((rows, n), F32),
        compiler_params=_cparams(1, 40 * 1024 * 1024),
        name="modmm",
    )(x, mod, w)


def _mla_prep_kernel(p_ref, c_ref, sp_ref, sm_ref, qn_ref, kvn_ref, wuq_ref, wuk_ref, wuv_ref,
                     q_ref, k_ref, v_ref, ckv_ref, kpe_ref):
    p = p_ref[...]
    c, sp, sm = c_ref[...], sp_ref[...], sm_ref[...]
    qn = _rmsnorm(p[:, 0:MLA_QR], qn_ref[...])
    q = _dot(qn.astype(BF16), wuq_ref[...])
    ckv = _rmsnorm(p[:, MLA_QR:MLA_QR + MLA_KVR], kvn_ref[...])
    ckv_ref[...] = ckv
    kpe = _rope(p[:, MLA_QR + MLA_KVR:512], c, sp, sm, MLA_ROPE // 4)
    kpe_ref[...] = kpe
    ckv_b = ckv.astype(BF16)
    kn = _dot(ckv_b, wuk_ref[...])
    for h in range(MLA_HEADS):
        sl = slice(MLA_HP * h, MLA_HP * (h + 1))
        q_ref[:, sl] = _rope(q[:, sl], c, sp, sm, MLA_ROPE // 4).astype(BF16)
        k_ref[:, sl] = (kn[:, sl] + kpe).astype(BF16)
    v_ref[...] = _dot(ckv_b, wuv_ref[...]).astype(BF16)


def _mla_prep(proj, tabs, ew, per_batch):
    rows = proj.shape[0]
    c, sp, sm = tabs

    def tab_idx(i):
        return (0 if per_batch is None else 1 + i % per_batch, 0)

    tab_spec = pl.BlockSpec((TM, LANE), tab_idx)
    full = lambda a: pl.BlockSpec(a.shape, lambda i: (0,) * a.ndim)
    return pl.pallas_call(
        _mla_prep_kernel,
        grid=(rows // TM,),
        in_specs=[pl.BlockSpec((TM, 512), lambda i: (i, 0)), tab_spec, tab_spec, tab_spec,
                  full(ew["q_norm"]), full(ew["kv_norm"]), full(ew["w_uq"]), full(ew["w_uk"]), full(ew["w_uv"])],
        out_specs=[pl.BlockSpec((TM, MLA_HEADS * MLA_HP), lambda i: (i, 0)),
                   pl.BlockSpec((TM, MLA_HEADS * MLA_HP), lambda i: (i, 0)),
                   pl.BlockSpec((TM, MLA_HEADS * MLA_V), lambda i: (i, 0)),
                   pl.BlockSpec((TM, MLA_KVR), lambda i: (i, 0)),
                   pl.BlockSpec((TM, LANE), lambda i: (i, 0))],
        out_shape=[jax.ShapeDtypeStruct((rows, MLA_HEADS * MLA_HP), BF16),
                   jax.ShapeDtypeStruct((rows, MLA_HEADS * MLA_HP), BF16),
                   jax.ShapeDtypeStruct((rows, MLA_HEADS * MLA_V), BF16),
                   jax.ShapeDtypeStruct((rows, MLA_KVR), F32),
                   jax.ShapeDtypeStruct((rows, LANE), F32)],
        compiler_params=_cparams(1),
        name="mla_prep",
    )(proj, c, sp, sm, ew["q_norm"], ew["kv_norm"], ew["w_uq"], ew["w_uk"], ew["w_uv"])


def _mla_ctx_kernel(ckv_ref, kpe_ref, wuk_ref, wuv_ref, k_ref, v_ref):
    ckv_b = ckv_ref[...].astype(BF16)
    kn = _dot(ckv_b, wuk_ref[...])
    kpe = kpe_ref[...]
    for h in range(MLA_HEADS):
        sl = slice(MLA_HP * h, MLA_HP * (h + 1))
        k_ref[:, sl] = (kn[:, sl] + kpe).astype(BF16)
    v_ref[...] = _dot(ckv_b, wuv_ref[...]).astype(BF16)


def _mla_ctx(ckv, kpe_pad, ew):
    rows = ckv.shape[0]
    full = lambda a: pl.BlockSpec(a.shape, lambda i: (0,) * a.ndim)
    return pl.pallas_call(
        _mla_ctx_kernel,
        grid=(rows // TM,),
        in_specs=[pl.BlockSpec((TM, MLA_KVR), lambda i: (i, 0)), pl.BlockSpec((TM, LANE), lambda i: (i, 0)),
                  full(ew["w_uk"]), full(ew["w_uv"])],
        out_specs=[pl.BlockSpec((TM, MLA_HEADS * MLA_HP), lambda i: (i, 0)),
                   pl.BlockSpec((TM, MLA_HEADS * MLA_V), lambda i: (i, 0))],
        out_shape=[jax.ShapeDtypeStruct((rows, MLA_HEADS * MLA_HP), BF16),
                   jax.ShapeDtypeStruct((rows, MLA_HEADS * MLA_V), BF16)],
        compiler_params=_cparams(1),
        name="mla_ctx",
    )(ckv, kpe_pad, ew["w_uk"], ew["w_uv"])


def _attn_kernel(*refs, n_heads, group, dqk, dv, scale, has_ctx, has_sink, window, tq, seq):
    refs = list(refs)
    sink_ref = refs.pop(0) if has_sink else None
    q_ref, k_ref, v_ref = refs[:3]
    kc_ref, vc_ref = (refs[3], refs[4]) if has_ctx else (None, None)
    o_ref = refs[-1]
    i = pl.program_id(1)
    if window:
        kw = tq + 2 * window
        start = pl.multiple_of(jnp.clip(i * tq - window, 0, seq - kw), LANE)
        qpos = i * tq + lax.broadcasted_iota(jnp.int32, (tq, kw), 0)
        kpos = start + lax.broadcasted_iota(jnp.int32, (tq, kw), 1)
        valid = jnp.abs(qpos - kpos) <= window
    def scores(q, k):
        s = _dot_nt(q, k)
        return s if scale is None else s * scale

    for h in range(n_heads):
        hk = h // group
        q = q_ref[:, h * dqk:(h + 1) * dqk]
        if window:
            k = k_ref[pl.ds(start, kw), hk * dqk:(hk + 1) * dqk]
            v = v_ref[pl.ds(start, kw), hk * dv:(hk + 1) * dv]
            s = jnp.where(valid, scores(q, k), NEG_INF)
        else:
            k = k_ref[:, hk * dqk:(hk + 1) * dqk]
            v = v_ref[:, hk * dv:(hk + 1) * dv]
            s = scores(q, k)
        m = jnp.max(s, axis=-1, keepdims=True)
        if has_ctx:
            sc = scores(q, kc_ref[:, hk * dqk:(hk + 1) * dqk])
            m = jnp.maximum(m, jnp.max(sc, axis=-1, keepdims=True))
        if has_sink:
            sk = sink_ref[h]
            m = jnp.maximum(m, sk)
        p = jnp.exp(s - m)
        l = jnp.sum(p, axis=-1, keepdims=True)
        o = _dot(p.astype(BF16), v)
        if has_ctx:
            pc = jnp.exp(sc - m)
            l = l + jnp.sum(pc, axis=-1, keepdims=True)
            o = o + _dot(pc.astype(BF16), vc_ref[:, hk * dv:(hk + 1) * dv])
        if has_sink:
            l = l + jnp.exp(sk - m)
        o_ref[:, h * dv:(h + 1) * dv] = (o / l).astype(o_ref.dtype)


def _attention(q, k, v, ctx, sink, *, n_batch, seq, n_heads, group, dqk, dv, scale, window):
    tq = TM
    nq = seq // tq
    n_kv = n_heads // group
    in_specs = []
    args = []
    if sink is not None:
        in_specs.append(pl.BlockSpec(memory_space=pltpu.SMEM))
        args.append(sink)
    in_specs += [pl.BlockSpec((tq, n_heads * dqk), lambda b, i: (b * nq + i, 0)),
                 pl.BlockSpec((seq, n_kv * dqk), lambda b, i: (b, 0)),
                 pl.BlockSpec((seq, n_kv * dv), lambda b, i: (b, 0))]
    args += [q, k, v]
    if ctx is not None:
        kc, vc = ctx
        sc = kc.shape[0] // n_batch
        in_specs += [pl.BlockSpec((sc, n_kv * dqk), lambda b, i: (b, 0)),
                     pl.BlockSpec((sc, n_kv * dv), lambda b, i: (b, 0))]
        args += [kc, vc]
    kern = functools.partial(_attn_kernel, n_heads=n_heads, group=group, dqk=dqk, dv=dv, scale=scale,
                             has_ctx=ctx is not None, has_sink=sink is not None, window=window, tq=tq, seq=seq)
    return pl.pallas_call(
        kern,
        grid=(n_batch, nq),
        in_specs=in_specs,
        out_specs=pl.BlockSpec((tq, n_heads * dv), lambda b, i: (b * nq + i, 0)),
        out_shape=jax.ShapeDtypeStruct((n_batch * seq, n_heads * dv), BF16),
        compiler_params=_cparams(2, 40 * 1024 * 1024),
        name="attention",
    )(*args)


def _ret_kernel(*refs, seq, has_init, out_state):
    refs = list(refs)
    df_ref, db_ref, rq_ref, rk_ref, rv_ref, rg_ref, gn_ref = refs[:7]
    pos = 7
    if has_init:
        s0f_ref, s0b_ref = refs[pos], refs[pos + 1]
        pos += 2
    o_ref = refs[pos]
    pos += 1
    if out_state:
        sf_ref, sb_ref = refs[pos], refs[pos + 1]
        pos += 2
    of_scr, ob_scr = refs[pos], refs[pos + 1]

    grp = pl.program_id(1)
    n_chunks = seq // RET_CHUNK
    idx_c = lax.broadcasted_iota(jnp.int32, (RET_CHUNK, 1), 0).astype(F32)
    diff = (lax.broadcasted_iota(jnp.int32, (RET_CHUNK, RET_CHUNK), 0)
            - lax.broadcasted_iota(jnp.int32, (RET_CHUNK, RET_CHUNK), 1)).astype(F32)
    for hh in range(RET_HPS):
        h = RET_HPS * grp + hh
        hs = slice(RET_HD * hh, RET_HD * (hh + 1))
        rows = [slice(RET_CHUNK * ci, RET_CHUNK * (ci + 1)) for ci in range(n_chunks)]
        qs = [rq_ref[r, hs].astype(BF16) for r in rows]
        ks = [rk_ref[r, hs] * (RET_HD ** -0.5) for r in rows]
        vs = [rv_ref[r, hs].astype(BF16) for r in rows]
        qk = [_dot_nt(q, k.astype(BF16)) for q, k in zip(qs, ks)]
        for fwd in (True, False):
            d = jnp.full((1, 1), (df_ref if fwd else db_ref)[h], F32)
            lg = jnp.minimum(d, 0.0) - jnp.log1p(jnp.exp(-jnp.abs(d)))
            dd = diff if fwd else -diff
            mask = jnp.where(dd >= 0, jnp.exp(lg * jnp.maximum(dd, 0.0)), 0.0)
            if fwd:
                q_dec = jnp.exp(lg * (idx_c + 1.0))
                k_dec = jnp.exp(lg * (RET_CHUNK - 1.0 - idx_c))
            else:
                q_dec = jnp.exp(lg * (RET_CHUNK - idx_c))
                k_dec = jnp.exp(lg * idx_c)
            c_dec = jnp.exp(lg * RET_CHUNK)
            scr = of_scr if fwd else ob_scr
            state = (s0f_ref if fwd else s0b_ref)[0, hh] if has_init else None
            for ci in (range(n_chunks) if fwd else reversed(range(n_chunks))):
                o = _dot((qk[ci] * mask).astype(BF16), vs[ci])
                kv = _dot_tn((ks[ci] * k_dec).astype(BF16), vs[ci])
                if state is not None:
                    o = o + _dot(qs[ci], state.astype(BF16)) * q_dec
                    kv = state * c_dec + kv
                scr[rows[ci], hs] = o
                state = kv
            if out_state:
                (sf_ref if fwd else sb_ref)[0, hh] = state

    def head_norm(x):
        mu = jnp.mean(x, axis=-1, keepdims=True)
        xc = x - mu
        return xc * lax.rsqrt(jnp.mean(xc * xc, axis=-1, keepdims=True) + LN_EPS)

    for hh in range(RET_HPS):
        hs = slice(RET_HD * hh, RET_HD * (hh + 1))
        o = head_norm(of_scr[:, hs]) + head_norm(ob_scr[:, hs])
        o_ref[:, hs] = (o * gn_ref[:, hs] * jax.nn.silu(rg_ref[:, hs])).astype(o_ref.dtype)


def _retention(proj, dec_f, dec_b, gn, init, *, n_batch, seq, out_state):
    groups = RET_HEADS // RET_HPS
    width = RET_HPS * RET_HD
    col0 = 512 // width

    def col_spec(k):
        return pl.BlockSpec((seq, width), lambda b, p: (b, col0 + k * groups + p))

    smem = pl.BlockSpec(memory_space=pltpu.SMEM)
    st_spec = pl.BlockSpec((1, RET_HPS, RET_HD, RET_HD), lambda b, p: (b, p, 0, 0))
    in_specs = [smem, smem, col_spec(0), col_spec(1), col_spec(2), col_spec(3),
                pl.BlockSpec((1, width), lambda b, p: (0, p))]
    args = [dec_f, dec_b, proj, proj, proj, proj, gn]
    if init is not None:
        in_specs += [st_spec, st_spec]
        args += list(init)
    out_specs = [pl.BlockSpec((seq, width), lambda b, p: (b, p))]
    out_shape = [jax.ShapeDtypeStruct((n_batch * seq, RET_W), BF16)]
    if out_state:
        out_specs += [st_spec, st_spec]
        out_shape += [jax.ShapeDtypeStruct((n_batch, RET_HEADS, RET_HD, RET_HD), F32)] * 2
    kern = functools.partial(_ret_kernel, seq=seq, has_init=init is not None, out_state=out_state)
    return pl.pallas_call(
        kern,
        grid=(n_batch, groups),
        in_specs=in_specs,
        out_specs=out_specs,
        out_shape=out_shape,
        scratch_shapes=[pltpu.VMEM((seq, width), F32), pltpu.VMEM((seq, width), F32)],
        compiler_params=_cparams(2),
        name="retention",
    )(*args)


def _swa_prep_kernel(p_ref, c_ref, sp_ref, sm_ref, q_ref, k_ref, v_ref):
    c, sp, sm = c_ref[...], sp_ref[...], sm_ref[...]
    for j in range(SWA_Q // LANE):
        sl = slice(LANE * j, LANE * (j + 1))
        q_ref[:, sl] = (_rope(p_ref[:, sl], c, sp, sm, SWA_HD // 4) * SWA_SCALE).astype(BF16)
    for j in range(SWA_KV // LANE):
        sl = slice(LANE * j, LANE * (j + 1))
        k_ref[:, sl] = _rope(p_ref[:, SWA_Q + LANE * j:SWA_Q + LANE * (j + 1)], c, sp, sm, SWA_HD // 4).astype(BF16)
    v_ref[...] = p_ref[:, SWA_Q + SWA_KV:].astype(BF16)


def _swa_prep(proj, tabs, per_batch):
    rows = proj.shape[0]
    c, sp, sm = tabs

    def tab_idx(i):
        return (0 if per_batch is None else 1 + i % per_batch, 0)

    tab_spec = pl.BlockSpec((TM, LANE), tab_idx)
    return pl.pallas_call(
        _swa_prep_kernel,
        grid=(rows // TM,),
        in_specs=[pl.BlockSpec((TM, SWA_Q + 2 * SWA_KV), lambda i: (i, 0)), tab_spec, tab_spec, tab_spec],
        out_specs=[pl.BlockSpec((TM, SWA_Q), lambda i: (i, 0)),
                   pl.BlockSpec((TM, SWA_KV), lambda i: (i, 0)),
                   pl.BlockSpec((TM, SWA_KV), lambda i: (i, 0))],
        out_shape=[jax.ShapeDtypeStruct((rows, SWA_Q), BF16),
                   jax.ShapeDtypeStruct((rows, SWA_KV), BF16),
                   jax.ShapeDtypeStruct((rows, SWA_KV), BF16)],
        compiler_params=_cparams(1),
        name="swa_prep",
    )(proj, c, sp, sm)


def _to_row_tiles(ref, x):
    rows = x.shape[0]
    for s in range(D // LANE):
        ref[pl.ds(s, rows, stride=SUB), :] = x[:, LANE * s:LANE * (s + 1)]


def _from_row_tiles(ref, rows):
    return jnp.concatenate([ref[pl.ds(s, rows, stride=SUB), :] for s in range(D // LANE)], axis=1)


def _post_kernel(*refs, n_parts):
    a_refs = refs[:n_parts]
    w_ref, x_ref, m_ref, g_ref, b_ref, rh_ref, rl_ref, x1_ref, h2_ref, lg_ref = refs[n_parts:]
    out = None
    off = 0
    for a_ref in a_refs:
        kk = a_ref.shape[1]
        part = _dot(a_ref[...], w_ref[off:off + kk, :])
        out = part if out is None else out + part
        off += kk
    z = ALPHA * x_ref[...] + m_ref[0, 2:3, :] * out
    x1 = _layernorm(z, g_ref[...], b_ref[...])
    x1_ref[...] = x1
    h2 = x1 * (1.0 + m_ref[0, 4:5, :]) + m_ref[0, 3:4, :]
    _to_row_tiles(h2_ref, h2)
    hi = h2.astype(BF16)
    lo = (h2 - hi.astype(F32)).astype(BF16)
    rh = rh_ref[...]
    lg_ref[...] = _dot(hi, rh) + _dot(lo, rh) + _dot(hi, rl_ref[...])


def _post_mixer(parts, w_out, x, mod, ln_g, ln_b, r_hi, r_lo, per_batch):
    rows = x.shape[0]
    full = lambda a: pl.BlockSpec(a.shape, lambda i: (0,) * a.ndim)
    in_specs = [pl.BlockSpec((TM, a.shape[1]), lambda i: (i, 0)) for a in parts]
    in_specs += [full(w_out), pl.BlockSpec((TM, D), lambda i: (i, 0)),
                 pl.BlockSpec((1, 6, D), lambda i: (_mod_row(i, per_batch), 0, 0)),
                 full(ln_g), full(ln_b), full(r_hi), full(r_lo)]
    return pl.pallas_call(
        functools.partial(_post_kernel, n_parts=len(parts)),
        grid=(rows // TM,),
        in_specs=in_specs,
        out_specs=[pl.BlockSpec((TM, D), lambda i: (i, 0)),
                   pl.BlockSpec((TM * SUB, LANE), lambda i: (i, 0)),
                   pl.BlockSpec((TM, LANE), lambda i: (i, 0))],
        out_shape=[jax.ShapeDtypeStruct((rows, D), F32),
                   jax.ShapeDtypeStruct((rows * SUB, LANE), F32),
                   jax.ShapeDtypeStruct((rows, LANE), F32)],
        compiler_params=_cparams(1),
        name="post_mixer",
    )(*parts, w_out, x, mod, ln_g, ln_b, r_hi, r_lo)


def _route_kernel(lg_ref, bias_ref, idx_ref, rank_ref, w_ref, cnt_ref, carry_ref):
    i = pl.program_id(0)

    @pl.when(i == 0)
    def _():
        carry_ref[...] = jnp.zeros_like(carry_ref)

    t = lg_ref.shape[0]
    gsz = N_EXP // N_GROUPS
    scores = jax.nn.sigmoid(lg_ref[...].T[:N_EXP])
    sel = scores + bias_ref[...]
    g3 = sel.reshape(N_GROUPS, gsz, t)
    sub_iota = lax.broadcasted_iota(jnp.int32, g3.shape, 1)
    m1 = jnp.max(g3, axis=1)
    first = jnp.min(jnp.where(g3 == m1[:, None, :], sub_iota, gsz), axis=1)
    m2 = jnp.max(jnp.where(sub_iota == first[:, None, :], -jnp.inf, g3), axis=1)
    grp = m1 + m2
    g_iota = lax.broadcasted_iota(jnp.int32, grp.shape, 0)
    gmask = jnp.zeros(grp.shape, jnp.bool_)
    for _ in range(TOPK_GROUPS):
        gm = jnp.max(grp, axis=0, keepdims=True)
        gi = jnp.min(jnp.where(grp == gm, g_iota, N_GROUPS), axis=0, keepdims=True)
        hit = g_iota == gi
        gmask = jnp.logical_or(gmask, hit)
        grp = jnp.where(hit, -jnp.inf, grp)
    emask = jnp.broadcast_to(gmask[:, None, :], g3.shape).reshape(N_EXP, t)
    cur = jnp.where(emask, sel, NEG_INF)
    e_iota = lax.broadcasted_iota(jnp.int32, cur.shape, 0)
    hits = []
    member = jnp.zeros(cur.shape, F32)
    for _ in range(TOP_K):
        cm = jnp.max(cur, axis=0, keepdims=True)
        ci = jnp.min(jnp.where(cur == cm, e_iota, N_EXP), axis=0, keepdims=True)
        hit = e_iota == ci
        hits.append((hit, ci))
        member = member + hit.astype(F32)
        cur = jnp.where(hit, -jnp.inf, cur)
    tri = (lax.broadcasted_iota(jnp.int32, (t, t), 0) < lax.broadcasted_iota(jnp.int32, (t, t), 1)).astype(BF16)
    before = _dot(member.astype(BF16), tri) + carry_ref[:, 0:1]
    ws = [jnp.sum(jnp.where(hit, scores, 0.0), axis=0, keepdims=True) for hit, _ in hits]
    wsum = ws[0]
    for w in ws[1:]:
        wsum = wsum + w
    for k, (hit, ci) in enumerate(hits):
        idx_ref[k:k + 1, :] = ci
        rank_ref[k:k + 1, :] = jnp.sum(jnp.where(hit, before, 0.0), axis=0, keepdims=True).astype(jnp.int32)
        w_ref[k:k + 1, :] = ws[k] / wsum * ROUTED_SCALE
    total = carry_ref[...] + jnp.sum(member, axis=1, keepdims=True)
    carry_ref[...] = total
    cnt_ref[...] = total.astype(jnp.int32)


def _route(logits, bias_col):
    rows = logits.shape[0]
    row_spec = pl.BlockSpec((TOP_K, TM), lambda i: (0, i))
    return pl.pallas_call(
        _route_kernel,
        grid=(rows // TM,),
        in_specs=[pl.BlockSpec((TM, LANE), lambda i: (i, 0)), pl.BlockSpec((N_EXP, 1), lambda i: (0, 0))],
        out_specs=[row_spec, row_spec, row_spec, pl.BlockSpec((N_EXP, LANE), lambda i: (0, 0))],
        out_shape=[jax.ShapeDtypeStruct((TOP_K, rows), jnp.int32),
                   jax.ShapeDtypeStruct((TOP_K, rows), jnp.int32),
                   jax.ShapeDtypeStruct((TOP_K, rows), F32),
                   jax.ShapeDtypeStruct((N_EXP, LANE), jnp.int32)],
        scratch_shapes=[pltpu.VMEM((N_EXP, LANE), F32)],
        compiler_params=_cparams(1),
        name="route",
    )(logits, bias_col)


EXP_LAG = 2
EXP_LEAD = (EXP_LAG + 1) * EXP_TM


def _experts_kernel(exp_ref, used_ref, dest_ref, seg_ref, cnt_ref, w_ref,
                    x_ref, wg_ref, wu_ref, wd_ref, y_ref,
                    xbuf0, xbuf1, obuf0, obuf1, wg_b0, wg_b1, wu_b0, wu_b1, asg, *, n_tok):
    s = pl.program_id(0)
    used = used_ref[0]
    n_assign = dest_ref.shape[0]
    pad_id = n_assign

    @pl.when(s == 0)
    def _():
        y_ref[...] = jnp.zeros_like(y_ref)
        for buf in (xbuf0, xbuf1, obuf0, obuf1, wg_b0, wg_b1, wu_b0, wu_b1):
            buf[...] = jnp.zeros_like(buf)

        def lead(j, c):
            for u in range(SUB):
                asg[j * SUB + u] = pad_id
            return c
        lax.fori_loop(0, EXP_LEAD // SUB, lead, 0)

        def fill(e, c):
            @pl.when(cnt_ref[e] > 0)
            def _():
                last = asg.at[pl.ds(pl.multiple_of(seg_ref[e + 1] - EXP_TM, EXP_TM), EXP_TM)]
                for u in range(EXP_TM):
                    last[u] = pad_id
            return c
        lax.fori_loop(0, N_EXP, fill, 0)

        def invert(j, c):
            for u in range(16):
                a = j * 16 + u
                asg[dest_ref[a]] = a
            return c
        lax.fori_loop(0, n_assign // 16, invert, 0)

    def stages(x_gather, x_mm, o_mm, o_scatter, wg_mm, wu_mm, wg_next, wu_next):
        g_slots = asg.at[pl.ds(jnp.minimum(s + 2, used + EXP_LAG) * EXP_TM, EXP_TM)]
        for r in range(EXP_TM):
            row0 = g_slots[r] & ((n_tok - 1) * SUB)
            x_gather[r * SUB:(r + 1) * SUB, :] = x_ref[pl.ds(pl.multiple_of(row0, SUB), SUB), :]

        x = _from_row_tiles(x_mm, EXP_TM).astype(BF16)
        h = (jax.nn.silu(_dot(x, wg_mm[...])) * _dot(x, wu_mm[...])).astype(BF16)
        _to_row_tiles(o_mm, _dot(h, wd_ref[...].astype(BF16)))
        wg_next[...] = wg_ref[...].astype(BF16)
        wu_next[...] = wu_ref[...].astype(BF16)

        s_slots = asg.at[pl.ds(s * EXP_TM, EXP_TM)]
        for j in range(EXP_TM // SUB):
            vals = []
            for u in range(SUB):
                r = j * SUB + u
                a = s_slots[r]
                rows = pl.ds(pl.multiple_of(a & -SUB, SUB), SUB)
                vals.append((rows, y_ref[rows, :] + w_ref[a] * o_scatter[r * SUB:(r + 1) * SUB, :]))
            for rows, v in vals:
                y_ref[rows, :] = v

    @pl.when(s < used + EXP_LAG + 1)
    def _():
        @pl.when(s % 2 == 0)
        def _():
            stages(xbuf0, xbuf1, obuf1, obuf0, wg_b0, wu_b0, wg_b1, wu_b1)

        @pl.when(s % 2 == 1)
        def _():
            stages(xbuf1, xbuf0, obuf0, obuf1, wg_b1, wu_b1, wg_b0, wu_b0)


def _experts(tile_exp, used, dest, seg, cnt, w_flat, h2_tiles, wg, wu, wd, layer):
    n_tok = h2_tiles.shape[0] // SUB
    assert TOP_K == SUB and dest.shape[0] == n_tok * TOP_K and w_flat.shape[0] == dest.shape[0] + SUB
    n_tiles = tile_exp.shape[0]

    def wmap(ahead):
        return lambda s, exp, used, *_: (layer, exp[jnp.clip(s - EXP_LAG + ahead, 0, used[0] - 1)], 0, 0)

    buf = pltpu.VMEM((EXP_TM * SUB, LANE), F32)
    gs = pltpu.PrefetchScalarGridSpec(
        num_scalar_prefetch=6,
        grid=(n_tiles + EXP_LAG + 1,),
        in_specs=[pl.BlockSpec(memory_space=pltpu.VMEM),
                  pl.BlockSpec((None, None, D, EXP_D), wmap(1)),
                  pl.BlockSpec((None, None, D, EXP_D), wmap(1)),
                  pl.BlockSpec((None, None, EXP_D, D), wmap(0))],
        out_specs=pl.BlockSpec(memory_space=pltpu.VMEM),
        scratch_shapes=[buf, buf, buf, buf] + [pltpu.VMEM((D, EXP_D), BF16)] * 4 + [
                        pltpu.SMEM((EXP_LEAD + n_tiles * EXP_TM,), jnp.int32)],
    )
    return pl.pallas_call(
        functools.partial(_experts_kernel, n_tok=n_tok),
        grid_spec=gs,
        out_shape=jax.ShapeDtypeStruct(((n_tok + 1) * SUB, LANE), F32),
        compiler_params=_cparams(1, VMEM_BIG),
        name="experts",
    )(tile_exp, used, dest, seg, cnt, w_flat, h2_tiles, wg, wu, wd)


def _dispatch_plan(idx, rank, w, cnt):
    n_tok = idx.shape[1]
    n_tiles = n_tok * TOP_K // EXP_TM + N_EXP
    tiles_e = (cnt + EXP_TM - 1) // EXP_TM
    tile_end = jnp.cumsum(tiles_e)
    seg = (EXP_LEAD + jnp.concatenate([jnp.zeros((1,), jnp.int32), tile_end * EXP_TM])).astype(jnp.int32)
    e_ids = jnp.arange(N_EXP, dtype=jnp.int32)
    dest = jnp.sum(jnp.where(idx[:, :, None] == e_ids, seg[:N_EXP], 0), axis=-1) + rank
    ii = jnp.arange(n_tiles, dtype=jnp.int32)
    used = tile_end[-1]
    tile_exp = jnp.sum(tile_end[None, :] <= jnp.minimum(ii, used - 1)[:, None], axis=1).astype(jnp.int32)
    tile_exp = jnp.minimum(tile_exp, N_EXP - 1)
    w_flat = jnp.concatenate([w.T.reshape(-1), jnp.zeros((SUB,), F32)])
    return tile_exp, used.reshape(1).astype(jnp.int32), dest.T.reshape(-1), seg, w_flat


def _final_kernel(y_ref, x1_ref, m_ref, sg_ref, su_ref, sd_ref, g_ref, b_ref, o_ref):
    x1 = x1_ref[...]
    routed = _from_row_tiles(y_ref, TM)
    h2 = (x1 * (1.0 + m_ref[0, 4:5, :]) + m_ref[0, 3:4, :]).astype(BF16)
    act = (jax.nn.silu(_dot(h2, sg_ref[...])) * _dot(h2, su_ref[...])).astype(BF16)
    shared = _dot(act, sd_ref[...])
    z = ALPHA * x1 + m_ref[0, 5:6, :] * (routed + shared)
    o_ref[...] = _layernorm(z, g_ref[...], b_ref[...])


def _final(y_tiles, x1, mod, sg, su, sd, ln_g, ln_b, per_batch):
    rows = x1.shape[0]
    full = lambda a: pl.BlockSpec(a.shape, lambda i: (0,) * a.ndim)
    return pl.pallas_call(
        _final_kernel,
        grid=(rows // TM,),
        in_specs=[pl.BlockSpec((TM * SUB, LANE), lambda i: (i, 0)),
                  pl.BlockSpec((TM, D), lambda i: (i, 0)),
                  pl.BlockSpec((1, 6, D), lambda i: (_mod_row(i, per_batch), 0, 0)),
                  full(sg), full(su), full(sd), full(ln_g), full(ln_b)],
        out_specs=pl.BlockSpec((TM, D), lambda i: (i, 0)),
        out_shape=jax.ShapeDtypeStruct((rows, D), F32),
        compiler_params=_cparams(1),
        name="final",
    )(y_tiles, x1, mod, sg, su, sd, ln_g, ln_b)


def _rope_tables(rot_dim, lane_lo, n_rot, n_rows):
    nf = rot_dim // 4
    t = jnp.arange(n_rows)
    r = (t // GRID_W).astype(F32)
    col = (t % GRID_W).astype(F32)
    freqs = ROPE_BASE ** (-jnp.arange(nf, dtype=F32) / nf)
    lane = jnp.arange(LANE)
    j = (lane - lane_lo) % rot_dim
    in_rot = jnp.logical_and(lane >= lane_lo, lane < lane_lo + n_rot * rot_dim)
    half = j // (2 * nf)
    second = (j % (2 * nf)) >= nf
    f = freqs[j % nf]
    ang = jnp.where(half[None, :] == 0, r[:, None], col[:, None]) * f[None, :]
    cos = jnp.where(in_rot[None, :], jnp.cos(ang), 1.0)
    sin = jnp.where(in_rot[None, :], jnp.sin(ang), 0.0)
    sp = jnp.where(second[None, :], sin, 0.0)
    sm = jnp.where(second[None, :], 0.0, -sin)
    ident = lambda v: jnp.full((TM, LANE), v, F32)
    return (jnp.concatenate([ident(1.0), cos], axis=0), jnp.concatenate([ident(0.0), sp], axis=0),
            jnp.concatenate([ident(0.0), sm], axis=0))


def _even_weights(w_in, q_norm, w_uq, kv_norm, w_uk, w_uv, w_out):
    z = lambda n: jnp.zeros((D, n), F32)
    mla_in = MLA_QR + MLA_KVR + MLA_ROPE
    w_in_p = jnp.concatenate([w_in[:, :MLA_QR + MLA_KVR], z(MLA_NOPE), w_in[:, MLA_QR + MLA_KVR:mla_in],
                              z(MLA_HP - MLA_NOPE - MLA_ROPE), w_in[:, mla_in:]], axis=1)
    uq = w_uq.reshape(MLA_QR, MLA_HEADS, MLA_NOPE + MLA_ROPE)
    uq = jnp.pad(uq, ((0, 0), (0, 0), (0, MLA_HP - MLA_NOPE - MLA_ROPE))).reshape(MLA_QR, MLA_HEADS * MLA_HP)
    uk = w_uk.reshape(MLA_KVR, MLA_HEADS, MLA_NOPE)
    uk = jnp.pad(uk, ((0, 0), (0, 0), (0, MLA_HP - MLA_NOPE))).reshape(MLA_KVR, MLA_HEADS * MLA_HP)
    return {"w_in": w_in_p.astype(BF16), "q_norm": q_norm.reshape(1, -1), "kv_norm": kv_norm.reshape(1, -1),
            "w_uq": uq.astype(BF16), "w_uk": uk.astype(BF16), "w_uv": w_uv.astype(BF16), "w_out": w_out.astype(BF16)}


def kernel(x_prompt, x_sample, cache_mla_ckv, cache_mla_kpe, state_ret_fwd, state_ret_bwd, cache_swa_k, cache_swa_v, c, c_ctx, w_mod, b_mod, ln1_g, ln1_b, ln2_g, ln2_b, mla_ret_w_in, mla_q_norm, mla_w_uq, mla_kv_norm, mla_w_uk, mla_w_uv, ret_decay_fwd, ret_decay_bwd, ret_gn_g, even_w_out, swa_w_in, swa_sink, swa_w_out, moe_router, moe_router_bias, moe_w_gate, moe_w_up, moe_w_down, shared_w_gate, shared_w_up, shared_w_down):
    n_p, seq_p, _ = x_prompt.shape
    n_s, seq_s, _ = x_sample.shape
    past = cache_mla_ckv.shape[2]
    groups = [
        dict(x=x_prompt.reshape(n_p * seq_p, D), nb=n_p, seq=seq_p, per_batch=None),
        dict(x=x_sample.reshape(n_s * seq_s, D), nb=n_s, seq=seq_s, per_batch=seq_s // TM),
    ]
    cond8 = jnp.zeros((8, D), F32).at[0].set(c_ctx).at[1:1 + n_s].set(c)
    mods = _modulation(cond8, w_mod, b_mod).reshape(DEPTH, 8, 6, D)
    tabs_mla = _rope_tables(MLA_ROPE, MLA_NOPE, 1, seq_s)
    tabs_swa = _rope_tables(SWA_HD, 0, LANE // SWA_HD, seq_s)

    outs = {k: [] for k in ("ckv", "kpe", "rf", "rb", "sk", "sv")}
    for l in range(DEPTH):
        mod = mods[l]
        r_pad = jnp.pad(moe_router[l], ((0, 0), (0, LANE - N_EXP)))
        r_hi = r_pad.astype(BF16)
        r_lo = (r_pad - r_hi.astype(F32)).astype(BF16)
        bias_col = moe_router_bias[l].reshape(N_EXP, 1)
        sg, su, sd = (shared_w_gate[l].astype(BF16), shared_w_up[l].astype(BF16), shared_w_down[l].astype(BF16))
        if l % 2 == 0:
            e = l // 2
            ew = _even_weights(mla_ret_w_in[e], mla_q_norm[e], mla_w_uq[e], mla_kv_norm[e], mla_w_uk[e],
                               mla_w_uv[e], even_w_out[e])
            kpe_ctx = jnp.pad(cache_mla_kpe[:, e].reshape(n_s * past, MLA_ROPE),
                              ((0, 0), (MLA_NOPE, MLA_HP - MLA_NOPE - MLA_ROPE)))
            ctx_kv = _mla_ctx(cache_mla_ckv[:, e].reshape(n_s * past, MLA_KVR), kpe_ctx, ew)
            gn = ret_gn_g[e].reshape(1, RET_W)
        else:
            o = l // 2
            w_in_o = swa_w_in[o].astype(BF16)
            w_out_o = swa_w_out[o].astype(BF16)
            ctx_swa = (cache_swa_k[:, o].reshape(n_s * past, SWA_KV).astype(BF16),
                       cache_swa_v[:, o].reshape(n_s * past, SWA_KV).astype(BF16))
        for gi, g in enumerate(groups):
            x, nb, seq, per_batch = g["x"], g["nb"], g["seq"], g["per_batch"]
            is_sample = gi == 1
            if l % 2 == 0:
                proj = _modmm(x, mod, ew["w_in"], per_batch)
                q, k, v, ckv, kpe = _mla_prep(proj, tabs_mla, ew, per_batch)
                o_mla = _attention(q, k, v, ctx_kv if is_sample else None, None, n_batch=nb, seq=seq,
                                   n_heads=MLA_HEADS, group=1, dqk=MLA_HP, dv=MLA_V,
                                   scale=(MLA_NOPE + MLA_ROPE) ** -0.5, window=0)
                init = (state_ret_fwd[:, e], state_ret_bwd[:, e]) if is_sample else None
                ret = _retention(proj, ret_decay_fwd[e], ret_decay_bwd[e], gn, init, n_batch=nb, seq=seq,
                                 out_state=not is_sample)
                if is_sample:
                    o_ret = ret[0]
                else:
                    o_ret, s_f, s_b = ret
                    outs["ckv"].append(ckv.reshape(nb, seq, MLA_KVR))
                    outs["kpe"].append(kpe[:, MLA_NOPE:MLA_NOPE + MLA_ROPE].reshape(nb, seq, MLA_ROPE))
                    outs["rf"].append(s_f)
                    outs["rb"].append(s_b)
                parts, w_out = [o_mla, o_ret], ew["w_out"]
            else:
                proj = _modmm(x, mod, w_in_o, per_batch)
                q, k, v = _swa_prep(proj, tabs_swa, per_batch)
                o_swa = _attention(q, k, v, ctx_swa if is_sample else None, swa_sink[o], n_batch=nb, seq=seq,
                                   n_heads=SWA_HEADS, group=SWA_HEADS // SWA_KVH, dqk=SWA_HD, dv=SWA_HD,
                                   scale=None, window=WINDOW if is_sample else 0)
                if not is_sample:
                    outs["sk"].append(proj[:, SWA_Q:SWA_Q + SWA_KV].reshape(nb, seq, SWA_KVH, SWA_HD))
                    outs["sv"].append(proj[:, SWA_Q + SWA_KV:].reshape(nb, seq, SWA_KVH, SWA_HD))
                parts, w_out = [o_swa], w_out_o
            x1, h2_tiles, logits = _post_mixer(parts, w_out, x, mod, ln1_g[l].reshape(1, D), ln1_b[l].reshape(1, D),
                                               r_hi, r_lo, per_batch)
            idx, rank, w, cnt = _route(logits, bias_col)
            tile_exp, used, dest, seg, w_flat = _dispatch_plan(idx, rank, w, cnt[:, 0])
            y_tiles = _experts(tile_exp, used, dest, seg, cnt[:, 0], w_flat, h2_tiles,
                               moe_w_gate, moe_w_up, moe_w_down, l)
            g["x"] = _final(y_tiles, x1, mod, sg, su, sd,
                            ln2_g[l].reshape(1, D), ln2_b[l].reshape(1, D), per_batch)
    y_prompt = groups[0]["x"].reshape(n_p, seq_p, D)
    y_sample = groups[1]["x"].reshape(n_s, seq_s, D)
    return (y_prompt, y_sample, jnp.stack(outs["ckv"], axis=1), jnp.stack(outs["kpe"], axis=1),
            jnp.stack(outs["rf"], axis=1), jnp.stack(outs["rb"], axis=1),
            jnp.stack(outs["sk"], axis=1), jnp.stack(outs["sv"], axis=1))
```

```python
import functools

import jax
import jax.numpy as jnp
from jax import lax
from jax.experimental import pallas as pl
from jax.experimental.pallas import tpu as pltpu

F32 = jnp.float32
BF16 = jnp.bfloat16

D = 1024
DEPTH = 4
GRID_W = 64
ALPHA = (2.0 * DEPTH) ** 0.25
LN_EPS = 1e-5
RMS_EPS = 1e-6
ROPE_BASE = 10000.0
NEG_INF = -1e30
MLA_HEADS = 8
MLA_NOPE = 64
MLA_ROPE = 32
MLA_V = 64
MLA_QR = 256
MLA_KVR = 128
MLA_HP = 128
RET_HEADS = 8
RET_HD = 64
RET_CHUNK = 256
RET_W = RET_HEADS * RET_HD
RET_HPS = 8
EVEN_P = 512 + 4 * RET_W
SWA_HEADS = 16
SWA_KVH = 4
SWA_HD = 64
WINDOW = 128
SWA_Q = SWA_HEADS * SWA_HD
SWA_KV = SWA_KVH * SWA_HD
SWA_SCALE = SWA_HD ** -0.5
assert SWA_SCALE == 0.125
N_EXP = 64
TOP_K = 8
N_GROUPS = 8
TOPK_GROUPS = 4
EXP_D = 256
ROUTED_SCALE = 2.5

TM = 256
TM2 = 512
SUB = 8
LANE = 128
EXP_TM = 256
VMEM_BIG = 56 * 1024 * 1024


def _cparams(n_axes, vmem=None):
    return pltpu.CompilerParams(dimension_semantics=("arbitrary",) * n_axes, vmem_limit_bytes=vmem)


def _dot(a, b):
    return jnp.dot(a, b, preferred_element_type=F32)


def _dot_nt(a, b):
    return lax.dot_general(a, b, (((1,), (1,)), ((), ())), preferred_element_type=F32)


def _dot_tn(a, b):
    return lax.dot_general(a, b, (((0,), (0,)), ((), ())), preferred_element_type=F32)


def _layernorm(z, g, b):
    mu = jnp.mean(z, axis=-1, keepdims=True)
    zc = z - mu
    var = jnp.mean(zc * zc, axis=-1, keepdims=True)
    return zc * lax.rsqrt(var + LN_EPS) * g + b


def _rmsnorm(x, g):
    return x * lax.rsqrt(jnp.mean(x * x, axis=-1, keepdims=True) + RMS_EPS) * g


def _rope(x, c, sp, sm, shift):
    w = x.shape[-1]
    return x * c + pltpu.roll(x, shift, 1) * sp + pltpu.roll(x, w - shift, 1) * sm


def _mod_kernel(c_ref, w_ref, b_ref, o_ref):
    s = jax.nn.silu(c_ref[...]).astype(BF16)
    o_ref[0] = _dot(s, w_ref[0].astype(BF16)) + b_ref[0]


def _modulation(cond8, w_mod, b_mod):
    nt = 4
    tn = 6 * D // nt
    return pl.pallas_call(
        _mod_kernel,
        grid=(DEPTH, nt),
        in_specs=[pl.BlockSpec((8, D), lambda l, j: (0, 0)),
                  pl.BlockSpec((1, D, tn), lambda l, j: (l, 0, j)),
                  pl.BlockSpec((1, 1, tn), lambda l, j: (l, 0, j))],
        out_specs=pl.BlockSpec((1, 8, tn), lambda l, j: (l, 0, j)),
        out_shape=jax.ShapeDtypeStruct((DEPTH, 8, 6 * D), F32),
        compiler_params=_cparams(2, 40 * 1024 * 1024),
        name="modulation",
    )(cond8, w_mod, b_mod.reshape(DEPTH, 1, 6 * D))


def _mod_row(i, per_batch, tm=TM):
    return 0 if per_batch is None else 1 + i // (per_batch * TM // tm)


def _modmm_kernel(x_ref, m_ref, w_ref, o_ref, *, shift_i, scale_i):
    h = x_ref[...] * (1.0 + m_ref[0, scale_i:scale_i + 1, :]) + m_ref[0, shift_i:shift_i + 1, :]
    o_ref[...] = _dot(h.astype(BF16), w_ref[...])


def _modmm(x, mod, w, per_batch):
    rows = x.shape[0]
    n = w.shape[1]
    return pl.pallas_call(
        functools.partial(_modmm_kernel, shift_i=0, scale_i=1),
        grid=(rows // TM2,),
        in_specs=[pl.BlockSpec((TM2, D), lambda i: (i, 0)),
                  pl.BlockSpec((1, 6, D), lambda i: (_mod_row(i, per_batch, TM2), 0, 0)),
                  pl.BlockSpec((D, n), lambda i: (0, 0))],
        out_specs=pl.BlockSpec((TM2, n), lambda i: (i, 0)),
        out_shape=jax.ShapeDtypeStruct((rows, n), F32),
        compiler_params=_cparams(1, 40 * 1024 * 1024),
        name="modmm",
    )(x, mod, w)


def _mla_prep_kernel(p_ref, c_ref, sp_ref, sm_ref, qn_ref, kvn_ref, wuq_ref, wuk_ref, wuv_ref,
                     q_ref, k_ref, v_ref, ckv_ref, kpe_ref):
    p = p_ref[...]
    c, sp, sm = c_ref[...], sp_ref[...], sm_ref[...]
    qn = _rmsnorm(p[:, 0:MLA_QR], qn_ref[...])
    q = _dot(qn.astype(BF16), wuq_ref[...])
    ckv = _rmsnorm(p[:, MLA_QR:MLA_QR + MLA_KVR], kvn_ref[...])
    ckv_ref[...] = ckv
    kpe = _rope(p[:, MLA_QR + MLA_KVR:512], c, sp, sm, MLA_ROPE // 4)
    kpe_ref[...] = kpe
    ckv_b = ckv.astype(BF16)
    kn = _dot(ckv_b, wuk_ref[...])
    for h in range(MLA_HEADS):
        sl = slice(MLA_HP * h, MLA_HP * (h + 1))
        q_ref[:, sl] = _rope(q[:, sl], c, sp, sm, MLA_ROPE // 4).astype(BF16)
        k_ref[:, sl] = (kn[:, sl] + kpe).astype(BF16)
    v_ref[...] = _dot(ckv_b, wuv_ref[...]).astype(BF16)


def _mla_prep(proj, tabs, ew, per_batch):
    rows = proj.shape[0]
    c, sp, sm = tabs

    def tab_idx(i):
        return (0 if per_batch is None else 1 + i % per_batch, 0)

    tab_spec = pl.BlockSpec((TM, LANE), tab_idx)
    full = lambda a: pl.BlockSpec(a.shape, lambda i: (0,) * a.ndim)
    return pl.pallas_call(
        _mla_prep_kernel,
        grid=(rows // TM,),
        in_specs=[pl.BlockSpec((TM, 512), lambda i: (i, 0)), tab_spec, tab_spec, tab_spec,
                  full(ew["q_norm"]), full(ew["kv_norm"]), full(ew["w_uq"]), full(ew["w_uk"]), full(ew["w_uv"])],
        out_specs=[pl.BlockSpec((TM, MLA_HEADS * MLA_HP), lambda i: (i, 0)),
                   pl.BlockSpec((TM, MLA_HEADS * MLA_HP), lambda i: (i, 0)),
                   pl.BlockSpec((TM, MLA_HEADS * MLA_V), lambda i: (i, 0)),
                   pl.BlockSpec((TM, MLA_KVR), lambda i: (i, 0)),
                   pl.BlockSpec((TM, LANE), lambda i: (i, 0))],
        out_shape=[jax.ShapeDtypeStruct((rows, MLA_HEADS * MLA_HP), BF16),
                   jax.ShapeDtypeStruct((rows, MLA_HEADS * MLA_HP), BF16),
                   jax.ShapeDtypeStruct((rows, MLA_HEADS * MLA_V), BF16),
                   jax.ShapeDtypeStruct((rows, MLA_KVR), F32),
                   jax.ShapeDtypeStruct((rows, LANE), F32)],
        compiler_params=_cparams(1),
        name="mla_prep",
    )(proj, c, sp, sm, ew["q_norm"], ew["kv_norm"], ew["w_uq"], ew["w_uk"], ew["w_uv"])


def _mla_ctx_kernel(ckv_ref, kpe_ref, wuk_ref, wuv_ref, k_ref, v_ref):
    ckv_b = ckv_ref[...].astype(BF16)
    kn = _dot(ckv_b, wuk_ref[...])
    kpe = kpe_ref[...]
    for h in range(MLA_HEADS):
        sl = slice(MLA_HP * h, MLA_HP * (h + 1))
        k_ref[:, sl] = (kn[:, sl] + kpe).astype(BF16)
    v_ref[...] = _dot(ckv_b, wuv_ref[...]).astype(BF16)


def _mla_ctx(ckv, kpe_pad, ew):
    rows = ckv.shape[0]
    full = lambda a: pl.BlockSpec(a.shape, lambda i: (0,) * a.ndim)
    return pl.pallas_call(
        _mla_ctx_kernel,
        grid=(rows // TM,),
        in_specs=[pl.BlockSpec((TM, MLA_KVR), lambda i: (i, 0)), pl.BlockSpec((TM, LANE), lambda i: (i, 0)),
                  full(ew["w_uk"]), full(ew["w_uv"])],
        out_specs=[pl.BlockSpec((TM, MLA_HEADS * MLA_HP), lambda i: (i, 0)),
                   pl.BlockSpec((TM, MLA_HEADS * MLA_V), lambda i: (i, 0))],
        out_shape=[jax.ShapeDtypeStruct((rows, MLA_HEADS * MLA_HP), BF16),
                   jax.ShapeDtypeStruct((rows, MLA_HEADS * MLA_V), BF16)],
        compiler_params=_cparams(1),
        name="mla_ctx",
    )(ckv, kpe_pad, ew["w_uk"], ew["w_uv"])


def _attn_kernel(*refs, n_heads, group, dqk, dv, scale, has_ctx, has_sink, window, tq, seq):
    refs = list(refs)
    sink_ref = refs.pop(0) if has_sink else None
    q_ref, k_ref, v_ref = refs[:3]
    kc_ref, vc_ref = (refs[3], refs[4]) if has_ctx else (None, None)
    o_ref = refs[-1]
    i = pl.program_id(1)
    if window:
        kw = tq + 2 * window
        start = pl.multiple_of(jnp.clip(i * tq - window, 0, seq - kw), LANE)
        qpos = i * tq + lax.broadcasted_iota(jnp.int32, (tq, kw), 0)
        kpos = start + lax.broadcasted_iota(jnp.int32, (tq, kw), 1)
        valid = jnp.abs(qpos - kpos) <= window
    def scores(q, k):
        s = _dot_nt(q, k)
        return s if scale is None else s * scale

    for h in range(n_heads):
        hk = h // group
        q = q_ref[:, h * dqk:(h + 1) * dqk]
        if window:
            k = k_ref[pl.ds(start, kw), hk * dqk:(hk + 1) * dqk]
            v = v_ref[pl.ds(start, kw), hk * dv:(hk + 1) * dv]
            s = jnp.where(valid, scores(q, k), NEG_INF)
        else:
            k = k_ref[:, hk * dqk:(hk + 1) * dqk]
            v = v_ref[:, hk * dv:(hk + 1) * dv]
            s = scores(q, k)
        m = jnp.max(s, axis=-1, keepdims=True)
        if has_ctx:
            sc = scores(q, kc_ref[:, hk * dqk:(hk + 1) * dqk])
            m = jnp.maximum(m, jnp.max(sc, axis=-1, keepdims=True))
        if has_sink:
            sk = sink_ref[h]
            m = jnp.maximum(m, sk)
        p = jnp.exp(s - m)
        l = jnp.sum(p, axis=-1, keepdims=True)
        o = _dot(p.astype(BF16), v)
        if has_ctx:
            pc = jnp.exp(sc - m)
            l = l + jnp.sum(pc, axis=-1, keepdims=True)
            o = o + _dot(pc.astype(BF16), vc_ref[:, hk * dv:(hk + 1) * dv])
        if has_sink:
            l = l + jnp.exp(sk - m)
        o_ref[:, h * dv:(h + 1) * dv] = (o / l).astype(o_ref.dtype)


def _attention(q, k, v, ctx, sink, *, n_batch, seq, n_heads, group, dqk, dv, scale, window):
    tq = TM
    nq = seq // tq
    n_kv = n_heads // group
    in_specs = []
    args = []
    if sink is not None:
        in_specs.append(pl.BlockSpec(memory_space=pltpu.SMEM))
        args.append(sink)
    in_specs += [pl.BlockSpec((tq, n_heads * dqk), lambda b, i: (b * nq + i, 0)),
                 pl.BlockSpec((seq, n_kv * dqk), lambda b, i: (b, 0)),
                 pl.BlockSpec((seq, n_kv * dv), lambda b, i: (b, 0))]
    args += [q, k, v]
    if ctx is not None:
        kc, vc = ctx
        sc = kc.shape[0] // n_batch
        in_specs += [pl.BlockSpec((sc, n_kv * dqk), lambda b, i: (b, 0)),
                     pl.BlockSpec((sc, n_kv * dv), lambda b, i: (b, 0))]
        args += [kc, vc]
    kern = functools.partial(_attn_kernel, n_heads=n_heads, group=group, dqk=dqk, dv=dv, scale=scale,
                             has_ctx=ctx is not None, has_sink=sink is not None, window=window, tq=tq, seq=seq)
    return pl.pallas_call(
        kern,
        grid=(n_batch, nq),
        in_specs=in_specs,
        out_specs=pl.BlockSpec((tq, n_heads * dv), lambda b, i: (b * nq + i, 0)),
        out_shape=jax.ShapeDtypeStruct((n_batch * seq, n_heads * dv), BF16),
        compiler_params=_cparams(2, 40 * 1024 * 1024),
        name="attention",
    )(*args)


def _ret_kernel(*refs, seq, has_init, out_state):
    refs = list(refs)
    df_ref, db_ref, rq_ref, rk_ref, rv_ref, rg_ref, gn_ref = refs[:7]
    pos = 7
    if has_init:
        s0f_ref, s0b_ref = refs[pos], refs[pos + 1]
        pos += 2
    o_ref = refs[pos]
    pos += 1
    if out_state:
        sf_ref, sb_ref = refs[pos], refs[pos + 1]
        pos += 2
    of_scr, ob_scr = refs[pos], refs[pos + 1]

    grp = pl.program_id(1)
    n_chunks = seq // RET_CHUNK
    idx_c = lax.broadcasted_iota(jnp.int32, (RET_CHUNK, 1), 0).astype(F32)
    diff = (lax.broadcasted_iota(jnp.int32, (RET_CHUNK, RET_CHUNK), 0)
            - lax.broadcasted_iota(jnp.int32, (RET_CHUNK, RET_CHUNK), 1)).astype(F32)
    for hh in range(RET_HPS):
        h = RET_HPS * grp + hh
        hs = slice(RET_HD * hh, RET_HD * (hh + 1))
        rows = [slice(RET_CHUNK * ci, RET_CHUNK * (ci + 1)) for ci in range(n_chunks)]
        qs = [rq_ref[r, hs].astype(BF16) for r in rows]
        ks = [rk_ref[r, hs] * (RET_HD ** -0.5) for r in rows]
        vs = [rv_ref[r, hs].astype(BF16) for r in rows]
        qk = [_dot_nt(q, k.astype(BF16)) for q, k in zip(qs, ks)]
        for fwd in (True, False):
            d = jnp.full((1, 1), (df_ref if fwd else db_ref)[h], F32)
            lg = jnp.minimum(d, 0.0) - jnp.log1p(jnp.exp(-jnp.abs(d)))
            dd = diff if fwd else -diff
            mask = jnp.where(dd >= 0, jnp.exp(lg * jnp.maximum(dd, 0.0)), 0.0)
            if fwd:
                q_dec = jnp.exp(lg * (idx_c + 1.0))
                k_dec = jnp.exp(lg * (RET_CHUNK - 1.0 - idx_c))
            else:
                q_dec = jnp.exp(lg * (RET_CHUNK - idx_c))
                k_dec = jnp.exp(lg * idx_c)
            c_dec = jnp.exp(lg * RET_CHUNK)
            scr = of_scr if fwd else ob_scr
            state = (s0f_ref if fwd else s0b_ref)[0, hh] if has_init else None
            for ci in (range(n_chunks) if fwd else reversed(range(n_chunks))):
                o = _dot((qk[ci] * mask).astype(BF16), vs[ci])
                kv = _dot_tn((ks[ci] * k_dec).astype(BF16), vs[ci])
                if state is not None:
                    o = o + _dot(qs[ci], state.astype(BF16)) * q_dec
                    kv = state * c_dec + kv
                scr[rows[ci], hs] = o
                state = kv
            if out_state:
                (sf_ref if fwd else sb_ref)[0, hh] = state

    def head_norm(x):
        mu = jnp.mean(x, axis=-1, keepdims=True)
        xc = x - mu
        return xc * lax.rsqrt(jnp.mean(xc * xc, axis=-1, keepdims=True) + LN_EPS)

    for hh in range(RET_HPS):
        hs = slice(RET_HD * hh, RET_HD * (hh + 1))
        o = head_norm(of_scr[:, hs]) + head_norm(ob_scr[:, hs])
        o_ref[:, hs] = (o * gn_ref[:, hs] * jax.nn.silu(rg_ref[:, hs])).astype(o_ref.dtype)


def _retention(proj, dec_f, dec_b, gn, init, *, n_batch, seq, out_state):
    groups = RET_HEADS // RET_HPS
    width = RET_HPS * RET_HD
    col0 = 512 // width

    def col_spec(k):
        return pl.BlockSpec((seq, width), lambda b, p: (b, col0 + k * groups + p))

    smem = pl.BlockSpec(memory_space=pltpu.SMEM)
    st_spec = pl.BlockSpec((1, RET_HPS, RET_HD, RET_HD), lambda b, p: (b, p, 0, 0))
    in_specs = [smem, smem, col_spec(0), col_spec(1), col_spec(2), col_spec(3),
                pl.BlockSpec((1, width), lambda b, p: (0, p))]
    args = [dec_f, dec_b, proj, proj, proj, proj, gn]
    if init is not None:
        in_specs += [st_spec, st_spec]
        args += list(init)
    out_specs = [pl.BlockSpec((seq, width), lambda b, p: (b, p))]
    out_shape = [jax.ShapeDtypeStruct((n_batch * seq, RET_W), BF16)]
    if out_state:
        out_specs += [st_spec, st_spec]
        out_shape += [jax.ShapeDtypeStruct((n_batch, RET_HEADS, RET_HD, RET_HD), F32)] * 2
    kern = functools.partial(_ret_kernel, seq=seq, has_init=init is not None, out_state=out_state)
    return pl.pallas_call(
        kern,
        grid=(n_batch, groups),
        in_specs=in_specs,
        out_specs=out_specs,
        out_shape=out_shape,
        scratch_shapes=[pltpu.VMEM((seq, width), F32), pltpu.VMEM((seq, width), F32)],
        compiler_params=_cparams(2),
        name="retention",
    )(*args)


def _swa_prep_kernel(p_ref, c_ref, sp_ref, sm_ref, q_ref, k_ref, v_ref):
    c, sp, sm = c_ref[...], sp_ref[...], sm_ref[...]
    for j in range(SWA_Q // LANE):
        sl = slice(LANE * j, LANE * (j + 1))
        q_ref[:, sl] = (_rope(p_ref[:, sl], c, sp, sm, SWA_HD // 4) * SWA_SCALE).astype(BF16)
    for j in range(SWA_KV // LANE):
        sl = slice(LANE * j, LANE * (j + 1))
        k_ref[:, sl] = _rope(p_ref[:, SWA_Q + LANE * j:SWA_Q + LANE * (j + 1)], c, sp, sm, SWA_HD // 4).astype(BF16)
    v_ref[...] = p_ref[:, SWA_Q + SWA_KV:].astype(BF16)


def _swa_prep(proj, tabs, per_batch):
    rows = proj.shape[0]
    c, sp, sm = tabs

    def tab_idx(i):
        return (0 if per_batch is None else 1 + i % per_batch, 0)

    tab_spec = pl.BlockSpec((TM, LANE), tab_idx)
    return pl.pallas_call(
        _swa_prep_kernel,
        grid=(rows // TM,),
        in_specs=[pl.BlockSpec((TM, SWA_Q + 2 * SWA_KV), lambda i: (i, 0)), tab_spec, tab_spec, tab_spec],
        out_specs=[pl.BlockSpec((TM, SWA_Q), lambda i: (i, 0)),
                   pl.BlockSpec((TM, SWA_KV), lambda i: (i, 0)),
                   pl.BlockSpec((TM, SWA_KV), lambda i: (i, 0))],
        out_shape=[jax.ShapeDtypeStruct((rows, SWA_Q), BF16),
                   jax.ShapeDtypeStruct((rows, SWA_KV), BF16),
                   jax.ShapeDtypeStruct((rows, SWA_KV), BF16)],
        compiler_params=_cparams(1),
        name="swa_prep",
    )(proj, c, sp, sm)


def _to_row_tiles(ref, x):
    rows = x.shape[0]
    for s in range(D // LANE):
        ref[pl.ds(s, rows, stride=SUB), :] = x[:, LANE * s:LANE * (s + 1)]


def _from_row_tiles(ref, rows):
    return jnp.concatenate([ref[pl.ds(s, rows, stride=SUB), :] for s in range(D // LANE)], axis=1)


def _post_kernel(*refs, n_parts):
    a_refs = refs[:n_parts]
    w_ref, x_ref, m_ref, g_ref, b_ref, rh_ref, rl_ref, x1_ref, h2_ref, lg_ref = refs[n_parts:]
    out = None
    off = 0
    for a_ref in a_refs:
        kk = a_ref.shape[1]
        part = _dot(a_ref[...], w_ref[off:off + kk, :])
        out = part if out is None else out + part
        off += kk
    z = ALPHA * x_ref[...] + m_ref[0, 2:3, :] * out
    x1 = _layernorm(z, g_ref[...], b_ref[...])
    x1_ref[...] = x1
    h2 = x1 * (1.0 + m_ref[0, 4:5, :]) + m_ref[0, 3:4, :]
    _to_row_tiles(h2_ref, h2)
    hi = h2.astype(BF16)
    lo = (h2 - hi.astype(F32)).astype(BF16)
    rh = rh_ref[...]
    lg_ref[...] = _dot(hi, rh) + _dot(lo, rh) + _dot(hi, rl_ref[...])


def _post_mixer(parts, w_out, x, mod, ln_g, ln_b, r_hi, r_lo, per_batch):
    rows = x.shape[0]
    full = lambda a: pl.BlockSpec(a.shape, lambda i: (0,) * a.ndim)
    in_specs = [pl.BlockSpec((TM, a.shape[1]), lambda i: (i, 0)) for a in parts]
    in_specs += [full(w_out), pl.BlockSpec((TM, D), lambda i: (i, 0)),
                 pl.BlockSpec((1, 6, D), lambda i: (_mod_row(i, per_batch), 0, 0)),
                 full(ln_g), full(ln_b), full(r_hi), full(r_lo)]
    return pl.pallas_call(
        functools.partial(_post_kernel, n_parts=len(parts)),
        grid=(rows // TM,),
        in_specs=in_specs,
        out_specs=[pl.BlockSpec((TM, D), lambda i: (i, 0)),
                   pl.BlockSpec((TM * SUB, LANE), lambda i: (i, 0)),
                   pl.BlockSpec((TM, LANE), lambda i: (i, 0))],
        out_shape=[jax.ShapeDtypeStruct((rows, D), F32),
                   jax.ShapeDtypeStruct((rows * SUB, LANE), F32),
                   jax.ShapeDtypeStruct((rows, LANE), F32)],
        compiler_params=_cparams(1),
        name="post_mixer",
    )(*parts, w_out, x, mod, ln_g, ln_b, r_hi, r_lo)


def _route_kernel(lg_ref, bias_ref, idx_ref, rank_ref, w_ref, cnt_ref, carry_ref):
    i = pl.program_id(0)

    @pl.when(i == 0)
    def _():
        carry_ref[...] = jnp.zeros_like(carry_ref)

    t = lg_ref.shape[0]
    gsz = N_EXP // N_GROUPS
    scores = jax.nn.sigmoid(lg_ref[...].T[:N_EXP])
    sel = scores + bias_ref[...]
    g3 = sel.reshape(N_GROUPS, gsz, t)
    sub_iota = lax.broadcasted_iota(jnp.int32, g3.shape, 1)
    m1 = jnp.max(g3, axis=1)
    first = jnp.min(jnp.where(g3 == m1[:, None, :], sub_iota, gsz), axis=1)
    m2 = jnp.max(jnp.where(sub_iota == first[:, None, :], -jnp.inf, g3), axis=1)
    grp = m1 + m2
    g_iota = lax.broadcasted_iota(jnp.int32, grp.shape, 0)
    gmask = jnp.zeros(grp.shape, jnp.bool_)
    for _ in range(TOPK_GROUPS):
        gm = jnp.max(grp, axis=0, keepdims=True)
        gi = jnp.min(jnp.where(grp == gm, g_iota, N_GROUPS), axis=0, keepdims=True)
        hit = g_iota == gi
        gmask = jnp.logical_or(gmask, hit)
        grp = jnp.where(hit, -jnp.inf, grp)
    emask = jnp.broadcast_to(gmask[:, None, :], g3.shape).reshape(N_EXP, t)
    cur = jnp.where(emask, sel, NEG_INF)
    e_iota = lax.broadcasted_iota(jnp.int32, cur.shape, 0)
    hits = []
    member = jnp.zeros(cur.shape, F32)
    for _ in range(TOP_K):
        cm = jnp.max(cur, axis=0, keepdims=True)
        ci = jnp.min(jnp.where(cur == cm, e_iota, N_EXP), axis=0, keepdims=True)
        hit = e_iota == ci
        hits.append((hit, ci))
        member = member + hit.astype(F32)
        cur = jnp.where(hit, -jnp.inf, cur)
    tri = (lax.broadcasted_iota(jnp.int32, (t, t), 0) < lax.broadcasted_iota(jnp.int32, (t, t), 1)).astype(BF16)
    before = _dot(member.astype(BF16), tri) + carry_ref[:, 0:1]
    ws = [jnp.sum(jnp.where(hit, scores, 0.0), axis=0, keepdims=True) for hit, _ in hits]
    wsum = ws[0]
    for w in ws[1:]:
        wsum = wsum + w
    for k, (hit, ci) in enumerate(hits):
        idx_ref[k:k + 1, :] = ci
        rank_ref[k:k + 1, :] = jnp.sum(jnp.where(hit, before, 0.0), axis=0, keepdims=True).astype(jnp.int32)
        w_ref[k:k + 1, :] = ws[k] / wsum * ROUTED_SCALE
    total = carry_ref[...] + jnp.sum(member, axis=1, keepdims=True)
    carry_ref[...] = total
    cnt_ref[...] = total.astype(jnp.int32)


def _route(logits, bias_col):
    rows = logits.shape[0]
    row_spec = pl.BlockSpec((TOP_K, TM), lambda i: (0, i))
    return pl.pallas_call(
        _route_kernel,
        grid=(rows // TM,),
        in_specs=[pl.BlockSpec((TM, LANE), lambda i: (i, 0)), pl.BlockSpec((N_EXP, 1), lambda i: (0, 0))],
        out_specs=[row_spec, row_spec, row_spec, pl.BlockSpec((N_EXP, LANE), lambda i: (0, 0))],
        out_shape=[jax.ShapeDtypeStruct((TOP_K, rows), jnp.int32),
                   jax.ShapeDtypeStruct((TOP_K, rows), jnp.int32),
                   jax.ShapeDtypeStruct((TOP_K, rows), F32),
                   jax.ShapeDtypeStruct((N_EXP, LANE), jnp.int32)],
        scratch_shapes=[pltpu.VMEM((N_EXP, LANE), F32)],
        compiler_params=_cparams(1),
        name="route",
    )(logits, bias_col)


EXP_LAG = 2
EXP_LEAD = (EXP_LAG + 1) * EXP_TM


def _experts_kernel(exp_ref, used_ref, dest_ref, seg_ref, cnt_ref, w_ref,
                    x_ref, wg_ref, wu_ref, wd_ref, y_ref,
                    xbuf0, xbuf1, obuf0, obuf1, wg_b0, wg_b1, wu_b0, wu_b1, asg, *, n_tok):
    s = pl.program_id(0)
    used = used_ref[0]
    n_assign = dest_ref.shape[0]
    pad_id = n_assign

    @pl.when(s == 0)
    def _():
        y_ref[...] = jnp.zeros_like(y_ref)
        for buf in (xbuf0, xbuf1, obuf0, obuf1, wg_b0, wg_b1, wu_b0, wu_b1):
            buf[...] = jnp.zeros_like(buf)

        def lead(j, c):
            for u in range(SUB):
                asg[j * SUB + u] = pad_id
            return c
        lax.fori_loop(0, EXP_LEAD // SUB, lead, 0)

        def fill(e, c):
            @pl.when(cnt_ref[e] > 0)
            def _():
                last = asg.at[pl.ds(pl.multiple_of(seg_ref[e + 1] - EXP_TM, EXP_TM), EXP_TM)]
                for u in range(EXP_TM):
                    last[u] = pad_id
            return c
        lax.fori_loop(0, N_EXP, fill, 0)

        def invert(j, c):
            for u in range(16):
                a = j * 16 + u
                asg[dest_ref[a]] = a
            return c
        lax.fori_loop(0, n_assign // 16, invert, 0)

    def stages(x_gather, x_mm, o_mm, o_scatter, wg_mm, wu_mm, wg_next, wu_next):
        g_slots = asg.at[pl.ds(jnp.minimum(s + 2, used + EXP_LAG) * EXP_TM, EXP_TM)]
        for r in range(EXP_TM):
            row0 = g_slots[r] & ((n_tok - 1) * SUB)
            x_gather[r * SUB:(r + 1) * SUB, :] = x_ref[pl.ds(pl.multiple_of(row0, SUB), SUB), :]

        x = _from_row_tiles(x_mm, EXP_TM).astype(BF16)
        h = (jax.nn.silu(_dot(x, wg_mm[...])) * _dot(x, wu_mm[...])).astype(BF16)
        _to_row_tiles(o_mm, _dot(h, wd_ref[...].astype(BF16)))
        wg_next[...] = wg_ref[...].astype(BF16)
        wu_next[...] = wu_ref[...].astype(BF16)

        s_slots = asg.at[pl.ds(s * EXP_TM, EXP_TM)]
        for j in range(EXP_TM // SUB):
            vals = []
            for u in range(SUB):
                r = j * SUB + u
                a = s_slots[r]
                rows = pl.ds(pl.multiple_of(a & -SUB, SUB), SUB)
                vals.append((rows, y_ref[rows, :] + w_ref[a] * o_scatter[r * SUB:(r + 1) * SUB, :]))
            for rows, v in vals:
                y_ref[rows, :] = v

    @pl.when(s < used + EXP_LAG + 1)
    def _():
        @pl.when(s % 2 == 0)
        def _():
            stages(xbuf0, xbuf1, obuf1, obuf0, wg_b0, wu_b0, wg_b1, wu_b1)

        @pl.when(s % 2 == 1)
        def _():
            stages(xbuf1, xbuf0, obuf0, obuf1, wg_b1, wu_b1, wg_b0, wu_b0)


def _experts(tile_exp, used, dest, seg, cnt, w_flat, h2_tiles, wg, wu, wd, layer):
    n_tok = h2_tiles.shape[0] // SUB
    assert TOP_K == SUB and dest.shape[0] == n_tok * TOP_K and w_flat.shape[0] == dest.shape[0] + SUB
    n_tiles = tile_exp.shape[0]

    def wmap(ahead):
        return lambda s, exp, used, *_: (layer, exp[jnp.clip(s - EXP_LAG + ahead, 0, used[0] - 1)], 0, 0)

    buf = pltpu.VMEM((EXP_TM * SUB, LANE), F32)
    gs = pltpu.PrefetchScalarGridSpec(
        num_scalar_prefetch=6,
        grid=(n_tiles + EXP_LAG + 1,),
        in_specs=[pl.BlockSpec(memory_space=pltpu.VMEM),
                  pl.BlockSpec((None, None, D, EXP_D), wmap(1)),
                  pl.BlockSpec((None, None, D, EXP_D), wmap(1)),
                  pl.BlockSpec((None, None, EXP_D, D), wmap(0))],
        out_specs=pl.BlockSpec(memory_space=pltpu.VMEM),
        scratch_shapes=[buf, buf, buf, buf] + [pltpu.VMEM((D, EXP_D), BF16)] * 4 + [
                        pltpu.SMEM((EXP_LEAD + n_tiles * EXP_TM,), jnp.int32)],
    )
    return pl.pallas_call(
        functools.partial(_experts_kernel, n_tok=n_tok),
        grid_spec=gs,
        out_shape=jax.ShapeDtypeStruct(((n_tok + 1) * SUB, LANE), F32),
        compiler_params=_cparams(1, VMEM_BIG),
        name="experts",
    )(tile_exp, used, dest, seg, cnt, w_flat, h2_tiles, wg, wu, wd)


def _dispatch_plan(idx, rank, w, cnt):
    n_tok = idx.shape[1]
    n_tiles = n_tok * TOP_K // EXP_TM + N_EXP
    tiles_e = (cnt + EXP_TM - 1) // EXP_TM
    tile_end = jnp.cumsum(tiles_e)
    seg = (EXP_LEAD + jnp.concatenate([jnp.zeros((1,), jnp.int32), tile_end * EXP_TM])).astype(jnp.int32)
    e_ids = jnp.arange(N_EXP, dtype=jnp.int32)
    dest = jnp.sum(jnp.where(idx[:, :, None] == e_ids, seg[:N_EXP], 0), axis=-1) + rank
    ii = jnp.arange(n_tiles, dtype=jnp.int32)
    used = tile_end[-1]
    tile_exp = jnp.sum(tile_end[None, :] <= jnp.minimum(ii, used - 1)[:, None], axis=1).astype(jnp.int32)
    tile_exp = jnp.minimum(tile_exp, N_EXP - 1)
    w_flat = jnp.concatenate([w.T.reshape(-1), jnp.zeros((SUB,), F32)])
    return tile_exp, used.reshape(1).astype(jnp.int32), dest.T.reshape(-1), seg, w_flat


def _final_kernel(y_ref, x1_ref, m_ref, sg_ref, su_ref, sd_ref, g_ref, b_ref, o_ref):
    x1 = x1_ref[...]
    routed = _from_row_tiles(y_ref, TM2)
    h2 = (x1 * (1.0 + m_ref[0, 4:5, :]) + m_ref[0, 3:4, :]).astype(BF16)
    act = (jax.nn.silu(_dot(h2, sg_ref[...])) * _dot(h2, su_ref[...])).astype(BF16)
    shared = _dot(act, sd_ref[...])
    z = ALPHA * x1 + m_ref[0, 5:6, :] * (routed + shared)
    o_ref[...] = _layernorm(z, g_ref[...], b_ref[...])


def _final(y_tiles, x1, mod, sg, su, sd, ln_g, ln_b, per_batch):
    rows = x1.shape[0]
    full = lambda a: pl.BlockSpec(a.shape, lambda i: (0,) * a.ndim)
    return pl.pallas_call(
        _final_kernel,
        grid=(rows // TM2,),
        in_specs=[pl.BlockSpec((TM2 * SUB, LANE), lambda i: (i, 0)),
                  pl.BlockSpec((TM2, D), lambda i: (i, 0)),
                  pl.BlockSpec((1, 6, D), lambda i: (_mod_row(i, per_batch, TM2), 0, 0)),
                  full(sg), full(su), full(sd), full(ln_g), full(ln_b)],
        out_specs=pl.BlockSpec((TM2, D), lambda i: (i, 0)),
        out_shape=jax.ShapeDtypeStruct((rows, D), F32),
        compiler_params=_cparams(1),
        name="final",
    )(y_tiles, x1, mod, sg, su, sd, ln_g, ln_b)


def _rope_tables(rot_dim, lane_lo, n_rot, n_rows):
    nf = rot_dim // 4
    t = jnp.arange(n_rows)
    r = (t // GRID_W).astype(F32)
    col = (t % GRID_W).astype(F32)
    freqs = ROPE_BASE ** (-jnp.arange(nf, dtype=F32) / nf)
    lane = jnp.arange(LANE)
    j = (lane - lane_lo) % rot_dim
    in_rot = jnp.logical_and(lane >= lane_lo, lane < lane_lo + n_rot * rot_dim)
    half = j // (2 * nf)
    second = (j % (2 * nf)) >= nf
    f = freqs[j % nf]
    ang = jnp.where(half[None, :] == 0, r[:, None], col[:, None]) * f[None, :]
    cos = jnp.where(in_rot[None, :], jnp.cos(ang), 1.0)
    sin = jnp.where(in_rot[None, :], jnp.sin(ang), 0.0)
    sp = jnp.where(second[None, :], sin, 0.0)
    sm = jnp.where(second[None, :], 0.0, -sin)
    ident = lambda v: jnp.full((TM, LANE), v, F32)
    return (jnp.concatenate([ident(1.0), cos], axis=0), jnp.concatenate([ident(0.0), sp], axis=0),
            jnp.concatenate([ident(0.0), sm], axis=0))


def _even_weights(w_in, q_norm, w_uq, kv_norm, w_uk, w_uv, w_out):
    z = lambda n: jnp.zeros((D, n), F32)
    mla_in = MLA_QR + MLA_KVR + MLA_ROPE
    w_in_p = jnp.concatenate([w_in[:, :MLA_QR + MLA_KVR], z(MLA_NOPE), w_in[:, MLA_QR + MLA_KVR:mla_in],
                              z(MLA_HP - MLA_NOPE - MLA_ROPE), w_in[:, mla_in:]], axis=1)
    uq = w_uq.reshape(MLA_QR, MLA_HEADS, MLA_NOPE + MLA_ROPE)
    uq = jnp.pad(uq, ((0, 0), (0, 0), (0, MLA_HP - MLA_NOPE - MLA_ROPE))).reshape(MLA_QR, MLA_HEADS * MLA_HP)
    uk = w_uk.reshape(MLA_KVR, MLA_HEADS, MLA_NOPE)
    uk = jnp.pad(uk, ((0, 0), (0, 0), (0, MLA_HP - MLA_NOPE))).reshape(MLA_KVR, MLA_HEADS * MLA_HP)
    return {"w_in": w_in_p.astype(BF16), "q_norm": q_norm.reshape(1, -1), "kv_norm": kv_norm.reshape(1, -1),
            "w_uq": uq.astype(BF16), "w_uk": uk.astype(BF16), "w_uv": w_uv.astype(BF16), "w_out": w_out.astype(BF16)}


def kernel(x_prompt, x_sample, cache_mla_ckv, cache_mla_kpe, state_ret_fwd, state_ret_bwd, cache_swa_k, cache_swa_v, c, c_ctx, w_mod, b_mod, ln1_g, ln1_b, ln2_g, ln2_b, mla_ret_w_in, mla_q_norm, mla_w_uq, mla_kv_norm, mla_w_uk, mla_w_uv, ret_decay_fwd, ret_decay_bwd, ret_gn_g, even_w_out, swa_w_in, swa_sink, swa_w_out, moe_router, moe_router_bias, moe_w_gate, moe_w_up, moe_w_down, shared_w_gate, shared_w_up, shared_w_down):
    n_p, seq_p, _ = x_prompt.shape
    n_s, seq_s, _ = x_sample.shape
    past = cache_mla_ckv.shape[2]
    groups = [
        dict(x=x_prompt.reshape(n_p * seq_p, D), nb=n_p, seq=seq_p, per_batch=None),
        dict(x=x_sample.reshape(n_s * seq_s, D), nb=n_s, seq=seq_s, per_batch=seq_s // TM),
    ]
    cond8 = jnp.zeros((8, D), F32).at[0].set(c_ctx).at[1:1 + n_s].set(c)
    mods = _modulation(cond8, w_mod, b_mod).reshape(DEPTH, 8, 6, D)
    tabs_mla = _rope_tables(MLA_ROPE, MLA_NOPE, 1, seq_s)
    tabs_swa = _rope_tables(SWA_HD, 0, LANE // SWA_HD, seq_s)

    outs = {k: [] for k in ("ckv", "kpe", "rf", "rb", "sk", "sv")}
    for l in range(DEPTH):
        mod = mods[l]
        r_pad = jnp.pad(moe_router[l], ((0, 0), (0, LANE - N_EXP)))
        r_hi = r_pad.astype(BF16)
        r_lo = (r_pad - r_hi.astype(F32)).astype(BF16)
        bias_col = moe_router_bias[l].reshape(N_EXP, 1)
        sg, su, sd = (shared_w_gate[l].astype(BF16), shared_w_up[l].astype(BF16), shared_w_down[l].astype(BF16))
        if l % 2 == 0:
            e = l // 2
            ew = _even_weights(mla_ret_w_in[e], mla_q_norm[e], mla_w_uq[e], mla_kv_norm[e], mla_w_uk[e],
                               mla_w_uv[e], even_w_out[e])
            kpe_ctx = jnp.pad(cache_mla_kpe[:, e].reshape(n_s * past, MLA_ROPE),
                              ((0, 0), (MLA_NOPE, MLA_HP - MLA_NOPE - MLA_ROPE)))
            ctx_kv = _mla_ctx(cache_mla_ckv[:, e].reshape(n_s * past, MLA_KVR), kpe_ctx, ew)
            gn = ret_gn_g[e].reshape(1, RET_W)
        else:
            o = l // 2
            w_in_o = swa_w_in[o].astype(BF16)
            w_out_o = swa_w_out[o].astype(BF16)
            ctx_swa = (cache_swa_k[:, o].reshape(n_s * past, SWA_KV).astype(BF16),
                       cache_swa_v[:, o].reshape(n_s * past, SWA_KV).astype(BF16))
        for gi, g in enumerate(groups):
            x, nb, seq, per_batch = g["x"], g["nb"], g["seq"], g["per_batch"]
            is_sample = gi == 1
            if l % 2 == 0:
                proj = _modmm(x, mod, ew["w_in"], per_batch)
                q, k, v, ckv, kpe = _mla_prep(proj, tabs_mla, ew, per_batch)
                o_mla = _attention(q, k, v, ctx_kv if is_sample else None, None, n_batch=nb, seq=seq,
                                   n_heads=MLA_HEADS, group=1, dqk=MLA_HP, dv=MLA_V,
                                   scale=(MLA_NOPE + MLA_ROPE) ** -0.5, window=0)
                init = (state_ret_fwd[:, e], state_ret_bwd[:, e]) if is_sample else None
                ret = _retention(proj, ret_decay_fwd[e], ret_decay_bwd[e], gn, init, n_batch=nb, seq=seq,
                                 out_state=not is_sample)
                if is_sample:
                    o_ret = ret[0]
                else:
                    o_ret, s_f, s_b = ret
                    outs["ckv"].append(ckv.reshape(nb, seq, MLA_KVR))
                    outs["kpe"].append(kpe[:, MLA_NOPE:MLA_NOPE + MLA_ROPE].reshape(nb, seq, MLA_ROPE))
                    outs["rf"].append(s_f)
                    outs["rb"].append(s_b)
                parts, w_out = [o_mla, o_ret], ew["w_out"]
            else:
                proj = _modmm(x, mod, w_in_o, per_batch)
                q, k, v = _swa_prep(proj, tabs_swa, per_batch)
                o_swa = _attention(q, k, v, ctx_swa if is_sample else None, swa_sink[o], n_batch=nb, seq=seq,
                                   n_heads=SWA_HEADS, group=SWA_HEADS // SWA_KVH, dqk=SWA_HD, dv=SWA_HD,
                                   scale=None, window=WINDOW if is_sample else 0)
                if not is_sample:
                    outs["sk"].append(proj[:, SWA_Q:SWA_Q + SWA_KV].reshape(nb, seq, SWA_KVH, SWA_HD))
                    outs["sv"].append(proj[:, SWA_Q + SWA_KV:].reshape(nb, seq, SWA_KVH, SWA_HD))
                parts, w_out = [o_swa], w_out_o
            x1, h2_tiles, logits = _post_mixer(parts, w_out, x, mod, ln1_g[l].reshape(1, D), ln1_b[l].reshape(1, D),
                                               r_hi, r_lo, per_batch)
            idx, rank, w, cnt = _route(logits, bias_col)
            tile_exp, used, dest, seg, w_flat = _dispatch_plan(idx, rank, w, cnt[:, 0])
            y_tiles = _experts(tile_exp, used, dest, seg, cnt[:, 0], w_flat, h2_tiles,
                               moe_w_gate, moe_w_up, moe_w_down, l)
            g["x"] = _final(y_tiles, x1, mod, sg, su, sd,
                            ln2_g[l].reshape(1, D), ln2_b[l].reshape(1, D), per_batch)
    y_prompt = groups[0]["x"].reshape(n_p, seq_p, D)
    y_sample = groups[1]["x"].reshape(n_s, seq_s, D)
    return (y_prompt, y_sample, jnp.stack(outs["ckv"], axis=1), jnp.stack(outs["kpe"], axis=1),
            jnp.stack(outs["rf"], axis=1), jnp.stack(outs["rb"], axis=1),
            jnp.stack(outs["sk"], axis=1), jnp.stack(outs["sv"], axis=1))
```

```python
import functools

import jax
import jax.numpy as jnp
from jax import lax
from jax.experimental import pallas as pl
from jax.experimental.pallas import tpu as pltpu

F32 = jnp.float32
BF16 = jnp.bfloat16

D = 1024
DEPTH = 4
GRID_W = 64
ALPHA = (2.0 * DEPTH) ** 0.25
LN_EPS = 1e-5
RMS_EPS = 1e-6
ROPE_BASE = 10000.0
NEG_INF = -1e30
MLA_HEADS = 8
MLA_NOPE = 64
MLA_ROPE = 32
MLA_V = 64
MLA_QR = 256
MLA_KVR = 128
MLA_HP = 128
RET_HEADS = 8
RET_HD = 64
RET_CHUNK = 256
RET_W = RET_HEADS * RET_HD
RET_HPS = 8
EVEN_P = 512 + 4 * RET_W
SWA_HEADS = 16
SWA_KVH = 4
SWA_HD = 64
WINDOW = 128
SWA_Q = SWA_HEADS * SWA_HD
SWA_KV = SWA_KVH * SWA_HD
SWA_SCALE = SWA_HD ** -0.5
assert SWA_SCALE == 0.125
N_EXP = 64
TOP_K = 8
N_GROUPS = 8
TOPK_GROUPS = 4
EXP_D = 256
ROUTED_SCALE = 2.5

TM = 256
TM2 = 512
SUB = 8
LANE = 128
EXP_TM = 256
VMEM_BIG = 56 * 1024 * 1024


def _cparams(n_axes, vmem=None):
    return pltpu.CompilerParams(dimension_semantics=("arbitrary",) * n_axes, vmem_limit_bytes=vmem)


def _dot(a, b):
    return jnp.dot(a, b, preferred_element_type=F32)


def _dot_nt(a, b):
    return lax.dot_general(a, b, (((1,), (1,)), ((), ())), preferred_element_type=F32)


def _dot_tn(a, b):
    return lax.dot_general(a, b, (((0,), (0,)), ((), ())), preferred_element_type=F32)


def _layernorm(z, g, b):
    mu = jnp.mean(z, axis=-1, keepdims=True)
    zc = z - mu
    var = jnp.mean(zc * zc, axis=-1, keepdims=True)
    return zc * lax.rsqrt(var + LN_EPS) * g + b


def _rmsnorm(x, g):
    return x * lax.rsqrt(jnp.mean(x * x, axis=-1, keepdims=True) + RMS_EPS) * g


def _rope(x, c, sp, sm, shift):
    w = x.shape[-1]
    return x * c + pltpu.roll(x, shift, 1) * sp + pltpu.roll(x, w - shift, 1) * sm


def _mod_kernel(c_ref, w_ref, b_ref, o_ref):
    s = jax.nn.silu(c_ref[...]).astype(BF16)
    o_ref[0] = _dot(s, w_ref[0].astype(BF16)) + b_ref[0]


def _modulation(cond8, w_mod, b_mod):
    nt = 4
    tn = 6 * D // nt
    return pl.pallas_call(
        _mod_kernel,
        grid=(DEPTH, nt),
        in_specs=[pl.BlockSpec((8, D), lambda l, j: (0, 0)),
                  pl.BlockSpec((1, D, tn), lambda l, j: (l, 0, j)),
                  pl.BlockSpec((1, 1, tn), lambda l, j: (l, 0, j))],
        out_specs=pl.BlockSpec((1, 8, tn), lambda l, j: (l, 0, j)),
        out_shape=jax.ShapeDtypeStruct((DEPTH, 8, 6 * D), F32),
        compiler_params=_cparams(2, 40 * 1024 * 1024),
        name="modulation",
    )(cond8, w_mod, b_mod.reshape(DEPTH, 1, 6 * D))


def _mod_row(i, per_batch, tm=TM):
    return 0 if per_batch is None else 1 + i // (per_batch * TM // tm)


def _modmm_kernel(x_ref, m_ref, w_ref, o_ref, *, shift_i, scale_i):
    h = x_ref[...] * (1.0 + m_ref[0, scale_i:scale_i + 1, :]) + m_ref[0, shift_i:shift_i + 1, :]
    o_ref[...] = _dot(h.astype(BF16), w_ref[...])


def _modmm(x, mod, w, per_batch):
    rows = x.shape[0]
    n = w.shape[1]
    return pl.pallas_call(
        functools.partial(_modmm_kernel, shift_i=0, scale_i=1),
        grid=(rows // TM2,),
        in_specs=[pl.BlockSpec((TM2, D), lambda i: (i, 0)),
                  pl.BlockSpec((1, 6, D), lambda i: (_mod_row(i, per_batch, TM2), 0, 0)),
                  pl.BlockSpec((D, n), lambda i: (0, 0))],
        out_specs=pl.BlockSpec((TM2, n), lambda i: (i, 0)),
        out_shape=jax.ShapeDtypeStruct((rows, n), F32),
        compiler_params=_cparams(1, 40 * 1024 * 1024),
        name="modmm",
    )(x, mod, w)


def _mla_prep_kernel(p_ref, c_ref, sp_ref, sm_ref, qn_ref, kvn_ref, wuq_ref, wuk_ref, wuv_ref,
                     q_ref, k_ref, v_ref, ckv_ref, kpe_ref):
    p = p_ref[...]
    c, sp, sm = c_ref[...], sp_ref[...], sm_ref[...]
    qn = _rmsnorm(p[:, 0:MLA_QR], qn_ref[...])
    q = _dot(qn.astype(BF16), wuq_ref[...])
    ckv = _rmsnorm(p[:, MLA_QR:MLA_QR + MLA_KVR], kvn_ref[...])
    ckv_ref[...] = ckv
    kpe = _rope(p[:, MLA_QR + MLA_KVR:512], c, sp, sm, MLA_ROPE // 4)
    kpe_ref[...] = kpe
    ckv_b = ckv.astype(BF16)
    kn = _dot(ckv_b, wuk_ref[...])
    for h in range(MLA_HEADS):
        sl = slice(MLA_HP * h, MLA_HP * (h + 1))
        q_ref[:, sl] = _rope(q[:, sl], c, sp, sm, MLA_ROPE // 4).astype(BF16)
        k_ref[:, sl] = (kn[:, sl] + kpe).astype(BF16)
    v_ref[...] = _dot(ckv_b, wuv_ref[...]).astype(BF16)


def _mla_prep(proj, tabs, ew, per_batch):
    rows = proj.shape[0]
    c, sp, sm = tabs

    def tab_idx(i):
        return (0 if per_batch is None else 1 + i % per_batch, 0)

    tab_spec = pl.BlockSpec((TM, LANE), tab_idx)
    full = lambda a: pl.BlockSpec(a.shape, lambda i: (0,) * a.ndim)
    return pl.pallas_call(
        _mla_prep_kernel,
        grid=(rows // TM,),
        in_specs=[pl.BlockSpec((TM, 512), lambda i: (i, 0)), tab_spec, tab_spec, tab_spec,
                  full(ew["q_norm"]), full(ew["kv_norm"]), full(ew["w_uq"]), full(ew["w_uk"]), full(ew["w_uv"])],
        out_specs=[pl.BlockSpec((TM, MLA_HEADS * MLA_HP), lambda i: (i, 0)),
                   pl.BlockSpec((TM, MLA_HEADS * MLA_HP), lambda i: (i, 0)),
                   pl.BlockSpec((TM, MLA_HEADS * MLA_V), lambda i: (i, 0)),
                   pl.BlockSpec((TM, MLA_KVR), lambda i: (i, 0)),
                   pl.BlockSpec((TM, LANE), lambda i: (i, 0))],
        out_shape=[jax.ShapeDtypeStruct((rows, MLA_HEADS * MLA_HP), BF16),
                   jax.ShapeDtypeStruct((rows, MLA_HEADS * MLA_HP), BF16),
                   jax.ShapeDtypeStruct((rows, MLA_HEADS * MLA_V), BF16),
                   jax.ShapeDtypeStruct((rows, MLA_KVR), F32),
                   jax.ShapeDtypeStruct((rows, LANE), F32)],
        compiler_params=_cparams(1),
        name="mla_prep",
    )(proj, c, sp, sm, ew["q_norm"], ew["kv_norm"], ew["w_uq"], ew["w_uk"], ew["w_uv"])


def _mla_ctx_kernel(ckv_ref, kpe_ref, wuk_ref, wuv_ref, k_ref, v_ref):
    ckv_b = ckv_ref[...].astype(BF16)
    kn = _dot(ckv_b, wuk_ref[...])
    kpe = kpe_ref[...]
    for h in range(MLA_HEADS):
        sl = slice(MLA_HP * h, MLA_HP * (h + 1))
        k_ref[:, sl] = (kn[:, sl] + kpe).astype(BF16)
    v_ref[...] = _dot(ckv_b, wuv_ref[...]).astype(BF16)


def _mla_ctx(ckv, kpe_pad, ew):
    rows = ckv.shape[0]
    full = lambda a: pl.BlockSpec(a.shape, lambda i: (0,) * a.ndim)
    return pl.pallas_call(
        _mla_ctx_kernel,
        grid=(rows // TM,),
        in_specs=[pl.BlockSpec((TM, MLA_KVR), lambda i: (i, 0)), pl.BlockSpec((TM, LANE), lambda i: (i, 0)),
                  full(ew["w_uk"]), full(ew["w_uv"])],
        out_specs=[pl.BlockSpec((TM, MLA_HEADS * MLA_HP), lambda i: (i, 0)),
                   pl.BlockSpec((TM, MLA_HEADS * MLA_V), lambda i: (i, 0))],
        out_shape=[jax.ShapeDtypeStruct((rows, MLA_HEADS * MLA_HP), BF16),
                   jax.ShapeDtypeStruct((rows, MLA_HEADS * MLA_V), BF16)],
        compiler_params=_cparams(1),
        name="mla_ctx",
    )(ckv, kpe_pad, ew["w_uk"], ew["w_uv"])


def _attn_kernel(*refs, n_heads, group, dqk, dv, scale, has_ctx, has_sink, window, tq, seq):
    refs = list(refs)
    sink_ref = refs.pop(0) if has_sink else None
    q_ref, k_ref, v_ref = refs[:3]
    kc_ref, vc_ref = (refs[3], refs[4]) if has_ctx else (None, None)
    o_ref = refs[-1]
    i = pl.program_id(1)
    if window:
        kw = tq + 2 * window
        start = pl.multiple_of(jnp.clip(i * tq - window, 0, seq - kw), LANE)
        qpos = i * tq + lax.broadcasted_iota(jnp.int32, (tq, kw), 0)
        kpos = start + lax.broadcasted_iota(jnp.int32, (tq, kw), 1)
        valid = jnp.abs(qpos - kpos) <= window
    def scores(q, k):
        s = _dot_nt(q, k)
        return s if scale is None else s * scale

    for h in range(n_heads):
        hk = h // group
        q = q_ref[:, h * dqk:(h + 1) * dqk]
        if window:
            k = k_ref[pl.ds(start, kw), hk * dqk:(hk + 1) * dqk]
            v = v_ref[pl.ds(start, kw), hk * dv:(hk + 1) * dv]
            s = jnp.where(valid, scores(q, k), NEG_INF)
        else:
            k = k_ref[:, hk * dqk:(hk + 1) * dqk]
            v = v_ref[:, hk * dv:(hk + 1) * dv]
            s = scores(q, k)
        m = jnp.max(s, axis=-1, keepdims=True)
        if has_ctx:
            sc = scores(q, kc_ref[:, hk * dqk:(hk + 1) * dqk])
            m = jnp.maximum(m, jnp.max(sc, axis=-1, keepdims=True))
        if has_sink:
            sk = sink_ref[h]
            m = jnp.maximum(m, sk)
        p = jnp.exp(s - m)
        l = jnp.sum(p, axis=-1, keepdims=True)
        o = _dot(p.astype(BF16), v)
        if has_ctx:
            pc = jnp.exp(sc - m)
            l = l + jnp.sum(pc, axis=-1, keepdims=True)
            o = o + _dot(pc.astype(BF16), vc_ref[:, hk * dv:(hk + 1) * dv])
        if has_sink:
            l = l + jnp.exp(sk - m)
        o_ref[:, h * dv:(h + 1) * dv] = (o / l).astype(o_ref.dtype)


def _attention(q, k, v, ctx, sink, *, n_batch, seq, n_heads, group, dqk, dv, scale, window):
    tq = TM
    nq = seq // tq
    n_kv = n_heads // group
    in_specs = []
    args = []
    if sink is not None:
        in_specs.append(pl.BlockSpec(memory_space=pltpu.SMEM))
        args.append(sink)
    in_specs += [pl.BlockSpec((tq, n_heads * dqk), lambda b, i: (b * nq + i, 0)),
                 pl.BlockSpec((seq, n_kv * dqk), lambda b, i: (b, 0)),
                 pl.BlockSpec((seq, n_kv * dv), lambda b, i: (b, 0))]
    args += [q, k, v]
    if ctx is not None:
        kc, vc = ctx
        sc = kc.shape[0] // n_batch
        in_specs += [pl.BlockSpec((sc, n_kv * dqk), lambda b, i: (b, 0)),
                     pl.BlockSpec((sc, n_kv * dv), lambda b, i: (b, 0))]
        args += [kc, vc]
    kern = functools.partial(_attn_kernel, n_heads=n_heads, group=group, dqk=dqk, dv=dv, scale=scale,
                             has_ctx=ctx is not None, has_sink=sink is not None, window=window, tq=tq, seq=seq)
    return pl.pallas_call(
        kern,
        grid=(n_batch, nq),
        in_specs=in_specs,
        out_specs=pl.BlockSpec((tq, n_heads * dv), lambda b, i: (b * nq + i, 0)),
        out_shape=jax.ShapeDtypeStruct((n_batch * seq, n_heads * dv), BF16),
        compiler_params=_cparams(2, 40 * 1024 * 1024),
        name="attention",
    )(*args)


def _ret_kernel(*refs, seq, has_init, out_state):
    refs = list(refs)
    df_ref, db_ref, rq_ref, rk_ref, rv_ref, rg_ref, gn_ref = refs[:7]
    pos = 7
    if has_init:
        s0f_ref, s0b_ref = refs[pos], refs[pos + 1]
        pos += 2
    o_ref = refs[pos]
    pos += 1
    if out_state:
        sf_ref, sb_ref = refs[pos], refs[pos + 1]
        pos += 2
    of_scr, ob_scr = refs[pos], refs[pos + 1]

    grp = pl.program_id(1)
    n_chunks = seq // RET_CHUNK
    idx_c = lax.broadcasted_iota(jnp.int32, (RET_CHUNK, 1), 0).astype(F32)
    diff = (lax.broadcasted_iota(jnp.int32, (RET_CHUNK, RET_CHUNK), 0)
            - lax.broadcasted_iota(jnp.int32, (RET_CHUNK, RET_CHUNK), 1)).astype(F32)
    for hh in range(RET_HPS):
        h = RET_HPS * grp + hh
        hs = slice(RET_HD * hh, RET_HD * (hh + 1))
        rows = [slice(RET_CHUNK * ci, RET_CHUNK * (ci + 1)) for ci in range(n_chunks)]
        qs = [rq_ref[r, hs].astype(BF16) for r in rows]
        ks = [rk_ref[r, hs] * (RET_HD ** -0.5) for r in rows]
        vs = [rv_ref[r, hs].astype(BF16) for r in rows]
        qk = [_dot_nt(q, k.astype(BF16)) for q, k in zip(qs, ks)]
        for fwd in (True, False):
            d = jnp.full((1, 1), (df_ref if fwd else db_ref)[h], F32)
            lg = jnp.minimum(d, 0.0) - jnp.log1p(jnp.exp(-jnp.abs(d)))
            dd = diff if fwd else -diff
            mask = jnp.where(dd >= 0, jnp.exp(lg * jnp.maximum(dd, 0.0)), 0.0)
            if fwd:
                q_dec = jnp.exp(lg * (idx_c + 1.0))
                k_dec = jnp.exp(lg * (RET_CHUNK - 1.0 - idx_c))
            else:
                q_dec = jnp.exp(lg * (RET_CHUNK - idx_c))
                k_dec = jnp.exp(lg * idx_c)
            c_dec = jnp.exp(lg * RET_CHUNK)
            scr = of_scr if fwd else ob_scr
            state = (s0f_ref if fwd else s0b_ref)[0, hh] if has_init else None
            for ci in (range(n_chunks) if fwd else reversed(range(n_chunks))):
                o = _dot((qk[ci] * mask).astype(BF16), vs[ci])
                kv = _dot_tn((ks[ci] * k_dec).astype(BF16), vs[ci])
                if state is not None:
                    o = o + _dot(qs[ci], state.astype(BF16)) * q_dec
                    kv = state * c_dec + kv
                scr[rows[ci], hs] = o
                state = kv
            if out_state:
                (sf_ref if fwd else sb_ref)[0, hh] = state

    def head_norm(x):
        mu = jnp.mean(x, axis=-1, keepdims=True)
        xc = x - mu
        return xc * lax.rsqrt(jnp.mean(xc * xc, axis=-1, keepdims=True) + LN_EPS)

    for hh in range(RET_HPS):
        hs = slice(RET_HD * hh, RET_HD * (hh + 1))
        o = head_norm(of_scr[:, hs]) + head_norm(ob_scr[:, hs])
        o_ref[:, hs] = (o * gn_ref[:, hs] * jax.nn.silu(rg_ref[:, hs])).astype(o_ref.dtype)


def _retention(proj, dec_f, dec_b, gn, init, *, n_batch, seq, out_state):
    groups = RET_HEADS // RET_HPS
    width = RET_HPS * RET_HD
    col0 = 512 // width

    def col_spec(k):
        return pl.BlockSpec((seq, width), lambda b, p: (b, col0 + k * groups + p))

    smem = pl.BlockSpec(memory_space=pltpu.SMEM)
    st_spec = pl.BlockSpec((1, RET_HPS, RET_HD, RET_HD), lambda b, p: (b, p, 0, 0))
    in_specs = [smem, smem, col_spec(0), col_spec(1), col_spec(2), col_spec(3),
                pl.BlockSpec((1, width), lambda b, p: (0, p))]
    args = [dec_f, dec_b, proj, proj, proj, proj, gn]
    if init is not None:
        in_specs += [st_spec, st_spec]
        args += list(init)
    out_specs = [pl.BlockSpec((seq, width), lambda b, p: (b, p))]
    out_shape = [jax.ShapeDtypeStruct((n_batch * seq, RET_W), BF16)]
    if out_state:
        out_specs += [st_spec, st_spec]
        out_shape += [jax.ShapeDtypeStruct((n_batch, RET_HEADS, RET_HD, RET_HD), F32)] * 2
    kern = functools.partial(_ret_kernel, seq=seq, has_init=init is not None, out_state=out_state)
    return pl.pallas_call(
        kern,
        grid=(n_batch, groups),
        in_specs=in_specs,
        out_specs=out_specs,
        out_shape=out_shape,
        scratch_shapes=[pltpu.VMEM((seq, width), F32), pltpu.VMEM((seq, width), F32)],
        compiler_params=_cparams(2),
        name="retention",
    )(*args)


def _swa_prep_kernel(p_ref, c_ref, sp_ref, sm_ref, q_ref, k_ref, v_ref):
    c, sp, sm = c_ref[...], sp_ref[...], sm_ref[...]
    for j in range(SWA_Q // LANE):
        sl = slice(LANE * j, LANE * (j + 1))
        q_ref[:, sl] = (_rope(p_ref[:, sl], c, sp, sm, SWA_HD // 4) * SWA_SCALE).astype(BF16)
    for j in range(SWA_KV // LANE):
        sl = slice(LANE * j, LANE * (j + 1))
        k_ref[:, sl] = _rope(p_ref[:, SWA_Q + LANE * j:SWA_Q + LANE * (j + 1)], c, sp, sm, SWA_HD // 4).astype(BF16)
    v_ref[...] = p_ref[:, SWA_Q + SWA_KV:].astype(BF16)


def _swa_prep(proj, tabs, per_batch):
    rows = proj.shape[0]
    c, sp, sm = tabs

    def tab_idx(i):
        return (0 if per_batch is None else 1 + i % per_batch, 0)

    tab_spec = pl.BlockSpec((TM, LANE), tab_idx)
    return pl.pallas_call(
        _swa_prep_kernel,
        grid=(rows // TM,),
        in_specs=[pl.BlockSpec((TM, SWA_Q + 2 * SWA_KV), lambda i: (i, 0)), tab_spec, tab_spec, tab_spec],
        out_specs=[pl.BlockSpec((TM, SWA_Q), lambda i: (i, 0)),
                   pl.BlockSpec((TM, SWA_KV), lambda i: (i, 0)),
                   pl.BlockSpec((TM, SWA_KV), lambda i: (i, 0))],
        out_shape=[jax.ShapeDtypeStruct((rows, SWA_Q), BF16),
                   jax.ShapeDtypeStruct((rows, SWA_KV), BF16),
                   jax.ShapeDtypeStruct((rows, SWA_KV), BF16)],
        compiler_params=_cparams(1),
        name="swa_prep",
    )(proj, c, sp, sm)


def _to_row_tiles(ref, x):
    rows = x.shape[0]
    for s in range(D // LANE):
        ref[pl.ds(s, rows, stride=SUB), :] = x[:, LANE * s:LANE * (s + 1)]


def _from_row_tiles(ref, rows):
    return jnp.concatenate([ref[pl.ds(s, rows, stride=SUB), :] for s in range(D // LANE)], axis=1)


def _post_kernel(*refs, n_parts):
    a_refs = refs[:n_parts]
    (w_ref, x_ref, m_ref, g_ref, b_ref, rh_ref, rl_ref, bias_ref,
     x1_ref, h2_ref, idx_ref, rank_ref, wt_ref, cnt_ref, carry_ref) = refs[n_parts:]
    out = None
    off = 0
    for a_ref in a_refs:
        kk = a_ref.shape[1]
        part = _dot(a_ref[...], w_ref[off:off + kk, :])
        out = part if out is None else out + part
        off += kk
    z = ALPHA * x_ref[...] + m_ref[0, 2:3, :] * out
    x1 = _layernorm(z, g_ref[...], b_ref[...])
    x1_ref[...] = x1
    h2 = x1 * (1.0 + m_ref[0, 4:5, :]) + m_ref[0, 3:4, :]
    _to_row_tiles(h2_ref, h2)
    hi = h2.astype(BF16)
    lo = (h2 - hi.astype(F32)).astype(BF16)
    rh = rh_ref[...]
    logits = _dot(hi, rh) + _dot(lo, rh) + _dot(hi, rl_ref[...])
    _route_tile(logits, bias_ref, idx_ref, rank_ref, wt_ref, cnt_ref, carry_ref)


def _post_mixer(parts, w_out, x, mod, ln_g, ln_b, r_hi, r_lo, bias_col, per_batch):
    rows = x.shape[0]
    full = lambda a: pl.BlockSpec(a.shape, lambda i: (0,) * a.ndim)
    in_specs = [pl.BlockSpec((TM, a.shape[1]), lambda i: (i, 0)) for a in parts]
    in_specs += [full(w_out), pl.BlockSpec((TM, D), lambda i: (i, 0)),
                 pl.BlockSpec((1, 6, D), lambda i: (_mod_row(i, per_batch), 0, 0)),
                 full(ln_g), full(ln_b), full(r_hi), full(r_lo), full(bias_col)]
    row_spec = pl.BlockSpec((TOP_K, TM), lambda i: (0, i))
    return pl.pallas_call(
        functools.partial(_post_kernel, n_parts=len(parts)),
        grid=(rows // TM,),
        in_specs=in_specs,
        out_specs=[pl.BlockSpec((TM, D), lambda i: (i, 0)),
                   pl.BlockSpec((TM * SUB, LANE), lambda i: (i, 0)),
                   row_spec, row_spec, row_spec, pl.BlockSpec((N_EXP, LANE), lambda i: (0, 0))],
        out_shape=[jax.ShapeDtypeStruct((rows, D), F32),
                   jax.ShapeDtypeStruct((rows * SUB, LANE), F32),
                   jax.ShapeDtypeStruct((TOP_K, rows), jnp.int32),
                   jax.ShapeDtypeStruct((TOP_K, rows), jnp.int32),
                   jax.ShapeDtypeStruct((TOP_K, rows), F32),
                   jax.ShapeDtypeStruct((N_EXP, LANE), jnp.int32)],
        scratch_shapes=[pltpu.VMEM((N_EXP, LANE), F32)],
        compiler_params=_cparams(1),
        name="post_mixer",
    )(*parts, w_out, x, mod, ln_g, ln_b, r_hi, r_lo, bias_col)


def _route_tile(logits, bias_ref, idx_ref, rank_ref, w_ref, cnt_ref, carry_ref):
    i = pl.program_id(0)

    @pl.when(i == 0)
    def _():
        carry_ref[...] = jnp.zeros_like(carry_ref)

    t = logits.shape[0]
    gsz = N_EXP // N_GROUPS
    scores = jax.nn.sigmoid(logits.T[:N_EXP])
    sel = scores + bias_ref[...]
    g3 = sel.reshape(N_GROUPS, gsz, t)
    sub_iota = lax.broadcasted_iota(jnp.int32, g3.shape, 1)
    m1 = jnp.max(g3, axis=1)
    first = jnp.min(jnp.where(g3 == m1[:, None, :], sub_iota, gsz), axis=1)
    m2 = jnp.max(jnp.where(sub_iota == first[:, None, :], -jnp.inf, g3), axis=1)
    grp = m1 + m2
    g_iota = lax.broadcasted_iota(jnp.int32, grp.shape, 0)
    gmask = jnp.zeros(grp.shape, jnp.bool_)
    for _ in range(TOPK_GROUPS):
        gm = jnp.max(grp, axis=0, keepdims=True)
        gi = jnp.min(jnp.where(grp == gm, g_iota, N_GROUPS), axis=0, keepdims=True)
        hit = g_iota == gi
        gmask = jnp.logical_or(gmask, hit)
        grp = jnp.where(hit, -jnp.inf, grp)
    emask = jnp.broadcast_to(gmask[:, None, :], g3.shape).reshape(N_EXP, t)
    cur = jnp.where(emask, sel, NEG_INF)
    e_iota = lax.broadcasted_iota(jnp.int32, cur.shape, 0)
    hits = []
    member = jnp.zeros(cur.shape, F32)
    for _ in range(TOP_K):
        cm = jnp.max(cur, axis=0, keepdims=True)
        ci = jnp.min(jnp.where(cur == cm, e_iota, N_EXP), axis=0, keepdims=True)
        hit = e_iota == ci
        hits.append((hit, ci))
        member = member + hit.astype(F32)
        cur = jnp.where(hit, -jnp.inf, cur)
    tri = (lax.broadcasted_iota(jnp.int32, (t, t), 0) < lax.broadcasted_iota(jnp.int32, (t, t), 1)).astype(BF16)
    before = _dot(member.astype(BF16), tri) + carry_ref[:, 0:1]
    ws = [jnp.sum(jnp.where(hit, scores, 0.0), axis=0, keepdims=True) for hit, _ in hits]
    wsum = ws[0]
    for w in ws[1:]:
        wsum = wsum + w
    for k, (hit, ci) in enumerate(hits):
        idx_ref[k:k + 1, :] = ci
        rank_ref[k:k + 1, :] = jnp.sum(jnp.where(hit, before, 0.0), axis=0, keepdims=True).astype(jnp.int32)
        w_ref[k:k + 1, :] = ws[k] / wsum * ROUTED_SCALE
    total = carry_ref[...] + jnp.sum(member, axis=1, keepdims=True)
    carry_ref[...] = total
    cnt_ref[...] = total.astype(jnp.int32)


EXP_LAG = 2
EXP_LEAD = (EXP_LAG + 1) * EXP_TM


def _experts_kernel(exp_ref, used_ref, dest_ref, seg_ref, cnt_ref, w_ref,
                    x_ref, wg_ref, wu_ref, wd_ref, y_ref,
                    xbuf0, xbuf1, obuf0, obuf1, wg_b0, wg_b1, wu_b0, wu_b1, asg, *, n_tok):
    s = pl.program_id(0)
    used = used_ref[0]
    n_assign = dest_ref.shape[0]
    pad_id = n_assign

    @pl.when(s == 0)
    def _():
        y_ref[...] = jnp.zeros_like(y_ref)
        for buf in (xbuf0, xbuf1, obuf0, obuf1, wg_b0, wg_b1, wu_b0, wu_b1):
            buf[...] = jnp.zeros_like(buf)

        def lead(j, c):
            for u in range(SUB):
                asg[j * SUB + u] = pad_id
            return c
        lax.fori_loop(0, EXP_LEAD // SUB, lead, 0)

        def fill(e, c):
            @pl.when(cnt_ref[e] > 0)
            def _():
                last = asg.at[pl.ds(pl.multiple_of(seg_ref[e + 1] - EXP_TM, EXP_TM), EXP_TM)]
                for u in range(EXP_TM):
                    last[u] = pad_id
            return c
        lax.fori_loop(0, N_EXP, fill, 0)

        def invert(j, c):
            for u in range(16):
                a = j * 16 + u
                asg[dest_ref[a]] = a
            return c
        lax.fori_loop(0, n_assign // 16, invert, 0)

    def stages(x_gather, x_mm, o_mm, o_scatter, wg_mm, wu_mm, wg_next, wu_next):
        g_slots = asg.at[pl.ds(jnp.minimum(s + 2, used + EXP_LAG) * EXP_TM, EXP_TM)]
        for r in range(EXP_TM):
            row0 = g_slots[r] & ((n_tok - 1) * SUB)
            x_gather[r * SUB:(r + 1) * SUB, :] = x_ref[pl.ds(pl.multiple_of(row0, SUB), SUB), :]

        x = _from_row_tiles(x_mm, EXP_TM).astype(BF16)
        h = (jax.nn.silu(_dot(x, wg_mm[...])) * _dot(x, wu_mm[...])).astype(BF16)
        _to_row_tiles(o_mm, _dot(h, wd_ref[...].astype(BF16)))
        wg_next[...] = wg_ref[...].astype(BF16)
        wu_next[...] = wu_ref[...].astype(BF16)

        s_slots = asg.at[pl.ds(s * EXP_TM, EXP_TM)]
        for j in range(EXP_TM // SUB):
            vals = []
            for u in range(SUB):
                r = j * SUB + u
                a = s_slots[r]
                rows = pl.ds(pl.multiple_of(a & -SUB, SUB), SUB)
                vals.append((rows, y_ref[rows, :] + w_ref[a] * o_scatter[r * SUB:(r + 1) * SUB, :]))
            for rows, v in vals:
                y_ref[rows, :] = v

    @pl.when(s < used + EXP_LAG + 1)
    def _():
        @pl.when(s % 2 == 0)
        def _():
            stages(xbuf0, xbuf1, obuf1, obuf0, wg_b0, wu_b0, wg_b1, wu_b1)

        @pl.when(s % 2 == 1)
        def _():
            stages(xbuf1, xbuf0, obuf0, obuf1, wg_b1, wu_b1, wg_b0, wu_b0)


def _experts(tile_exp, used, dest, seg, cnt, w_flat, h2_tiles, wg, wu, wd, layer):
    n_tok = h2_tiles.shape[0] // SUB
    assert TOP_K == SUB and dest.shape[0] == n_tok * TOP_K and w_flat.shape[0] == dest.shape[0] + SUB
    n_tiles = tile_exp.shape[0]

    def wmap(ahead):
        return lambda s, exp, used, *_: (layer, exp[jnp.clip(s - EXP_LAG + ahead, 0, used[0] - 1)], 0, 0)

    buf = pltpu.VMEM((EXP_TM * SUB, LANE), F32)
    gs = pltpu.PrefetchScalarGridSpec(
        num_scalar_prefetch=6,
        grid=(n_tiles + EXP_LAG + 1,),
        in_specs=[pl.BlockSpec(memory_space=pltpu.VMEM),
                  pl.BlockSpec((None, None, D, EXP_D), wmap(1)),
                  pl.BlockSpec((None, None, D, EXP_D), wmap(1)),
                  pl.BlockSpec((None, None, EXP_D, D), wmap(0))],
        out_specs=pl.BlockSpec(memory_space=pltpu.VMEM),
        scratch_shapes=[buf, buf, buf, buf] + [pltpu.VMEM((D, EXP_D), BF16)] * 4 + [
                        pltpu.SMEM((EXP_LEAD + n_tiles * EXP_TM,), jnp.int32)],
    )
    return pl.pallas_call(
        functools.partial(_experts_kernel, n_tok=n_tok),
        grid_spec=gs,
        out_shape=jax.ShapeDtypeStruct(((n_tok + 1) * SUB, LANE), F32),
        compiler_params=_cparams(1, VMEM_BIG),
        name="experts",
    )(tile_exp, used, dest, seg, cnt, w_flat, h2_tiles, wg, wu, wd)


def _dispatch_plan(idx, rank, w, cnt):
    n_tok = idx.shape[1]
    n_tiles = n_tok * TOP_K // EXP_TM + N_EXP
    tiles_e = (cnt + EXP_TM - 1) // EXP_TM
    tile_end = jnp.cumsum(tiles_e)
    seg = (EXP_LEAD + jnp.concatenate([jnp.zeros((1,), jnp.int32), tile_end * EXP_TM])).astype(jnp.int32)
    e_ids = jnp.arange(N_EXP, dtype=jnp.int32)
    dest = jnp.sum(jnp.where(idx[:, :, None] == e_ids, seg[:N_EXP], 0), axis=-1) + rank
    ii = jnp.arange(n_tiles, dtype=jnp.int32)
    used = tile_end[-1]
    tile_exp = jnp.sum(tile_end[None, :] <= jnp.minimum(ii, used - 1)[:, None], axis=1).astype(jnp.int32)
    tile_exp = jnp.minimum(tile_exp, N_EXP - 1)
    w_flat = jnp.concatenate([w.T.reshape(-1), jnp.zeros((SUB,), F32)])
    return tile_exp, used.reshape(1).astype(jnp.int32), dest.T.reshape(-1), seg, w_flat


def _final_kernel(y_ref, x1_ref, m_ref, sg_ref, su_ref, sd_ref, g_ref, b_ref, o_ref):
    x1 = x1_ref[...]
    routed = _from_row_tiles(y_ref, TM2)
    h2 = (x1 * (1.0 + m_ref[0, 4:5, :]) + m_ref[0, 3:4, :]).astype(BF16)
    act = (jax.nn.silu(_dot(h2, sg_ref[...])) * _dot(h2, su_ref[...])).astype(BF16)
    shared = _dot(act, sd_ref[...])
    z = ALPHA * x1 + m_ref[0, 5:6, :] * (routed + shared)
    o_ref[...] = _layernorm(z, g_ref[...], b_ref[...])


def _final(y_tiles, x1, mod, sg, su, sd, ln_g, ln_b, per_batch):
    rows = x1.shape[0]
    full = lambda a: pl.BlockSpec(a.shape, lambda i: (0,) * a.ndim)
    return pl.pallas_call(
        _final_kernel,
        grid=(rows // TM2,),
        in_specs=[pl.BlockSpec((TM2 * SUB, LANE), lambda i: (i, 0)),
                  pl.BlockSpec((TM2, D), lambda i: (i, 0)),
                  pl.BlockSpec((1, 6, D), lambda i: (_mod_row(i, per_batch, TM2), 0, 0)),
                  full(sg), full(su), full(sd), full(ln_g), full(ln_b)],
        out_specs=pl.BlockSpec((TM2, D), lambda i: (i, 0)),
        out_shape=jax.ShapeDtypeStruct((rows, D), F32),
        compiler_params=_cparams(1),
        name="final",
    )(y_tiles, x1, mod, sg, su, sd, ln_g, ln_b)


def _rope_tables(rot_dim, lane_lo, n_rot, n_rows):
    nf = rot_dim // 4
    t = jnp.arange(n_rows)
    r = (t // GRID_W).astype(F32)
    col = (t % GRID_W).astype(F32)
    freqs = ROPE_BASE ** (-jnp.arange(nf, dtype=F32) / nf)
    lane = jnp.arange(LANE)
    j = (lane - lane_lo) % rot_dim
    in_rot = jnp.logical_and(lane >= lane_lo, lane < lane_lo + n_rot * rot_dim)
    half = j // (2 * nf)
    second = (j % (2 * nf)) >= nf
    f = freqs[j % nf]
    ang = jnp.where(half[None, :] == 0, r[:, None], col[:, None]) * f[None, :]
    cos = jnp.where(in_rot[None, :], jnp.cos(ang), 1.0)
    sin = jnp.where(in_rot[None, :], jnp.sin(ang), 0.0)
    sp = jnp.where(second[None, :], sin, 0.0)
    sm = jnp.where(second[None, :], 0.0, -sin)
    ident = lambda v: jnp.full((TM, LANE), v, F32)
    return (jnp.concatenate([ident(1.0), cos], axis=0), jnp.concatenate([ident(0.0), sp], axis=0),
            jnp.concatenate([ident(0.0), sm], axis=0))


def _even_weights(w_in, q_norm, w_uq, kv_norm, w_uk, w_uv, w_out):
    z = lambda n: jnp.zeros((D, n), F32)
    mla_in = MLA_QR + MLA_KVR + MLA_ROPE
    w_in_p = jnp.concatenate([w_in[:, :MLA_QR + MLA_KVR], z(MLA_NOPE), w_in[:, MLA_QR + MLA_KVR:mla_in],
                              z(MLA_HP - MLA_NOPE - MLA_ROPE), w_in[:, mla_in:]], axis=1)
    uq = w_uq.reshape(MLA_QR, MLA_HEADS, MLA_NOPE + MLA_ROPE)
    uq = jnp.pad(uq, ((0, 0), (0, 0), (0, MLA_HP - MLA_NOPE - MLA_ROPE))).reshape(MLA_QR, MLA_HEADS * MLA_HP)
    uk = w_uk.reshape(MLA_KVR, MLA_HEADS, MLA_NOPE)
    uk = jnp.pad(uk, ((0, 0), (0, 0), (0, MLA_HP - MLA_NOPE))).reshape(MLA_KVR, MLA_HEADS * MLA_HP)
    return {"w_in": w_in_p.astype(BF16), "q_norm": q_norm.reshape(1, -1), "kv_norm": kv_norm.reshape(1, -1),
            "w_uq": uq.astype(BF16), "w_uk": uk.astype(BF16), "w_uv": w_uv.astype(BF16), "w_out": w_out.astype(BF16)}


def kernel(x_prompt, x_sample, cache_mla_ckv, cache_mla_kpe, state_ret_fwd, state_ret_bwd, cache_swa_k, cache_swa_v, c, c_ctx, w_mod, b_mod, ln1_g, ln1_b, ln2_g, ln2_b, mla_ret_w_in, mla_q_norm, mla_w_uq, mla_kv_norm, mla_w_uk, mla_w_uv, ret_decay_fwd, ret_decay_bwd, ret_gn_g, even_w_out, swa_w_in, swa_sink, swa_w_out, moe_router, moe_router_bias, moe_w_gate, moe_w_up, moe_w_down, shared_w_gate, shared_w_up, shared_w_down):
    n_p, seq_p, _ = x_prompt.shape
    n_s, seq_s, _ = x_sample.shape
    past = cache_mla_ckv.shape[2]
    groups = [
        dict(x=x_prompt.reshape(n_p * seq_p, D), nb=n_p, seq=seq_p, per_batch=None),
        dict(x=x_sample.reshape(n_s * seq_s, D), nb=n_s, seq=seq_s, per_batch=seq_s // TM),
    ]
    cond8 = jnp.zeros((8, D), F32).at[0].set(c_ctx).at[1:1 + n_s].set(c)
    mods = _modulation(cond8, w_mod, b_mod).reshape(DEPTH, 8, 6, D)
    tabs_mla = _rope_tables(MLA_ROPE, MLA_NOPE, 1, seq_s)
    tabs_swa = _rope_tables(SWA_HD, 0, LANE // SWA_HD, seq_s)

    outs = {k: [] for k in ("ckv", "kpe", "rf", "rb", "sk", "sv")}
    for l in range(DEPTH):
        mod = mods[l]
        r_pad = jnp.pad(moe_router[l], ((0, 0), (0, LANE - N_EXP)))
        r_hi = r_pad.astype(BF16)
        r_lo = (r_pad - r_hi.astype(F32)).astype(BF16)
        bias_col = moe_router_bias[l].reshape(N_EXP, 1)
        sg, su, sd = (shared_w_gate[l].astype(BF16), shared_w_up[l].astype(BF16), shared_w_down[l].astype(BF16))
        if l % 2 == 0:
            e = l // 2
            ew = _even_weights(mla_ret_w_in[e], mla_q_norm[e], mla_w_uq[e], mla_kv_norm[e], mla_w_uk[e],
                               mla_w_uv[e], even_w_out[e])
            kpe_ctx = jnp.pad(cache_mla_kpe[:, e].reshape(n_s * past, MLA_ROPE),
                              ((0, 0), (MLA_NOPE, MLA_HP - MLA_NOPE - MLA_ROPE)))
            ctx_kv = _mla_ctx(cache_mla_ckv[:, e].reshape(n_s * past, MLA_KVR), kpe_ctx, ew)
            gn = ret_gn_g[e].reshape(1, RET_W)
        else:
            o = l // 2
            w_in_o = swa_w_in[o].astype(BF16)
            w_out_o = swa_w_out[o].astype(BF16)
            ctx_swa = (cache_swa_k[:, o].reshape(n_s * past, SWA_KV).astype(BF16),
                       cache_swa_v[:, o].reshape(n_s * past, SWA_KV).astype(BF16))
        for gi, g in enumerate(groups):
            x, nb, seq, per_batch = g["x"], g["nb"], g["seq"], g["per_batch"]
            is_sample = gi == 1
            if l % 2 == 0:
                proj = _modmm(x, mod, ew["w_in"], per_batch)
                q, k, v, ckv, kpe = _mla_prep(proj, tabs_mla, ew, per_batch)
                o_mla = _attention(q, k, v, ctx_kv if is_sample else None, None, n_batch=nb, seq=seq,
                                   n_heads=MLA_HEADS, group=1, dqk=MLA_HP, dv=MLA_V,
                                   scale=(MLA_NOPE + MLA_ROPE) ** -0.5, window=0)
                init = (state_ret_fwd[:, e], state_ret_bwd[:, e]) if is_sample else None
                ret = _retention(proj, ret_decay_fwd[e], ret_decay_bwd[e], gn, init, n_batch=nb, seq=seq,
                                 out_state=not is_sample)
                if is_sample:
                    o_ret = ret[0]
                else:
                    o_ret, s_f, s_b = ret
                    outs["ckv"].append(ckv.reshape(nb, seq, MLA_KVR))
                    outs["kpe"].append(kpe[:, MLA_NOPE:MLA_NOPE + MLA_ROPE].reshape(nb, seq, MLA_ROPE))
                    outs["rf"].append(s_f)
                    outs["rb"].append(s_b)
                parts, w_out = [o_mla, o_ret], ew["w_out"]
            else:
                proj = _modmm(x, mod, w_in_o, per_batch)
                q, k, v = _swa_prep(proj, tabs_swa, per_batch)
                o_swa = _attention(q, k, v, ctx_swa if is_sample else None, swa_sink[o], n_batch=nb, seq=seq,
                                   n_heads=SWA_HEADS, group=SWA_HEADS // SWA_KVH, dqk=SWA_HD, dv=SWA_HD,
                                   scale=None, window=WINDOW if is_sample else 0)
                if not is_sample:
                    outs["sk"].append(proj[:, SWA_Q:SWA_Q + SWA_KV].reshape(nb, seq, SWA_KVH, SWA_HD))
                    outs["sv"].append(proj[:, SWA_Q + SWA_KV:].reshape(nb, seq, SWA_KVH, SWA_HD))
                parts, w_out = [o_swa], w_out_o
            x1, h2_tiles, idx, rank, w, cnt = _post_mixer(parts, w_out, x, mod, ln1_g[l].reshape(1, D),
                                                          ln1_b[l].reshape(1, D), r_hi, r_lo, bias_col, per_batch)
            tile_exp, used, dest, seg, w_flat = _dispatch_plan(idx, rank, w, cnt[:, 0])
            y_tiles = _experts(tile_exp, used, dest, seg, cnt[:, 0], w_flat, h2_tiles,
                               moe_w_gate, moe_w_up, moe_w_down, l)
            g["x"] = _final(y_tiles, x1, mod, sg, su, sd,
                            ln2_g[l].reshape(1, D), ln2_b[l].reshape(1, D), per_batch)
    y_prompt = groups[0]["x"].reshape(n_p, seq_p, D)
    y_sample = groups[1]["x"].reshape(n_s, seq_s, D)
    return (y_prompt, y_sample, jnp.stack(outs["ckv"], axis=1), jnp.stack(outs["kpe"], axis=1),
            jnp.stack(outs["rf"], axis=1), jnp.stack(outs["rb"], axis=1),
            jnp.stack(outs["sk"], axis=1), jnp.stack(outs["sv"], axis=1))
```
